```python
import math
import jax
import jax.numpy as jnp
from jax import lax
import numpy as np

D_MODEL = 1024
BATCH = 2
SEQ = 16384
DEPTH = 4

N_MIXERS = 2
RET_HEADS = 4
RET_QK_DIM = D_MODEL // RET_HEADS
RET_V_DIM = 2 * RET_QK_DIM
RET_CHUNK = 128
ROPE_BASE = 10000.0
DIL_PATTERN = ((128, 1), (512, 4), (2048, 16))
N_GROUPS = len(DIL_PATTERN)
HEADS_PER_GROUP = 4
ATT_HEAD_DIM = 128
ATT_Q_BLOCK = 128
D_FF = (7 * D_MODEL) // 2
N_EXPERTS = 8
TOP_K = 2
MOE_BLOCK = 512
ALPHA = (2 * DEPTH) ** 0.25
BETA = (8 * DEPTH) ** -0.25
LN_EPS = 1e-5
N_RET = (DEPTH + 1) // 2
N_ATT = DEPTH // 2

kernel_name = 'hybrid_retention_dilated_attn_moe'


def layer_norm(x, g, b):
    xf = x.astype(jnp.float32)
    mu = jnp.mean(xf, axis=-1, keepdims=True)
    var = jnp.mean(jnp.square(xf - mu), axis=-1, keepdims=True)
    return ((xf - mu) * lax.rsqrt(var + LN_EPS) * g + b).astype(x.dtype)


def rotary(x, pos):
    half = x.shape[-1] // 2
    inv = ROPE_BASE ** (-jnp.arange(half, dtype=jnp.float32) / half)
    ang = pos.astype(jnp.float32)[:, None] * inv[None, :]
    cos = jnp.cos(ang)[None, :, None, :]
    sin = jnp.sin(ang)[None, :, None, :]
    x1 = x[..., :half].astype(jnp.float32)
    x2 = x[..., half:].astype(jnp.float32)
    return jnp.concatenate([x1 * cos - x2 * sin, x2 * cos + x1 * sin], axis=-1).astype(x.dtype)


def retention(h, w_in, gn_gain, w_out):
    B, S, _ = h.shape
    H, dk, dv, C = RET_HEADS, RET_QK_DIM, RET_V_DIM, RET_CHUNK
    nC = S // C
    proj = h @ w_in
    q, k, v, g = jnp.split(proj, [H * dk, 2 * H * dk, 2 * H * dk + H * dv], axis=-1)
    pos = jnp.arange(S)
    q = rotary(q.reshape(B, S, H, dk), pos)
    k = rotary(k.reshape(B, S, H, dk), pos) * (dk ** -0.5)
    v = v.reshape(B, S, H, dv)
    log_gamma = jnp.log(1.0 - 2.0 ** (-5.0 - jnp.arange(H, dtype=jnp.float32)))
    idx = jnp.arange(C, dtype=jnp.float32)
    rel = idx[:, None] - idx[None, :]
    decay_intra = jnp.where(rel >= 0, jnp.exp(log_gamma[:, None, None] * jnp.maximum(rel, 0.0)), 0.0)
    q_dec = jnp.exp(log_gamma[None, :] * (idx[:, None] + 1.0))
    k_dec = jnp.exp(log_gamma[None, :] * (C - 1.0 - idx[:, None]))
    chunk_dec = jnp.exp(log_gamma * C)
    qc = q.reshape(B, nC, C, H, dk)
    kc = k.reshape(B, nC, C, H, dk)
    vc = v.reshape(B, nC, C, H, dv)
    scores = jnp.einsum('bnihd,bnjhd->bnhij', qc, kc).astype(jnp.float32) * decay_intra[None, None]
    intra = jnp.einsum('bnhij,bnjhe->bnihe', scores.astype(vc.dtype), vc)
    xs = (jnp.moveaxis(qc * q_dec[None, None, :, :, None], 1, 0),
          jnp.moveaxis(kc * k_dec[None, None, :, :, None], 1, 0),
          jnp.moveaxis(vc, 1, 0))

    def step(R, inp):
        qn, kn, vn = inp
        out = jnp.einsum('bihd,bhde->bihe', qn, R)
        R = chunk_dec[None, :, None, None] * R + jnp.einsum('bjhd,bjhe->bhde', kn, vn)
        return R, out

    R0 = jnp.zeros((B, H, dk, dv), jnp.float32)
    _, cross = lax.scan(step, R0, xs)
    r = (intra.astype(jnp.float32) + jnp.moveaxis(cross, 0, 1).astype(jnp.float32)).reshape(B, S, H, dv)
    mu = jnp.mean(r, axis=-1, keepdims=True)
    var = jnp.mean(jnp.square(r - mu), axis=-1, keepdims=True)
    normed = ((r - mu) * lax.rsqrt(var + LN_EPS)).reshape(B, S, H * dv) * gn_gain
    return (jax.nn.silu(g) * normed.astype(g.dtype)) @ w_out


def dilated_attention(h, w_qkv, w_out):
    B, S, _ = h.shape
    G, Hg, dh, Q = N_GROUPS, HEADS_PER_GROUP, ATT_HEAD_DIM, ATT_Q_BLOCK
    proj = (h @ w_qkv).reshape(B, S, 3, G, Hg, dh)
    qkv = jnp.transpose(proj, (2, 3, 0, 4, 1, 5))
    qs = [qkv[0, gi] * (dh ** -0.5) for gi in range(G)]
    ks_ = [qkv[1, gi] for gi in range(G)]
    vs = [qkv[2, gi] for gi in range(G)]
    offs = jnp.arange(Q)

    def block(bidx):
        s0 = bidx * Q
        qpos = s0 + offs
        outs, lses = [], []
        for gi, (win, dil) in enumerate(DIL_PATTERN):
            n_keys = win // dil + 1
            kpos = qpos[:, None] - dil * jnp.arange(n_keys)[None, :]
            valid = kpos >= 0
            kidx = jnp.maximum(kpos, 0)
            qb = lax.dynamic_slice_in_dim(qs[gi], s0, Q, axis=2)
            kg = jnp.take(ks_[gi], kidx, axis=2)
            vg = jnp.take(vs[gi], kidx, axis=2)
            logits = jnp.einsum('bhqd,bhqjd->bhqj', qb, kg).astype(jnp.float32)
            logits = jnp.where(valid[None, None], logits, -jnp.inf)
            lse = jax.nn.logsumexp(logits, axis=-1)
            p = jnp.exp(logits - lse[..., None])
            outs.append(jnp.einsum('bhqj,bhqjd->bhqd', p.astype(vg.dtype), vg))
            lses.append(lse)
        wgt = jax.nn.softmax(jnp.stack(lses, axis=0), axis=0)
        o = jnp.sum(wgt[..., None] * jnp.stack(outs, axis=0).astype(jnp.float32), axis=0)
        return o.astype(h.dtype)

    o = lax.map(block, jnp.arange(S // Q))
    o = jnp.transpose(o, (1, 0, 3, 2, 4)).reshape(B, S, Hg * dh)
    return o @ w_out


def swiglu(h, w_gate, w_up, w_down):
    return (jax.nn.silu(h @ w_gate) * (h @ w_up)) @ w_down


def moe_swiglu(h, w_router, w_gate, w_up, w_down):
    B, S, D = h.shape
    T = B * S
    xt = h.reshape(T, D)
    logits = (xt @ w_router).astype(jnp.float32)
    top_val, top_idx = lax.top_k(logits, TOP_K)
    gates = jax.nn.softmax(top_val, axis=-1)
    A = T * TOP_K
    e_flat = top_idx.reshape(A)
    tok_flat = jnp.arange(A, dtype=jnp.int32) // TOP_K
    g_flat = gates.reshape(A)
    order = jnp.argsort(e_flat)
    e_sorted = e_flat[order]
    counts = jnp.bincount(e_flat, length=N_EXPERTS)
    padded = ((counts + MOE_BLOCK - 1) // MOE_BLOCK) * MOE_BLOCK
    pad_end = jnp.cumsum(padded)
    pad_start = pad_end - padded
    start = jnp.cumsum(counts) - counts
    dest = pad_start[e_sorted] + (jnp.arange(A) - start[e_sorted])
    NB = -(-A // MOE_BLOCK) + N_EXPERTS
    P = NB * MOE_BLOCK
    tok_buf = jnp.full((P,), T, jnp.int32).at[dest].set(tok_flat[order])
    gate_buf = jnp.zeros((P,), jnp.float32).at[dest].set(g_flat[order])
    blk_exp = jnp.minimum(jnp.searchsorted(pad_end, jnp.arange(NB) * MOE_BLOCK, side='right'), N_EXPERTS - 1)
    x_pad = jnp.concatenate([xt, jnp.zeros((1, D), xt.dtype)], axis=0)

    def run_block(args):
        tok, e = args
        xb = x_pad[tok]
        return swiglu(xb, w_gate[e], w_up[e], w_down[e])

    y = lax.map(run_block, (tok_buf.reshape(NB, MOE_BLOCK), blk_exp)).reshape(P, D)
    y = y * gate_buf[:, None].astype(y.dtype)
    out = jnp.zeros((T + 1, D), y.dtype).at[tok_buf].add(y)[:T]
    return out.reshape(B, S, D)


def setup_inputs(seed: int = 0) -> dict:
    key = jax.random.key(seed)
    ks = jax.random.split(key, 16)
    f32 = jnp.float32

    def nrm(k, shape, fan_in, scale=1.0):
        return jax.random.normal(k, shape, f32) * (scale * fan_in ** -0.5)

    ret_in_cols = RET_HEADS * (2 * RET_QK_DIM + 2 * RET_V_DIM)
    ret_v_width = RET_HEADS * RET_V_DIM
    att_cols = 3 * N_GROUPS * HEADS_PER_GROUP * ATT_HEAD_DIM
    att_out_in = HEADS_PER_GROUP * ATT_HEAD_DIM
    return {
        'x': jax.random.normal(ks[0], (BATCH, SEQ, D_MODEL), f32),
        'ln_gain': 1.0 + 0.01 * jax.random.normal(ks[1], (DEPTH, 2, D_MODEL), f32),
        'ln_bias': 0.01 * jax.random.normal(ks[2], (DEPTH, 2, D_MODEL), f32),
        'ret_w_in': nrm(ks[3], (N_RET, D_MODEL, ret_in_cols), D_MODEL),
        'ret_gn_gain': 1.0 + 0.01 * jax.random.normal(ks[4], (N_RET, ret_v_width), f32),
        'ret_w_out': nrm(ks[5], (N_RET, ret_v_width, D_MODEL), ret_v_width, BETA),
        'att_w_qkv': nrm(ks[6], (N_ATT, D_MODEL, att_cols), D_MODEL),
        'att_w_out': nrm(ks[7], (N_ATT, att_out_in, D_MODEL), att_out_in, BETA),
        'ffn_w_gate': nrm(ks[8], (N_RET, D_MODEL, D_FF), D_MODEL),
        'ffn_w_up': nrm(ks[9], (N_RET, D_MODEL, D_FF), D_MODEL),
        'ffn_w_down': nrm(ks[10], (N_RET, D_FF, D_MODEL), D_FF, BETA),
        'moe_w_router': nrm(ks[11], (N_ATT, D_MODEL, N_EXPERTS), D_MODEL),
        'moe_w_gate': nrm(ks[12], (N_ATT, N_EXPERTS, D_MODEL, D_FF), D_MODEL),
        'moe_w_up': nrm(ks[13], (N_ATT, N_EXPERTS, D_MODEL, D_FF), D_MODEL),
        'moe_w_down': nrm(ks[14], (N_ATT, N_EXPERTS, D_FF, D_MODEL), D_FF, BETA),
    }


def reference(x, ln_gain, ln_bias, ret_w_in, ret_gn_gain, ret_w_out, att_w_qkv, att_w_out,
              ffn_w_gate, ffn_w_up, ffn_w_down, moe_w_router, moe_w_gate, moe_w_up, moe_w_down):
    for i in range(DEPTH):
        j = i // 2
        if i % N_MIXERS == 0:
            mix = retention(x, ret_w_in[j], ret_gn_gain[j], ret_w_out[j])
        else:
            mix = dilated_attention(x, att_w_qkv[j], att_w_out[j])
        x = layer_norm(ALPHA * x + mix, ln_gain[i, 0], ln_bias[i, 0])
        if i % 2 == 0:
            f = swiglu(x, ffn_w_gate[j], ffn_w_up[j], ffn_w_down[j])
        else:
            f = moe_swiglu(x, moe_w_router[j], moe_w_gate[j], moe_w_up[j], moe_w_down[j])
        x = layer_norm(ALPHA * x + f, ln_gain[i, 1], ln_bias[i, 1])
    return x
```

```python
import functools

import jax
import jax.numpy as jnp
from jax import lax
from jax.experimental import pallas as pl
from jax.experimental.pallas import tpu as pltpu

F32 = jnp.float32
BF16 = jnp.bfloat16

RET_HEADS = 4
RET_CHUNK = 128
ROPE_BASE = 10000.0
DIL_PATTERN = ((128, 1), (512, 4), (2048, 16))
N_GROUPS = len(DIL_PATTERN)
HEADS_PER_GROUP = 4
ATT_HEAD_DIM = 128
N_EXPERTS = 8
LN_EPS = 1e-5

LANES = 128
VMEM_LIMIT = 56 * 1024 * 1024

MM_TM = 1024
MM_TN = 512
LN_TM = 512
FFN_TM = 512
FFN_TF = 512
RET_ROWS = 512
ATT_Q = 128
ATT_OUT_W = HEADS_PER_GROUP * ATT_HEAD_DIM + LANES
MOE_BLOCK = 512
ROUTE_TB = 512
SCAT_TB = 1024
COMB_TB = 256
NEG_BIG = -1e30


def _params(*sem):
    return pltpu.CompilerParams(dimension_semantics=sem, vmem_limit_bytes=VMEM_LIMIT)


def _layer_norm(y, g, b):
    mu = jnp.mean(y, axis=-1, keepdims=True)
    d = y - mu
    var = jnp.mean(d * d, axis=-1, keepdims=True)
    return d * lax.rsqrt(var + LN_EPS) * g + b


def _mm_kernel(x_ref, w_ref, o_ref, xb_ref):
    @pl.when(pl.program_id(1) == 0)
    def _():
        xb_ref[...] = x_ref[...].astype(BF16)

    o_ref[...] = jnp.dot(xb_ref[...], w_ref[...], preferred_element_type=F32).astype(o_ref.dtype)


def _matmul(x, w, name):
    M, K = x.shape
    N = w.shape[1]
    tm, tn = min(MM_TM, M), MM_TN
    assert M % tm == 0 and N % tn == 0
    return pl.pallas_call(
        _mm_kernel,
        grid=(M // tm, N // tn),
        in_specs=[pl.BlockSpec((tm, K), lambda i, j: (i, 0)),
                  pl.BlockSpec((K, tn), lambda i, j: (0, j))],
        out_specs=pl.BlockSpec((tm, tn), lambda i, j: (i, j)),
        out_shape=jax.ShapeDtypeStruct((M, N), BF16),
        scratch_shapes=[pltpu.VMEM((tm, K), BF16)],
        compiler_params=_params("parallel", "arbitrary"),
        name=name,
    )(x, w)


def _ret_kernel(q_ref, k_ref, v_ref, g_ref, cos_ref, sin_ref, dec_ref, qdec_ref, kdec_ref,
                cdec_ref, gain_ref, o_ref, state_ref, *, n_chunks, dk):
    C = RET_CHUNK
    half = dk // 2

    @pl.when(pl.program_id(2) == 0)
    def _():
        state_ref[...] = jnp.zeros_like(state_ref)

    def rot(t, cos, sin):
        t1, t2 = t[:, :half], t[:, half:]
        return jnp.concatenate([t1 * cos - t2 * sin, t2 * cos + t1 * sin], axis=-1)

    def chunk(c, carry):
        rows = pl.ds(pl.multiple_of(c * C, C), C)
        cos, sin = cos_ref[rows, :], sin_ref[rows, :]
        q = rot(q_ref[rows, :].astype(F32), cos, sin)
        k = rot(k_ref[rows, :].astype(F32), cos, sin) * (dk ** -0.5)
        v = v_ref[rows, :]
        scores = lax.dot_general(q.astype(BF16), k.astype(BF16), (((1,), (1,)), ((), ())),
                                 preferred_element_type=F32) * dec_ref[0]
        intra = jnp.dot(scores.astype(BF16), v, preferred_element_type=F32)
        state = state_ref[...]
        cross = jnp.dot((q * qdec_ref[0]).astype(BF16), state.astype(BF16),
                        preferred_element_type=F32)
        kn_t = (k * kdec_ref[0]).T.astype(BF16)
        state_ref[...] = cdec_ref[0] * state + jnp.dot(kn_t, v, preferred_element_type=F32)
        r = intra + cross
        mu = jnp.mean(r, axis=-1, keepdims=True)
        d = r - mu
        var = jnp.mean(d * d, axis=-1, keepdims=True)
        normed = d * lax.rsqrt(var + LN_EPS) * gain_ref[...]
        gate = g_ref[rows, :].astype(F32)
        o_ref[rows, :] = (gate * jax.nn.sigmoid(gate) * normed).astype(o_ref.dtype)
        return carry

    lax.fori_loop(0, n_chunks, chunk, 0)


def _retention_core(proj, B, S, gn_gain):
    H, C = RET_HEADS, RET_CHUNK
    cols = proj.shape[1]
    dk = cols // (6 * H)
    dv = 2 * dk
    rb = min(RET_ROWS, S)
    assert S % rb == 0 and rb % C == 0
    nr = S // rb

    half = dk // 2
    inv = ROPE_BASE ** (-jnp.arange(half, dtype=F32) / half)
    ang = jnp.arange(S).astype(F32)[:, None] * inv[None, :]
    cos, sin = jnp.cos(ang), jnp.sin(ang)
    log_gamma = jnp.log(1.0 - 2.0 ** (-5.0 - jnp.arange(H, dtype=F32)))
    idx = jnp.arange(C, dtype=F32)
    rel = idx[:, None] - idx[None, :]
    decay_intra = jnp.where(rel >= 0, jnp.exp(log_gamma[:, None, None] * jnp.maximum(rel, 0.0)), 0.0)
    q_dec = jnp.exp(log_gamma[None, :] * (idx[:, None] + 1.0))
    k_dec = jnp.exp(log_gamma[None, :] * (C - 1.0 - idx[:, None]))
    chunk_dec = jnp.exp(log_gamma * C)
    qdec_b = jnp.broadcast_to(q_dec.T[:, :, None], (H, C, dk))
    kdec_b = jnp.broadcast_to(k_dec.T[:, :, None], (H, C, dk))
    cdec_b = jnp.broadcast_to(chunk_dec[:, None, None], (H, 1, dv))
    gain = gn_gain.reshape(1, H * dv)

    row = lambda b, h, i: b * nr + i
    kern = functools.partial(_ret_kernel, n_chunks=rb // C, dk=dk)
    return pl.pallas_call(
        kern,
        grid=(B, H, nr),
        in_specs=[
            pl.BlockSpec((rb, dk), lambda b, h, i: (row(b, h, i), h)),
            pl.BlockSpec((rb, dk), lambda b, h, i: (row(b, h, i), H + h)),
            pl.BlockSpec((rb, dv), lambda b, h, i: (row(b, h, i), H + h)),
            pl.BlockSpec((rb, dv), lambda b, h, i: (row(b, h, i), 2 * H + h)),
            pl.BlockSpec((rb, half), lambda b, h, i: (i, 0)),
            pl.BlockSpec((rb, half), lambda b, h, i: (i, 0)),
            pl.BlockSpec((1, C, C), lambda b, h, i: (h, 0, 0)),
            pl.BlockSpec((1, C, dk), lambda b, h, i: (h, 0, 0)),
            pl.BlockSpec((1, C, dk), lambda b, h, i: (h, 0, 0)),
            pl.BlockSpec((1, 1, dv), lambda b, h, i: (h, 0, 0)),
            pl.BlockSpec((1, dv), lambda b, h, i: (0, h)),
        ],
        out_specs=pl.BlockSpec((rb, dv), lambda b, h, i: (row(b, h, i), h)),
        out_shape=jax.ShapeDtypeStruct((B * S, H * dv), BF16),
        scratch_shapes=[pltpu.VMEM((dk, dv), F32)],
        compiler_params=_params("parallel", "parallel", "arbitrary"),
        name="retention_core",
    )(proj, proj, proj, proj, cos, sin, decay_intra, qdec_b, kdec_b, cdec_b, gain)


def _mm_res_ln_kernel(a_ref, w_ref, x_ref, g_ref, b_ref, o_ref, *, alpha):
    y = jnp.dot(a_ref[...], w_ref[...], preferred_element_type=F32)
    o_ref[...] = _layer_norm(alpha * x_ref[...] + y, g_ref[...], b_ref[...])


def _mm_res_ln(a, w, x, g, b, alpha, name):
    M, K = a.shape
    D = w.shape[1]
    tm = min(LN_TM, M)
    assert M % tm == 0
    return pl.pallas_call(
        functools.partial(_mm_res_ln_kernel, alpha=alpha),
        grid=(M // tm,),
        in_specs=[pl.BlockSpec((tm, K), lambda i: (i, 0)),
                  pl.BlockSpec((K, D), lambda i: (0, 0)),
                  pl.BlockSpec((tm, D), lambda i: (i, 0)),
                  pl.BlockSpec((1, D), lambda i: (0, 0)),
                  pl.BlockSpec((1, D), lambda i: (0, 0))],
        out_specs=pl.BlockSpec((tm, D), lambda i: (i, 0)),
        out_shape=jax.ShapeDtypeStruct((M, D), F32),
        compiler_params=_params("parallel"),
        name=name,
    )(a, w, x, g.reshape(1, D), b.reshape(1, D))


def _ffn_kernel(x_ref, wg_ref, wu_ref, wd_ref, g_ref, b_ref, o_ref, xb_ref, acc_ref, *, alpha):
    f = pl.program_id(1)

    @pl.when(f == 0)
    def _():
        xb_ref[...] = x_ref[...].astype(BF16)
        acc_ref[...] = jnp.zeros_like(acc_ref)

    xb = xb_ref[...]
    gate = jnp.dot(xb, wg_ref[...], preferred_element_type=F32)
    up = jnp.dot(xb, wu_ref[...], preferred_element_type=F32)
    hid = (gate * jax.nn.sigmoid(gate) * up).astype(BF16)
    acc_ref[...] += jnp.dot(hid, wd_ref[...], preferred_element_type=F32)

    @pl.when(f == pl.num_programs(1) - 1)
    def _():
        o_ref[...] = _layer_norm(alpha * x_ref[...] + acc_ref[...], g_ref[...], b_ref[...])


def _ffn_res_ln(x, wg, wu, wd, g, b, alpha):
    M, D = x.shape
    F = wg.shape[1]
    tm, tf = min(FFN_TM, M), FFN_TF
    assert M % tm == 0 and F % tf == 0
    return pl.pallas_call(
        functools.partial(_ffn_kernel, alpha=alpha),
        grid=(M // tm, F // tf),
        in_specs=[pl.BlockSpec((tm, D), lambda i, f: (i, 0)),
                  pl.BlockSpec((D, tf), lambda i, f: (0, f)),
                  pl.BlockSpec((D, tf), lambda i, f: (0, f)),
                  pl.BlockSpec((tf, D), lambda i, f: (f, 0)),
                  pl.BlockSpec((1, D), lambda i, f: (0, 0)),
                  pl.BlockSpec((1, D), lambda i, f: (0, 0))],
        out_specs=pl.BlockSpec((tm, D), lambda i, f: (i, 0)),
        out_shape=jax.ShapeDtypeStruct((M, D), F32),
        scratch_shapes=[pltpu.VMEM((tm, D), BF16), pltpu.VMEM((tm, D), F32)],
        compiler_params=_params("parallel", "arbitrary"),
        name="ffn_res_ln",
    )(x, wg, wu, wd, g.reshape(1, D), b.reshape(1, D))


def _attn_kernel(q_ref, kp_ref, kc_ref, vp_ref, vc_ref, o_ref):
    Q, dh = ATT_Q, ATT_HEAD_DIM
    qi = pl.program_id(2)
    row = lax.broadcasted_iota(jnp.int32, (Q, 2 * Q), 0)
    col = lax.broadcasted_iota(jnp.int32, (Q, 2 * Q), 1)
    first = jnp.where(qi > 0, 0, Q)
    valid = (col >= row) & (col <= row + Q) & (col >= first)
    lane = lax.broadcasted_iota(jnp.int32, (Q, LANES), 1)
    lse_slab = jnp.zeros((Q, LANES), F32)
    for h in range(HEADS_PER_GROUP):
        cs = slice(h * dh, (h + 1) * dh)
        k = jnp.concatenate([kp_ref[:, cs], kc_ref[:, cs]], axis=0)
        v = jnp.concatenate([vp_ref[:, cs], vc_ref[:, cs]], axis=0)
        s = lax.dot_general(q_ref[:, cs], k, (((1,), (1,)), ((), ())),
                            preferred_element_type=F32) * (dh ** -0.5)
        s = jnp.where(valid, s, NEG_BIG)
        m = jnp.max(s, axis=-1, keepdims=True)
        p = jnp.exp(s - m)
        l = jnp.sum(p, axis=-1, keepdims=True)
        o = jnp.dot(p.astype(BF16), v, preferred_element_type=F32)
        o_ref[:, cs] = o / l
        lse_slab = jnp.where(lane == h, m + jnp.log(l), lse_slab)
    o_ref[:, HEADS_PER_GROUP * dh:] = lse_slab


def _dilated_group(qkv, B, S, gi, dil):
    Q = ATT_Q
    gw = HEADS_PER_GROUP * ATT_HEAD_DIM
    nblk = qkv.shape[1] // gw
    sd = S // dil
    assert S % dil == 0 and sd % Q == 0
    view = qkv.reshape(B, sd, dil * qkv.shape[1])
    prev = lambda qi: jnp.maximum(qi - 1, 0)
    spec = lambda comp, blk: pl.BlockSpec(
        (None, Q, gw), lambda b, r, qi: (b, blk(qi), r * nblk + comp * N_GROUPS + gi))
    cur = lambda qi: qi
    out = pl.pallas_call(
        _attn_kernel,
        grid=(B, dil, sd // Q),
        in_specs=[spec(0, cur), spec(1, prev), spec(1, cur), spec(2, prev), spec(2, cur)],
        out_specs=pl.BlockSpec((None, Q, ATT_OUT_W), lambda b, r, qi: (b, qi, r)),
        out_shape=jax.ShapeDtypeStruct((B, sd, dil * ATT_OUT_W), F32),
        compiler_params=_params("parallel", "parallel", "arbitrary"),
        name=f"dilated_attn_g{gi}",
    )(view, view, view, view, view)
    return out.reshape(B * S, ATT_OUT_W)


def _attn_merge_kernel(o0_ref, o1_ref, o2_ref, w_ref, x_ref, g_ref, b_ref, o_ref, *, alpha):
    dh = ATT_HEAD_DIM
    gw = HEADS_PER_GROUP * dh
    refs = (o0_ref, o1_ref, o2_ref)
    lses = [r[:, gw:] for r in refs]
    mx = jnp.maximum(jnp.maximum(lses[0], lses[1]), lses[2])
    ws = [jnp.exp(l - mx) for l in lses]
    den = ws[0] + ws[1] + ws[2]
    ws = [w / den for w in ws]
    heads = []
    for h in range(HEADS_PER_GROUP):
        cs = slice(h * dh, (h + 1) * dh)
        acc = ws[0][:, h:h + 1] * refs[0][:, cs]
        acc += ws[1][:, h:h + 1] * refs[1][:, cs]
        acc += ws[2][:, h:h + 1] * refs[2][:, cs]
        heads.append(acc)
    merged = jnp.concatenate(heads, axis=-1).astype(BF16)
    y = jnp.dot(merged, w_ref[...], preferred_element_type=F32)
    o_ref[...] = _layer_norm(alpha * x_ref[...] + y, g_ref[...], b_ref[...])


def _attn_merge_res_ln(outs, w, x, g, b, alpha):
    M, D = x.shape
    K = w.shape[0]
    tm = min(LN_TM, M)
    assert M % tm == 0
    ospec = pl.BlockSpec((tm, ATT_OUT_W), lambda i: (i, 0))
    return pl.pallas_call(
        functools.partial(_attn_merge_kernel, alpha=alpha),
        grid=(M // tm,),
        in_specs=[ospec, ospec, ospec,
                  pl.BlockSpec((K, D), lambda i: (0, 0)),
                  pl.BlockSpec((tm, D), lambda i: (i, 0)),
                  pl.BlockSpec((1, D), lambda i: (0, 0)),
                  pl.BlockSpec((1, D), lambda i: (0, 0))],
        out_specs=pl.BlockSpec((tm, D), lambda i: (i, 0)),
        out_shape=jax.ShapeDtypeStruct((M, D), F32),
        compiler_params=_params("parallel"),
        name="attn_merge_res_ln",
    )(*outs, w, x, g.reshape(1, D), b.reshape(1, D))


def _router_kernel(x_ref, w_ref, slab_ref, cnt_ref, run_ref):
    tb = x_ref.shape[0]

    @pl.when(pl.program_id(0) == 0)
    def _():
        run_ref[...] = jnp.zeros_like(run_ref)

    logits = jnp.dot(x_ref[...], w_ref[...], preferred_element_type=F32,
                     precision=lax.Precision.HIGHEST)
    lane = lax.broadcasted_iota(jnp.int32, (tb, LANES), 1)
    logits = jnp.where(lane < N_EXPERTS, logits, NEG_BIG)
    m1 = jnp.max(logits, axis=-1, keepdims=True)
    i1 = jnp.min(jnp.where(logits == m1, lane, LANES), axis=-1, keepdims=True)
    rest = jnp.where(lane == i1, NEG_BIG, logits)
    m2 = jnp.max(rest, axis=-1, keepdims=True)
    i2 = jnp.min(jnp.where(rest == m2, lane, LANES), axis=-1, keepdims=True)
    e = jnp.exp(m2 - m1)
    g1 = 1.0 / (1.0 + e)
    g2 = e / (1.0 + e)
    onehot = jnp.where((lane == i1) | (lane == i2), 1.0, 0.0)
    r = lax.broadcasted_iota(jnp.int32, (tb, tb), 0)
    c = lax.broadcasted_iota(jnp.int32, (tb, tb), 1)
    lower = jnp.where(c < r, 1.0, 0.0).astype(BF16)
    before = jnp.dot(lower, onehot.astype(BF16), preferred_element_type=F32) + run_ref[...]
    rank1 = jnp.sum(jnp.where(lane == i1, before, 0.0), axis=-1, keepdims=True)
    rank2 = jnp.sum(jnp.where(lane == i2, before, 0.0), axis=-1, keepdims=True)
    total = run_ref[...] + jnp.sum(onehot, axis=0, keepdims=True)
    run_ref[...] = total
    cnt_ref[...] = total
    slab = jnp.where(lane == 0, i1.astype(F32), 0.0)
    slab = jnp.where(lane == 1, i2.astype(F32), slab)
    slab = jnp.where(lane == 2, g1, slab)
    slab = jnp.where(lane == 3, g2, slab)
    slab = jnp.where(lane == 4, rank1, slab)
    slab = jnp.where(lane == 5, rank2, slab)
    slab_ref[...] = slab


def _router(x, w_router):
    T, D = x.shape
    tb = min(ROUTE_TB, T)
    assert T % tb == 0
    w = jnp.zeros((D, LANES), F32).at[:, :N_EXPERTS].set(w_router)
    return pl.pallas_call(
        _router_kernel,
        grid=(T // tb,),
        in_specs=[pl.BlockSpec((tb, D), lambda i: (i, 0)),
                  pl.BlockSpec((D, LANES), lambda i: (0, 0))],
        out_specs=[pl.BlockSpec((tb, LANES), lambda i: (i, 0)),
                   pl.BlockSpec((1, LANES), lambda i: (0, 0))],
        out_shape=[jax.ShapeDtypeStruct((T, LANES), F32),
                   jax.ShapeDtypeStruct((1, LANES), F32)],
        scratch_shapes=[pltpu.VMEM((1, LANES), F32)],
        compiler_params=_params("arbitrary"),
        name="moe_router",
    )(x, w)


def _scatter_kernel(d1_ref, d2_ref, x_hbm, init_hbm, xs_hbm, sem):
    del init_hbm
    tb = d1_ref.shape[-1]
    base = pl.program_id(0) * tb

    def row_copy(t, dst):
        return pltpu.make_async_copy(x_hbm.at[pl.ds(base + t, 1)], xs_hbm.at[pl.ds(dst, 1)], sem)

    def issue(t, carry):
        row_copy(t, d1_ref[0, 0, t]).start()
        row_copy(t, d2_ref[0, 0, t]).start()
        return carry

    def drain(t, carry):
        row_copy(t, d1_ref[0, 0, t]).wait()
        row_copy(t, d2_ref[0, 0, t]).wait()
        return carry

    lax.fori_loop(0, tb, issue, 0)
    lax.fori_loop(0, tb, drain, 0)


def _scatter_rows(x, d1, d2, P):
    T, D = x.shape
    tb = min(SCAT_TB, T)
    assert T % tb == 0
    nb = T // tb
    smem_spec = pl.BlockSpec((1, 1, tb), lambda i: (i, 0, 0), memory_space=pltpu.SMEM)
    return pl.pallas_call(
        _scatter_kernel,
        grid=(nb,),
        in_specs=[smem_spec, smem_spec,
                  pl.BlockSpec(memory_space=pl.ANY),
                  pl.BlockSpec(memory_space=pl.ANY)],
        out_specs=pl.BlockSpec(memory_space=pl.ANY),
        out_shape=jax.ShapeDtypeStruct((P, D), F32),
        scratch_shapes=[pltpu.SemaphoreType.DMA(())],
        input_output_aliases={3: 0},
        compiler_params=_params("arbitrary"),
        name="moe_scatter_rows",
    )(d1.reshape(nb, 1, tb), d2.reshape(nb, 1, tb), x, jnp.zeros((P, D), F32))


def _expert_kernel(be_ref, x_ref, wg_ref, wu_ref, wd_ref, o_ref, xb_ref, acc_ref):
    del be_ref
    f = pl.program_id(1)

    @pl.when(f == 0)
    def _():
        xb_ref[...] = x_ref[...].astype(BF16)
        acc_ref[...] = jnp.zeros_like(acc_ref)

    xb = xb_ref[...]
    gate = jnp.dot(xb, wg_ref[...], preferred_element_type=F32)
    up = jnp.dot(xb, wu_ref[...], preferred_element_type=F32)
    hid = (gate * jax.nn.sigmoid(gate) * up).astype(BF16)
    acc_ref[...] += jnp.dot(hid, wd_ref[...], preferred_element_type=F32)

    @pl.when(f == pl.num_programs(1) - 1)
    def _():
        o_ref[...] = acc_ref[...]


def _experts(xs, blk_exp, wg, wu, wd):
    P, D = xs.shape
    F = wg.shape[2]
    tm, tf = MOE_BLOCK, FFN_TF
    assert P % tm == 0 and F % tf == 0
    grid_spec = pltpu.PrefetchScalarGridSpec(
        num_scalar_prefetch=1,
        grid=(P // tm, F // tf),
        in_specs=[pl.BlockSpec((tm, D), lambda i, f, be: (i, 0)),
                  pl.BlockSpec((None, D, tf), lambda i, f, be: (be[i], 0, f)),
                  pl.BlockSpec((None, D, tf), lambda i, f, be: (be[i], 0, f)),
                  pl.BlockSpec((None, tf, D), lambda i, f, be: (be[i], f, 0))],
        out_specs=pl.BlockSpec((tm, D), lambda i, f, be: (i, 0)),
        scratch_shapes=[pltpu.VMEM((tm, D), BF16), pltpu.VMEM((tm, D), F32)],
    )
    return pl.pallas_call(
        _expert_kernel,
        grid_spec=grid_spec,
        out_shape=jax.ShapeDtypeStruct((P, D), F32),
        compiler_params=_params("parallel", "arbitrary"),
        name="moe_experts",
    )(blk_exp, xs, wg, wu, wd)


def _combine_kernel(d1_ref, d2_ref, slab_ref, x_ref, g_ref, b_ref, y_hbm, o_ref, buf_ref, sem, *, alpha):
    tb = x_ref.shape[0]

    def row_copy(t, src, slot):
        return pltpu.make_async_copy(y_hbm.at[pl.ds(src, 1)], buf_ref.at[slot, pl.ds(t, 1)], sem)

    def issue(t, carry):
        row_copy(t, d1_ref[0, 0, t], 0).start()
        row_copy(t, d2_ref[0, 0, t], 1).start()
        return carry

    def drain(t, carry):
        row_copy(t, d1_ref[0, 0, t], 0).wait()
        row_copy(t, d2_ref[0, 0, t], 1).wait()
        return carry

    lax.fori_loop(0, tb, issue, 0)
    lax.fori_loop(0, tb, drain, 0)
    slab = slab_ref[...]
    mix = slab[:, 2:3] * buf_ref[0] + slab[:, 3:4] * buf_ref[1]
    o_ref[...] = _layer_norm(alpha * x_ref[...] + mix, g_ref[...], b_ref[...])


def _combine_res_ln(y, d1, d2, slab, x, g, b, alpha):
    T, D = x.shape
    tb = min(COMB_TB, T)
    assert T % tb == 0
    nb = T // tb
    smem_spec = pl.BlockSpec((1, 1, tb), lambda i: (i, 0, 0), memory_space=pltpu.SMEM)
    return pl.pallas_call(
        functools.partial(_combine_kernel, alpha=alpha),
        grid=(nb,),
        in_specs=[smem_spec, smem_spec,
                  pl.BlockSpec((tb, LANES), lambda i: (i, 0)),
                  pl.BlockSpec((tb, D), lambda i: (i, 0)),
                  pl.BlockSpec((1, D), lambda i: (0, 0)),
                  pl.BlockSpec((1, D), lambda i: (0, 0)),
                  pl.BlockSpec(memory_space=pl.ANY)],
        out_specs=pl.BlockSpec((tb, D), lambda i: (i, 0)),
        out_shape=jax.ShapeDtypeStruct((T, D), F32),
        scratch_shapes=[pltpu.VMEM((2, tb, D), F32), pltpu.SemaphoreType.DMA(())],
        compiler_params=_params("arbitrary"),
        name="moe_combine_res_ln",
    )(d1.reshape(nb, 1, tb), d2.reshape(nb, 1, tb), slab, x, g.reshape(1, D), b.reshape(1, D), y)


def _moe_res_ln(x, w_router, wg, wu, wd, g, b, alpha):
    T, D = x.shape
    slab, counts = _router(x, w_router)
    counts = counts[0, :N_EXPERTS].astype(jnp.int32)
    padded = ((counts + MOE_BLOCK - 1) // MOE_BLOCK) * MOE_BLOCK
    pad_end = jnp.cumsum(padded)
    pad_start = pad_end - padded
    e1 = slab[:, 0].astype(jnp.int32)
    e2 = slab[:, 1].astype(jnp.int32)
    d1 = pad_start[e1] + slab[:, 4].astype(jnp.int32)
    d2 = pad_start[e2] + slab[:, 5].astype(jnp.int32)
    nblk = -(-(2 * T) // MOE_BLOCK) + N_EXPERTS
    blk_exp = jnp.minimum(
        jnp.searchsorted(pad_end, jnp.arange(nblk, dtype=jnp.int32) * MOE_BLOCK, side="right"),
        N_EXPERTS - 1).astype(jnp.int32)
    xs = _scatter_rows(x, d1, d2, nblk * MOE_BLOCK)
    y = _experts(xs, blk_exp, wg, wu, wd)
    return _combine_res_ln(y, d1, d2, slab, x, g, b, alpha)


def kernel(x, ln_gain, ln_bias, ret_w_in, ret_gn_gain, ret_w_out, att_w_qkv, att_w_out,
           ffn_w_gate, ffn_w_up, ffn_w_down, moe_w_router, moe_w_gate, moe_w_up, moe_w_down):
    B, S, D = x.shape
    depth = ln_gain.shape[0]
    alpha = (2 * depth) ** 0.25
    h = x.reshape(B * S, D)
    for i in range(depth):
        j = i // 2
        if i % 2 == 0:
            proj = _matmul(h, ret_w_in[j].astype(BF16), "ret_in_proj")
            gated = _retention_core(proj, B, S, ret_gn_gain[j])
            h = _mm_res_ln(gated, ret_w_out[j].astype(BF16), h, ln_gain[i, 0], ln_bias[i, 0], alpha,
                           "ret_out_res_ln")
            h = _ffn_res_ln(h, ffn_w_gate[j].astype(BF16), ffn_w_up[j].astype(BF16),
                            ffn_w_down[j].astype(BF16), ln_gain[i, 1], ln_bias[i, 1], alpha)
        else:
            qkv = _matmul(h, att_w_qkv[j].astype(BF16), "att_qkv_proj")
            outs = [_dilated_group(qkv, B, S, gi, dil) for gi, (_, dil) in enumerate(DIL_PATTERN)]
            h = _attn_merge_res_ln(outs, att_w_out[j].astype(BF16), h, ln_gain[i, 0], ln_bias[i, 0], alpha)
            h = _moe_res_ln(h, moe_w_router[j], moe_w_gate[j].astype(BF16), moe_w_up[j].astype(BF16),
                            moe_w_down[j].astype(BF16), ln_gain[i, 1], ln_bias[i, 1], alpha)
    return h.reshape(B, S, D)
```

```python
import functools

import jax
import jax.numpy as jnp
from jax import lax
from jax.experimental import pallas as pl
from jax.experimental.pallas import tpu as pltpu

F32 = jnp.float32
BF16 = jnp.bfloat16

RET_HEADS = 4
RET_CHUNK = 128
ROPE_BASE = 10000.0
DIL_PATTERN = ((128, 1), (512, 4), (2048, 16))
N_GROUPS = len(DIL_PATTERN)
HEADS_PER_GROUP = 4
ATT_HEAD_DIM = 128
N_EXPERTS = 8
LN_EPS = 1e-5

LANES = 128
VMEM_LIMIT = 56 * 1024 * 1024

MM_TM = 1024
MM_TN = 512
LN_TM = 512
FFN_TM = 512
FFN_TF = 512
RET_ROWS = 512
ATT_Q = 128
ATT_OUT_W = HEADS_PER_GROUP * ATT_HEAD_DIM + LANES
MOE_BLOCK = 448
ROUTE_TB = 512
COMB_TB = 256
DMA_UNROLL = 8
NEG_BIG = -1e30


def _params(*sem):
    return pltpu.CompilerParams(dimension_semantics=sem, vmem_limit_bytes=VMEM_LIMIT)


def _layer_norm(y, g, b):
    mu = jnp.mean(y, axis=-1, keepdims=True)
    d = y - mu
    var = jnp.mean(d * d, axis=-1, keepdims=True)
    return d * lax.rsqrt(var + LN_EPS) * g + b


def _mm_kernel(x_ref, w_ref, o_ref, xb_ref):
    @pl.when(pl.program_id(1) == 0)
    def _():
        xb_ref[...] = x_ref[...].astype(BF16)

    o_ref[...] = jnp.dot(xb_ref[...], w_ref[...], preferred_element_type=F32).astype(o_ref.dtype)


def _matmul(x, w, name):
    M, K = x.shape
    N = w.shape[1]
    tm, tn = min(MM_TM, M), MM_TN
    assert M % tm == 0 and N % tn == 0
    return pl.pallas_call(
        _mm_kernel,
        grid=(M // tm, N // tn),
        in_specs=[pl.BlockSpec((tm, K), lambda i, j: (i, 0)),
                  pl.BlockSpec((K, tn), lambda i, j: (0, j))],
        out_specs=pl.BlockSpec((tm, tn), lambda i, j: (i, j)),
        out_shape=jax.ShapeDtypeStruct((M, N), BF16),
        scratch_shapes=[pltpu.VMEM((tm, K), BF16)],
        compiler_params=_params("parallel", "arbitrary"),
        name=name,
    )(x, w)


def _ret_kernel(q_ref, k_ref, v_ref, g_ref, cos_ref, sin_ref, dec_ref, qdec_ref, kdec_ref,
                cdec_ref, gain_ref, o_ref, state_ref, *, n_chunks, dk):
    C = RET_CHUNK
    half = dk // 2

    @pl.when(pl.program_id(2) == 0)
    def _():
        state_ref[...] = jnp.zeros_like(state_ref)

    def rot(t, cos, sin):
        t1, t2 = t[:, :half], t[:, half:]
        return jnp.concatenate([t1 * cos - t2 * sin, t2 * cos + t1 * sin], axis=-1)

    def chunk(c, carry):
        rows = pl.ds(pl.multiple_of(c * C, C), C)
        cos, sin = cos_ref[rows, :], sin_ref[rows, :]
        q = rot(q_ref[rows, :].astype(F32), cos, sin)
        k = rot(k_ref[rows, :].astype(F32), cos, sin) * (dk ** -0.5)
        v = v_ref[rows, :]
        scores = lax.dot_general(q.astype(BF16), k.astype(BF16), (((1,), (1,)), ((), ())),
                                 preferred_element_type=F32) * dec_ref[0]
        intra = jnp.dot(scores.astype(BF16), v, preferred_element_type=F32)
        state = state_ref[...]
        cross = jnp.dot((q * qdec_ref[0]).astype(BF16), state.astype(BF16),
                        preferred_element_type=F32)
        kn_t = (k * kdec_ref[0]).T.astype(BF16)
        state_ref[...] = cdec_ref[0] * state + jnp.dot(kn_t, v, preferred_element_type=F32)
        r = intra + cross
        mu = jnp.mean(r, axis=-1, keepdims=True)
        d = r - mu
        var = jnp.mean(d * d, axis=-1, keepdims=True)
        normed = d * lax.rsqrt(var + LN_EPS) * gain_ref[...]
        gate = g_ref[rows, :].astype(F32)
        o_ref[rows, :] = (gate * jax.nn.sigmoid(gate) * normed).astype(o_ref.dtype)
        return carry

    lax.fori_loop(0, n_chunks, chunk, 0)


def _retention_core(proj, B, S, gn_gain):
    H, C = RET_HEADS, RET_CHUNK
    cols = proj.shape[1]
    dk = cols // (6 * H)
    dv = 2 * dk
    rb = min(RET_ROWS, S)
    assert S % rb == 0 and rb % C == 0
    nr = S // rb

    half = dk // 2
    inv = ROPE_BASE ** (-jnp.arange(half, dtype=F32) / half)
    ang = jnp.arange(S).astype(F32)[:, None] * inv[None, :]
    cos, sin = jnp.cos(ang), jnp.sin(ang)
    log_gamma = jnp.log(1.0 - 2.0 ** (-5.0 - jnp.arange(H, dtype=F32)))
    idx = jnp.arange(C, dtype=F32)
    rel = idx[:, None] - idx[None, :]
    decay_intra = jnp.where(rel >= 0, jnp.exp(log_gamma[:, None, None] * jnp.maximum(rel, 0.0)), 0.0)
    q_dec = jnp.exp(log_gamma[None, :] * (idx[:, None] + 1.0))
    k_dec = jnp.exp(log_gamma[None, :] * (C - 1.0 - idx[:, None]))
    chunk_dec = jnp.exp(log_gamma * C)
    qdec_b = jnp.broadcast_to(q_dec.T[:, :, None], (H, C, dk))
    kdec_b = jnp.broadcast_to(k_dec.T[:, :, None], (H, C, dk))
    cdec_b = jnp.broadcast_to(chunk_dec[:, None, None], (H, 1, dv))
    gain = gn_gain.reshape(1, H * dv)

    row = lambda b, h, i: b * nr + i
    kern = functools.partial(_ret_kernel, n_chunks=rb // C, dk=dk)
    return pl.pallas_call(
        kern,
        grid=(B, H, nr),
        in_specs=[
            pl.BlockSpec((rb, dk), lambda b, h, i: (row(b, h, i), h)),
            pl.BlockSpec((rb, dk), lambda b, h, i: (row(b, h, i), H + h)),
            pl.BlockSpec((rb, dv), lambda b, h, i: (row(b, h, i), H + h)),
            pl.BlockSpec((rb, dv), lambda b, h, i: (row(b, h, i), 2 * H + h)),
            pl.BlockSpec((rb, half), lambda b, h, i: (i, 0)),
            pl.BlockSpec((rb, half), lambda b, h, i: (i, 0)),
            pl.BlockSpec((1, C, C), lambda b, h, i: (h, 0, 0)),
            pl.BlockSpec((1, C, dk), lambda b, h, i: (h, 0, 0)),
            pl.BlockSpec((1, C, dk), lambda b, h, i: (h, 0, 0)),
            pl.BlockSpec((1, 1, dv), lambda b, h, i: (h, 0, 0)),
            pl.BlockSpec((1, dv), lambda b, h, i: (0, h)),
        ],
        out_specs=pl.BlockSpec((rb, dv), lambda b, h, i: (row(b, h, i), h)),
        out_shape=jax.ShapeDtypeStruct((B * S, H * dv), BF16),
        scratch_shapes=[pltpu.VMEM((dk, dv), F32)],
        compiler_params=_params("parallel", "parallel", "arbitrary"),
        name="retention_core",
    )(proj, proj, proj, proj, cos, sin, decay_intra, qdec_b, kdec_b, cdec_b, gain)


def _mm_res_ln_kernel(a_ref, w_ref, x_ref, g_ref, b_ref, o_ref, *, alpha):
    y = jnp.dot(a_ref[...], w_ref[...], preferred_element_type=F32)
    o_ref[...] = _layer_norm(alpha * x_ref[...] + y, g_ref[...], b_ref[...])


def _mm_res_ln(a, w, x, g, b, alpha, name):
    M, K = a.shape
    D = w.shape[1]
    tm = min(LN_TM, M)
    assert M % tm == 0
    return pl.pallas_call(
        functools.partial(_mm_res_ln_kernel, alpha=alpha),
        grid=(M // tm,),
        in_specs=[pl.BlockSpec((tm, K), lambda i: (i, 0)),
                  pl.BlockSpec((K, D), lambda i: (0, 0)),
                  pl.BlockSpec((tm, D), lambda i: (i, 0)),
                  pl.BlockSpec((1, D), lambda i: (0, 0)),
                  pl.BlockSpec((1, D), lambda i: (0, 0))],
        out_specs=pl.BlockSpec((tm, D), lambda i: (i, 0)),
        out_shape=jax.ShapeDtypeStruct((M, D), F32),
        compiler_params=_params("parallel"),
        name=name,
    )(a, w, x, g.reshape(1, D), b.reshape(1, D))


def _ffn_kernel(x_ref, wg_ref, wu_ref, wd_ref, g_ref, b_ref, o_ref, xb_ref, acc_ref, *, alpha):
    f = pl.program_id(1)

    @pl.when(f == 0)
    def _():
        xb_ref[...] = x_ref[...].astype(BF16)
        acc_ref[...] = jnp.zeros_like(acc_ref)

    xb = xb_ref[...]
    gate = jnp.dot(xb, wg_ref[...], preferred_element_type=F32)
    up = jnp.dot(xb, wu_ref[...], preferred_element_type=F32)
    hid = (gate * jax.nn.sigmoid(gate) * up).astype(BF16)
    acc_ref[...] += jnp.dot(hid, wd_ref[...], preferred_element_type=F32)

    @pl.when(f == pl.num_programs(1) - 1)
    def _():
        o_ref[...] = _layer_norm(alpha * x_ref[...] + acc_ref[...], g_ref[...], b_ref[...])


def _ffn_res_ln(x, wg, wu, wd, g, b, alpha):
    M, D = x.shape
    F = wg.shape[1]
    tm, tf = min(FFN_TM, M), FFN_TF
    assert M % tm == 0 and F % tf == 0
    return pl.pallas_call(
        functools.partial(_ffn_kernel, alpha=alpha),
        grid=(M // tm, F // tf),
        in_specs=[pl.BlockSpec((tm, D), lambda i, f: (i, 0)),
                  pl.BlockSpec((D, tf), lambda i, f: (0, f)),
                  pl.BlockSpec((D, tf), lambda i, f: (0, f)),
                  pl.BlockSpec((tf, D), lambda i, f: (f, 0)),
                  pl.BlockSpec((1, D), lambda i, f: (0, 0)),
                  pl.BlockSpec((1, D), lambda i, f: (0, 0))],
        out_specs=pl.BlockSpec((tm, D), lambda i, f: (i, 0)),
        out_shape=jax.ShapeDtypeStruct((M, D), F32),
        scratch_shapes=[pltpu.VMEM((tm, D), BF16), pltpu.VMEM((tm, D), F32)],
        compiler_params=_params("parallel", "arbitrary"),
        name="ffn_res_ln",
    )(x, wg, wu, wd, g.reshape(1, D), b.reshape(1, D))


def _attn_kernel(q_ref, kp_ref, kc_ref, vp_ref, vc_ref, o_ref):
    Q, dh = ATT_Q, ATT_HEAD_DIM
    qi = pl.program_id(2)
    row = lax.broadcasted_iota(jnp.int32, (Q, 2 * Q), 0)
    col = lax.broadcasted_iota(jnp.int32, (Q, 2 * Q), 1)
    first = jnp.where(qi > 0, 0, Q)
    valid = (col >= row) & (col <= row + Q) & (col >= first)
    lane = lax.broadcasted_iota(jnp.int32, (Q, LANES), 1)
    lse_slab = jnp.zeros((Q, LANES), F32)
    for h in range(HEADS_PER_GROUP):
        cs = slice(h * dh, (h + 1) * dh)
        k = jnp.concatenate([kp_ref[:, cs], kc_ref[:, cs]], axis=0)
        v = jnp.concatenate([vp_ref[:, cs], vc_ref[:, cs]], axis=0)
        s = lax.dot_general(q_ref[:, cs], k, (((1,), (1,)), ((), ())),
                            preferred_element_type=F32) * (dh ** -0.5)
        s = jnp.where(valid, s, NEG_BIG)
        m = jnp.max(s, axis=-1, keepdims=True)
        p = jnp.exp(s - m)
        l = jnp.sum(p, axis=-1, keepdims=True)
        o = jnp.dot(p.astype(BF16), v, preferred_element_type=F32)
        o_ref[:, cs] = o / l
        lse_slab = jnp.where(lane == h, m + jnp.log(l), lse_slab)
    o_ref[:, HEADS_PER_GROUP * dh:] = lse_slab


def _dilated_group(qkv, B, S, gi, win, dil):
    Q = ATT_Q
    gw = HEADS_PER_GROUP * ATT_HEAD_DIM
    nblk = qkv.shape[1] // gw
    sd = S // dil
    assert win // dil == Q and S % dil == 0 and sd % Q == 0
    view = qkv.reshape(B, sd, dil * qkv.shape[1])
    prev = lambda qi: jnp.maximum(qi - 1, 0)
    spec = lambda comp, blk: pl.BlockSpec(
        (None, Q, gw), lambda b, r, qi: (b, blk(qi), r * nblk + comp))
    cur = lambda qi: qi
    out = pl.pallas_call(
        _attn_kernel,
        grid=(B, dil, sd // Q),
        in_specs=[spec(0, cur), spec(1, prev), spec(1, cur), spec(2, prev), spec(2, cur)],
        out_specs=pl.BlockSpec((None, Q, ATT_OUT_W), lambda b, r, qi: (b, qi, r)),
        out_shape=jax.ShapeDtypeStruct((B, sd, dil * ATT_OUT_W), F32),
        compiler_params=_params("parallel", "parallel", "arbitrary"),
        name=f"dilated_attn_g{gi}",
    )(view, view, view, view, view)
    return out.reshape(B * S, ATT_OUT_W)


def _attn_merge_kernel(o0_ref, o1_ref, o2_ref, w_ref, x_ref, g_ref, b_ref, o_ref, *, alpha):
    dh = ATT_HEAD_DIM
    gw = HEADS_PER_GROUP * dh
    refs = (o0_ref, o1_ref, o2_ref)
    lses = [r[:, gw:] for r in refs]
    mx = jnp.maximum(jnp.maximum(lses[0], lses[1]), lses[2])
    ws = [jnp.exp(l - mx) for l in lses]
    den = ws[0] + ws[1] + ws[2]
    ws = [w / den for w in ws]
    heads = []
    for h in range(HEADS_PER_GROUP):
        cs = slice(h * dh, (h + 1) * dh)
        acc = ws[0][:, h:h + 1] * refs[0][:, cs]
        acc += ws[1][:, h:h + 1] * refs[1][:, cs]
        acc += ws[2][:, h:h + 1] * refs[2][:, cs]
        heads.append(acc)
    merged = jnp.concatenate(heads, axis=-1).astype(BF16)
    y = jnp.dot(merged, w_ref[...], preferred_element_type=F32)
    o_ref[...] = _layer_norm(alpha * x_ref[...] + y, g_ref[...], b_ref[...])


def _attn_merge_res_ln(outs, w, x, g, b, alpha):
    M, D = x.shape
    K = w.shape[0]
    tm = min(LN_TM, M)
    assert M % tm == 0
    ospec = pl.BlockSpec((tm, ATT_OUT_W), lambda i: (i, 0))
    return pl.pallas_call(
        functools.partial(_attn_merge_kernel, alpha=alpha),
        grid=(M // tm,),
        in_specs=[ospec, ospec, ospec,
                  pl.BlockSpec((K, D), lambda i: (0, 0)),
                  pl.BlockSpec((tm, D), lambda i: (i, 0)),
                  pl.BlockSpec((1, D), lambda i: (0, 0)),
                  pl.BlockSpec((1, D), lambda i: (0, 0))],
        out_specs=pl.BlockSpec((tm, D), lambda i: (i, 0)),
        out_shape=jax.ShapeDtypeStruct((M, D), F32),
        compiler_params=_params("parallel"),
        name="attn_merge_res_ln",
    )(*outs, w, x, g.reshape(1, D), b.reshape(1, D))


def _router_kernel(x_ref, w_ref, slab_ref, cnt_ref, run_ref):
    tb = x_ref.shape[0]

    @pl.when(pl.program_id(0) == 0)
    def _():
        run_ref[...] = jnp.zeros_like(run_ref)

    logits = jnp.dot(x_ref[...], w_ref[...], preferred_element_type=F32,
                     precision=lax.Precision.HIGHEST)
    lane = lax.broadcasted_iota(jnp.int32, (tb, LANES), 1)
    logits = jnp.where(lane < N_EXPERTS, logits, NEG_BIG)
    m1 = jnp.max(logits, axis=-1, keepdims=True)
    i1 = jnp.min(jnp.where(logits == m1, lane, LANES), axis=-1, keepdims=True)
    rest = jnp.where(lane == i1, NEG_BIG, logits)
    m2 = jnp.max(rest, axis=-1, keepdims=True)
    i2 = jnp.min(jnp.where(rest == m2, lane, LANES), axis=-1, keepdims=True)
    e = jnp.exp(m2 - m1)
    g1 = 1.0 / (1.0 + e)
    g2 = e / (1.0 + e)
    onehot = jnp.where((lane == i1) | (lane == i2), 1.0, 0.0)
    r = lax.broadcasted_iota(jnp.int32, (tb, tb), 0)
    c = lax.broadcasted_iota(jnp.int32, (tb, tb), 1)
    lower = jnp.where(c < r, 1.0, 0.0).astype(BF16)
    before = jnp.dot(lower, onehot.astype(BF16), preferred_element_type=F32) + run_ref[...]
    rank1 = jnp.sum(jnp.where(lane == i1, before, 0.0), axis=-1, keepdims=True)
    rank2 = jnp.sum(jnp.where(lane == i2, before, 0.0), axis=-1, keepdims=True)
    total = run_ref[...] + jnp.sum(onehot, axis=0, keepdims=True)
    run_ref[...] = total
    cnt_ref[...] = total
    slab = jnp.where(lane == 0, i1.astype(F32), 0.0)
    slab = jnp.where(lane == 1, i2.astype(F32), slab)
    slab = jnp.where(lane == 2, g1, slab)
    slab = jnp.where(lane == 3, g2, slab)
    slab = jnp.where(lane == 4, rank1, slab)
    slab = jnp.where(lane == 5, rank2, slab)
    slab_ref[...] = slab


def _router(x, w_router):
    T, D = x.shape
    tb = min(ROUTE_TB, T)
    assert T % tb == 0
    w = jnp.zeros((D, LANES), F32).at[:, :N_EXPERTS].set(w_router)
    return pl.pallas_call(
        _router_kernel,
        grid=(T // tb,),
        in_specs=[pl.BlockSpec((tb, D), lambda i: (i, 0)),
                  pl.BlockSpec((D, LANES), lambda i: (0, 0))],
        out_specs=[pl.BlockSpec((tb, LANES), lambda i: (i, 0)),
                   pl.BlockSpec((1, LANES), lambda i: (0, 0))],
        out_shape=[jax.ShapeDtypeStruct((T, LANES), F32),
                   jax.ShapeDtypeStruct((1, LANES), F32)],
        scratch_shapes=[pltpu.VMEM((1, LANES), F32)],
        compiler_params=_params("arbitrary"),
        name="moe_router",
    )(x, w)


def _expert_kernel(be_ref, cur_ref, nxt_ref, x_hbm, wg_ref, wu_ref, wd_ref, o_ref,
                   xbuf_ref, xb_ref, acc_ref, sem, *, n_f):
    del be_ref
    i, f = pl.program_id(0), pl.program_id(1)
    tm = o_ref.shape[0]
    per = tm // n_f
    slot = lax.rem(i, 2)

    def row_copy(tok, t, s):
        return pltpu.make_async_copy(x_hbm.at[pl.ds(tok, 1)], xbuf_ref.at[s, pl.ds(t, 1)], sem.at[s])

    def drain(s):
        def body(t, carry):
            row_copy(0, t, s).wait()
            return carry
        lax.fori_loop(0, tm, body, 0, unroll=DMA_UNROLL)

    @pl.when((i == 0) & (f == 0))
    def _():
        def body(t, carry):
            row_copy(cur_ref[0, 0, t], t, 0).start()
            return carry
        lax.fori_loop(0, tm, body, 0, unroll=DMA_UNROLL)

    @pl.when(f == 0)
    def _():
        drain(slot)
        xb_ref[...] = xbuf_ref[slot].astype(BF16)
        acc_ref[...] = jnp.zeros_like(acc_ref)

    base = f * per
    for t in range(per):
        row_copy(nxt_ref[0, 0, base + t], base + t, 1 - slot).start()

    xb = xb_ref[...]
    gate = jnp.dot(xb, wg_ref[...], preferred_element_type=F32)
    up = jnp.dot(xb, wu_ref[...], preferred_element_type=F32)
    hid = (gate * jax.nn.sigmoid(gate) * up).astype(BF16)
    acc_ref[...] += jnp.dot(hid, wd_ref[...], preferred_element_type=F32)

    @pl.when(f == n_f - 1)
    def _():
        o_ref[...] = acc_ref[...]

    @pl.when((i == pl.num_programs(0) - 1) & (f == n_f - 1))
    def _():
        drain(1 - slot)


def _experts(x, slot_tok, blk_exp, wg, wu, wd):
    D = x.shape[1]
    P = slot_tok.shape[0]
    F = wg.shape[2]
    tm, tf = MOE_BLOCK, FFN_TF
    assert P % tm == 0 and F % tf == 0 and tm % (F // tf) == 0
    nb, nf = P // tm, F // tf
    toks = slot_tok.reshape(nb, 1, tm)
    grid_spec = pltpu.PrefetchScalarGridSpec(
        num_scalar_prefetch=1,
        grid=(nb, nf),
        in_specs=[pl.BlockSpec((1, 1, tm), lambda i, f, be: (i, 0, 0), memory_space=pltpu.SMEM),
                  pl.BlockSpec((1, 1, tm), lambda i, f, be: (jnp.minimum(i + 1, nb - 1), 0, 0),
                               memory_space=pltpu.SMEM),
                  pl.BlockSpec(memory_space=pl.ANY),
                  pl.BlockSpec((None, D, tf), lambda i, f, be: (be[i], 0, f)),
                  pl.BlockSpec((None, D, tf), lambda i, f, be: (be[i], 0, f)),
                  pl.BlockSpec((None, tf, D), lambda i, f, be: (be[i], f, 0))],
        out_specs=pl.BlockSpec((tm, D), lambda i, f, be: (i, 0)),
        scratch_shapes=[pltpu.VMEM((2, tm, D), F32), pltpu.VMEM((tm, D), BF16), pltpu.VMEM((tm, D), F32),
                        pltpu.SemaphoreType.DMA((2,))],
    )
    return pl.pallas_call(
        functools.partial(_expert_kernel, n_f=nf),
        grid_spec=grid_spec,
        out_shape=jax.ShapeDtypeStruct((P, D), F32),
        compiler_params=_params("arbitrary", "arbitrary"),
        name="moe_experts",
    )(blk_exp, toks, toks, x, wg, wu, wd)


def _combine_kernel(d1_ref, d2_ref, slab_ref, x_ref, g_ref, b_ref, y_hbm, o_ref, buf_ref, sem, *, alpha):
    tb = x_ref.shape[0]

    def row_copy(t, src, slot):
        return pltpu.make_async_copy(y_hbm.at[pl.ds(src, 1)], buf_ref.at[slot, pl.ds(t, 1)], sem)

    def issue(t, carry):
        row_copy(t, d1_ref[0, 0, t], 0).start()
        row_copy(t, d2_ref[0, 0, t], 1).start()
        return carry

    def drain(t, carry):
        row_copy(t, d1_ref[0, 0, t], 0).wait()
        row_copy(t, d2_ref[0, 0, t], 1).wait()
        return carry

    lax.fori_loop(0, tb, issue, 0, unroll=DMA_UNROLL)
    lax.fori_loop(0, tb, drain, 0, unroll=DMA_UNROLL)
    slab = slab_ref[...]
    mix = slab[:, 2:3] * buf_ref[0] + slab[:, 3:4] * buf_ref[1]
    o_ref[...] = _layer_norm(alpha * x_ref[...] + mix, g_ref[...], b_ref[...])


def _combine_res_ln(y, d1, d2, slab, x, g, b, alpha):
    T, D = x.shape
    tb = min(COMB_TB, T)
    assert T % tb == 0
    nb = T // tb
    smem_spec = pl.BlockSpec((1, 1, tb), lambda i: (i, 0, 0), memory_space=pltpu.SMEM)
    return pl.pallas_call(
        functools.partial(_combine_kernel, alpha=alpha),
        grid=(nb,),
        in_specs=[smem_spec, smem_spec,
                  pl.BlockSpec((tb, LANES), lambda i: (i, 0)),
                  pl.BlockSpec((tb, D), lambda i: (i, 0)),
                  pl.BlockSpec((1, D), lambda i: (0, 0)),
                  pl.BlockSpec((1, D), lambda i: (0, 0)),
                  pl.BlockSpec(memory_space=pl.ANY)],
        out_specs=pl.BlockSpec((tb, D), lambda i: (i, 0)),
        out_shape=jax.ShapeDtypeStruct((T, D), F32),
        scratch_shapes=[pltpu.VMEM((2, tb, D), F32), pltpu.SemaphoreType.DMA(())],
        compiler_params=_params("arbitrary"),
        name="moe_combine_res_ln",
    )(d1.reshape(nb, 1, tb), d2.reshape(nb, 1, tb), slab, x, g.reshape(1, D), b.reshape(1, D), y)


def _moe_res_ln(x, w_router, wg, wu, wd, g, b, alpha):
    T, D = x.shape
    slab, counts = _router(x, w_router)
    counts = counts[0, :N_EXPERTS].astype(jnp.int32)
    padded = ((counts + MOE_BLOCK - 1) // MOE_BLOCK) * MOE_BLOCK
    pad_end = jnp.cumsum(padded)
    pad_start = pad_end - padded
    e1 = slab[:, 0].astype(jnp.int32)
    e2 = slab[:, 1].astype(jnp.int32)
    d1 = pad_start[e1] + slab[:, 4].astype(jnp.int32)
    d2 = pad_start[e2] + slab[:, 5].astype(jnp.int32)
    nblk = -(-(2 * T) // MOE_BLOCK) + N_EXPERTS
    blk_start = jnp.arange(nblk, dtype=jnp.int32) * MOE_BLOCK
    blk_exp = jnp.minimum(jnp.sum(pad_end[None, :] <= blk_start[:, None], axis=1), N_EXPERTS - 1)
    tok = jnp.arange(T, dtype=jnp.int32)
    slot_tok = jnp.zeros((nblk * MOE_BLOCK,), jnp.int32).at[d1].set(tok).at[d2].set(tok)
    y = _experts(x, slot_tok, blk_exp.astype(jnp.int32), wg, wu, wd)
    return _combine_res_ln(y, d1, d2, slab, x, g, b, alpha)


def kernel(x, ln_gain, ln_bias, ret_w_in, ret_gn_gain, ret_w_out, att_w_qkv, att_w_out,
           ffn_w_gate, ffn_w_up, ffn_w_down, moe_w_router, moe_w_gate, moe_w_up, moe_w_down):
    B, S, D = x.shape
    depth = ln_gain.shape[0]
    alpha = (2 * depth) ** 0.25
    h = x.reshape(B * S, D)
    for i in range(depth):
        j = i // 2
        if i % 2 == 0:
            proj = _matmul(h, ret_w_in[j].astype(BF16), "ret_in_proj")
            gated = _retention_core(proj, B, S, ret_gn_gain[j])
            h = _mm_res_ln(gated, ret_w_out[j].astype(BF16), h, ln_gain[i, 0], ln_bias[i, 0], alpha,
                           "ret_out_res_ln")
            h = _ffn_res_ln(h, ffn_w_gate[j].astype(BF16), ffn_w_up[j].astype(BF16),
                            ffn_w_down[j].astype(BF16), ln_gain[i, 1], ln_bias[i, 1], alpha)
        else:
            gw = HEADS_PER_GROUP * ATT_HEAD_DIM
            outs = []
            for gi, (win, dil) in enumerate(DIL_PATTERN):
                w_g = jnp.concatenate(
                    [att_w_qkv[j][:, (c * N_GROUPS + gi) * gw:(c * N_GROUPS + gi + 1) * gw] for c in range(3)],
                    axis=1).astype(BF16)
                qkv = _matmul(h, w_g, f"att_qkv_proj_g{gi}")
                outs.append(_dilated_group(qkv, B, S, gi, win, dil))
            h = _attn_merge_res_ln(outs, att_w_out[j].astype(BF16), h, ln_gain[i, 0], ln_bias[i, 0], alpha)
            h = _moe_res_ln(h, moe_w_router[j], moe_w_gate[j].astype(BF16), moe_w_up[j].astype(BF16),
                            moe_w_down[j].astype(BF16), ln_gain[i, 1], ln_bias[i, 1], alpha)
    return h.reshape(B, S, D)
```

```python
import functools

import jax
import jax.numpy as jnp
from jax import lax
from jax.experimental import pallas as pl
from jax.experimental.pallas import tpu as pltpu

F32 = jnp.float32
BF16 = jnp.bfloat16

RET_HEADS = 4
RET_CHUNK = 128
ROPE_BASE = 10000.0
DIL_PATTERN = ((128, 1), (512, 4), (2048, 16))
N_GROUPS = len(DIL_PATTERN)
HEADS_PER_GROUP = 4
ATT_HEAD_DIM = 128
N_EXPERTS = 8
LN_EPS = 1e-5

LANES = 128
VMEM_LIMIT = 56 * 1024 * 1024

MM_TM = 1024
MM_TN = 2048
LN_TM = 512
FFN_TM = 1024
FFN_TF = 512
RET_ROWS = 512
ATT_Q = 128
ATT_QB = 512
ATT_OUT_W = HEADS_PER_GROUP * ATT_HEAD_DIM + LANES
MOE_BLOCK = 896
ROUTE_TB = 512
COMB_TB = 256
DMA_UNROLL = 8
NEG_BIG = -1e30


def _params(*sem):
    return pltpu.CompilerParams(dimension_semantics=sem, vmem_limit_bytes=VMEM_LIMIT)


def _layer_norm(y, g, b):
    mu = jnp.mean(y, axis=-1, keepdims=True)
    d = y - mu
    var = jnp.mean(d * d, axis=-1, keepdims=True)
    return d * lax.rsqrt(var + LN_EPS) * g + b


def _mm_kernel(x_ref, w_ref, o_ref, xb_ref):
    @pl.when(pl.program_id(1) == 0)
    def _():
        xb_ref[...] = x_ref[...].astype(BF16)

    o_ref[...] = jnp.dot(xb_ref[...], w_ref[...], preferred_element_type=F32).astype(o_ref.dtype)


def _matmul(x, w, name):
    M, K = x.shape
    N = w.shape[1]
    tm = min(MM_TM, M)
    tn = max(t for t in range(LANES, MM_TN + 1, LANES) if N % t == 0)
    assert M % tm == 0
    return pl.pallas_call(
        _mm_kernel,
        grid=(M // tm, N // tn),
        in_specs=[pl.BlockSpec((tm, K), lambda i, j: (i, 0)),
                  pl.BlockSpec((K, tn), lambda i, j: (0, j))],
        out_specs=pl.BlockSpec((tm, tn), lambda i, j: (i, j)),
        out_shape=jax.ShapeDtypeStruct((M, N), BF16),
        scratch_shapes=[pltpu.VMEM((tm, K), BF16)],
        compiler_params=_params("parallel", "arbitrary"),
        name=name,
    )(x, w)


def _ret_kernel(q_ref, k_ref, v_ref, g_ref, cos_ref, sin_ref, dec_ref, qdec_ref, kdec_ref,
                cdec_ref, gain_ref, o_ref, state_ref, *, n_chunks, dk):
    C, H = RET_CHUNK, RET_HEADS
    half = dk // 2
    dv = 2 * dk

    @pl.when(pl.program_id(1) == 0)
    def _():
        state_ref[...] = jnp.zeros_like(state_ref)

    def rot(t, cos, sin):
        t1, t2 = t[:, :half], t[:, half:]
        return jnp.concatenate([t1 * cos - t2 * sin, t2 * cos + t1 * sin], axis=-1)

    def chunk(c, carry):
        rows = pl.ds(pl.multiple_of(c * C, C), C)
        cos, sin = cos_ref[rows, :], sin_ref[rows, :]
        for h in range(H):
            kc = slice(h * dk, (h + 1) * dk)
            vc = slice(h * dv, (h + 1) * dv)
            q = rot(q_ref[rows, kc].astype(F32), cos, sin)
            k = rot(k_ref[rows, kc].astype(F32), cos, sin) * (dk ** -0.5)
            v = v_ref[rows, vc]
            scores = lax.dot_general(q.astype(BF16), k.astype(BF16), (((1,), (1,)), ((), ())),
                                     preferred_element_type=F32) * dec_ref[h]
            intra = jnp.dot(scores.astype(BF16), v, preferred_element_type=F32)
            state = state_ref[h]
            cross = jnp.dot((q * qdec_ref[h]).astype(BF16), state.astype(BF16),
                            preferred_element_type=F32)
            kn_t = (k * kdec_ref[h]).T.astype(BF16)
            state_ref[h] = cdec_ref[h] * state + jnp.dot(kn_t, v, preferred_element_type=F32)
            r = intra + cross
            mu = jnp.mean(r, axis=-1, keepdims=True)
            d = r - mu
            var = jnp.mean(d * d, axis=-1, keepdims=True)
            normed = d * lax.rsqrt(var + LN_EPS) * gain_ref[:, vc]
            gate = g_ref[rows, vc].astype(F32)
            o_ref[rows, vc] = (gate * jax.nn.sigmoid(gate) * normed).astype(o_ref.dtype)
        return carry

    lax.fori_loop(0, n_chunks, chunk, 0)


def _retention_core(proj, B, S, gn_gain):
    H, C = RET_HEADS, RET_CHUNK
    cols = proj.shape[1]
    dk = cols // (6 * H)
    dv = 2 * dk
    rb = min(RET_ROWS, S)
    assert S % rb == 0 and rb % C == 0
    nr = S // rb

    half = dk // 2
    inv = ROPE_BASE ** (-jnp.arange(half, dtype=F32) / half)
    ang = jnp.arange(S).astype(F32)[:, None] * inv[None, :]
    cos, sin = jnp.cos(ang), jnp.sin(ang)
    log_gamma = jnp.log(1.0 - 2.0 ** (-5.0 - jnp.arange(H, dtype=F32)))
    idx = jnp.arange(C, dtype=F32)
    rel = idx[:, None] - idx[None, :]
    decay_intra = jnp.where(rel >= 0, jnp.exp(log_gamma[:, None, None] * jnp.maximum(rel, 0.0)), 0.0)
    q_dec = jnp.exp(log_gamma[None, :] * (idx[:, None] + 1.0))
    k_dec = jnp.exp(log_gamma[None, :] * (C - 1.0 - idx[:, None]))
    chunk_dec = jnp.exp(log_gamma * C)
    qdec_b = jnp.broadcast_to(q_dec.T[:, :, None], (H, C, dk))
    kdec_b = jnp.broadcast_to(k_dec.T[:, :, None], (H, C, dk))
    cdec_b = jnp.broadcast_to(chunk_dec[:, None, None], (H, 1, dv))
    gain = gn_gain.reshape(1, H * dv)

    full = lambda shape: pl.BlockSpec(shape, lambda b, i: (0,) * len(shape))
    kern = functools.partial(_ret_kernel, n_chunks=rb // C, dk=dk)
    return pl.pallas_call(
        kern,
        grid=(B, nr),
        in_specs=[
            pl.BlockSpec((rb, H * dk), lambda b, i: (b * nr + i, 0)),
            pl.BlockSpec((rb, H * dk), lambda b, i: (b * nr + i, 1)),
            pl.BlockSpec((rb, H * dv), lambda b, i: (b * nr + i, 1)),
            pl.BlockSpec((rb, H * dv), lambda b, i: (b * nr + i, 2)),
            pl.BlockSpec((rb, half), lambda b, i: (i, 0)),
            pl.BlockSpec((rb, half), lambda b, i: (i, 0)),
            full((H, C, C)),
            full((H, C, dk)),
            full((H, C, dk)),
            full((H, 1, dv)),
            full((1, H * dv)),
        ],
        out_specs=pl.BlockSpec((rb, H * dv), lambda b, i: (b * nr + i, 0)),
        out_shape=jax.ShapeDtypeStruct((B * S, H * dv), BF16),
        scratch_shapes=[pltpu.VMEM((H, dk, dv), F32)],
        compiler_params=_params("parallel", "arbitrary"),
        name="retention_core",
    )(proj, proj, proj, proj, cos, sin, decay_intra, qdec_b, kdec_b, cdec_b, gain)


def _mm_res_ln_kernel(a_ref, w_ref, x_ref, g_ref, b_ref, o_ref, *, alpha):
    y = jnp.dot(a_ref[...], w_ref[...], preferred_element_type=F32)
    o_ref[...] = _layer_norm(alpha * x_ref[...] + y, g_ref[...], b_ref[...])


def _mm_res_ln(a, w, x, g, b, alpha, name):
    M, K = a.shape
    D = w.shape[1]
    tm = min(LN_TM, M)
    assert M % tm == 0
    return pl.pallas_call(
        functools.partial(_mm_res_ln_kernel, alpha=alpha),
        grid=(M // tm,),
        in_specs=[pl.BlockSpec((tm, K), lambda i: (i, 0)),
                  pl.BlockSpec((K, D), lambda i: (0, 0)),
                  pl.BlockSpec((tm, D), lambda i: (i, 0)),
                  pl.BlockSpec((1, D), lambda i: (0, 0)),
                  pl.BlockSpec((1, D), lambda i: (0, 0))],
        out_specs=pl.BlockSpec((tm, D), lambda i: (i, 0)),
        out_shape=jax.ShapeDtypeStruct((M, D), F32),
        compiler_params=_params("parallel"),
        name=name,
    )(a, w, x, g.reshape(1, D), b.reshape(1, D))


def _ffn_kernel(x_ref, wg_ref, wu_ref, wd_ref, g_ref, b_ref, o_ref, xb_ref, acc_ref, *, alpha):
    f = pl.program_id(1)

    @pl.when(f == 0)
    def _():
        xb_ref[...] = x_ref[...].astype(BF16)
        acc_ref[...] = jnp.zeros_like(acc_ref)

    xb = xb_ref[...]
    gate = jnp.dot(xb, wg_ref[...], preferred_element_type=F32)
    up = jnp.dot(xb, wu_ref[...], preferred_element_type=F32)
    hid = (gate * jax.nn.sigmoid(gate) * up).astype(BF16)
    acc_ref[...] += jnp.dot(hid, wd_ref[...], preferred_element_type=F32)

    @pl.when(f == pl.num_programs(1) - 1)
    def _():
        o_ref[...] = _layer_norm(alpha * x_ref[...] + acc_ref[...], g_ref[...], b_ref[...])


def _ffn_res_ln(x, wg, wu, wd, g, b, alpha):
    M, D = x.shape
    F = wg.shape[1]
    tm, tf = min(FFN_TM, M), FFN_TF
    assert M % tm == 0 and F % tf == 0
    return pl.pallas_call(
        functools.partial(_ffn_kernel, alpha=alpha),
        grid=(M // tm, F // tf),
        in_specs=[pl.BlockSpec((tm, D), lambda i, f: (i, 0)),
                  pl.BlockSpec((D, tf), lambda i, f: (0, f)),
                  pl.BlockSpec((D, tf), lambda i, f: (0, f)),
                  pl.BlockSpec((tf, D), lambda i, f: (f, 0)),
                  pl.BlockSpec((1, D), lambda i, f: (0, 0)),
                  pl.BlockSpec((1, D), lambda i, f: (0, 0))],
        out_specs=pl.BlockSpec((tm, D), lambda i, f: (i, 0)),
        out_shape=jax.ShapeDtypeStruct((M, D), F32),
        scratch_shapes=[pltpu.VMEM((tm, D), BF16), pltpu.VMEM((tm, D), F32)],
        compiler_params=_params("parallel", "arbitrary"),
        name="ffn_res_ln",
    )(x, wg, wu, wd, g.reshape(1, D), b.reshape(1, D))


def _attn_kernel(q_ref, kp_ref, kc_ref, vp_ref, vc_ref, o_ref):
    Q, dh = ATT_Q, ATT_HEAD_DIM
    n_sub = q_ref.shape[0] // Q
    qi = pl.program_id(2)
    row = lax.broadcasted_iota(jnp.int32, (Q, 2 * Q), 0)
    col = lax.broadcasted_iota(jnp.int32, (Q, 2 * Q), 1)
    band = (col >= row) & (col <= row + Q)
    first = jnp.where(qi > 0, 0, Q)
    lane = lax.broadcasted_iota(jnp.int32, (Q, LANES), 1)
    for j in range(n_sub):
        rs = slice(j * Q, (j + 1) * Q)
        ks = slice((j - 1) * Q, (j + 1) * Q)
        valid = (band & (col >= first)) if j == 0 else band
        lse_slab = jnp.zeros((Q, LANES), F32)
        for h in range(HEADS_PER_GROUP):
            cs = slice(h * dh, (h + 1) * dh)
            if j == 0:
                k = jnp.concatenate([kp_ref[:, cs], kc_ref[:Q, cs]], axis=0)
                v = jnp.concatenate([vp_ref[:, cs], vc_ref[:Q, cs]], axis=0)
            else:
                k, v = kc_ref[ks, cs], vc_ref[ks, cs]
            s = lax.dot_general(q_ref[rs, cs], k, (((1,), (1,)), ((), ())),
                                preferred_element_type=F32) * (dh ** -0.5)
            s = jnp.where(valid, s, NEG_BIG)
            m = jnp.max(s, axis=-1, keepdims=True)
            p = jnp.exp(s - m)
            l = jnp.sum(p, axis=-1, keepdims=True)
            o = jnp.dot(p.astype(BF16), v, preferred_element_type=F32)
            o_ref[rs, cs] = o / l
            lse_slab = jnp.where(lane == h, m + jnp.log(l), lse_slab)
        o_ref[rs, HEADS_PER_GROUP * dh:] = lse_slab


def _dilated_group(qkv, B, S, gi, win, dil):
    Q = ATT_Q
    gw = HEADS_PER_GROUP * ATT_HEAD_DIM
    nblk = qkv.shape[1] // gw
    sd = S // dil
    qb = min(ATT_QB, sd)
    assert win // dil == Q and S % dil == 0 and sd % qb == 0 and qb % Q == 0
    n_sub = qb // Q
    view = qkv.reshape(B, sd, dil * qkv.shape[1])
    cur = lambda comp: pl.BlockSpec((None, qb, gw), lambda b, r, qi: (b, qi, r * nblk + comp))
    prev = lambda comp: pl.BlockSpec(
        (None, Q, gw), lambda b, r, qi: (b, jnp.maximum(qi * n_sub - 1, 0), r * nblk + comp))
    out = pl.pallas_call(
        _attn_kernel,
        grid=(B, dil, sd // qb),
        in_specs=[cur(0), prev(1), cur(1), prev(2), cur(2)],
        out_specs=pl.BlockSpec((None, qb, ATT_OUT_W), lambda b, r, qi: (b, qi, r)),
        out_shape=jax.ShapeDtypeStruct((B, sd, dil * ATT_OUT_W), F32),
        compiler_params=_params("parallel", "parallel", "arbitrary"),
        name=f"dilated_attn_g{gi}",
    )(view, view, view, view, view)
    return out.reshape(B * S, ATT_OUT_W)


def _attn_merge_kernel(o0_ref, o1_ref, o2_ref, w_ref, x_ref, g_ref, b_ref, o_ref, *, alpha):
    dh = ATT_HEAD_DIM
    gw = HEADS_PER_GROUP * dh
    refs = (o0_ref, o1_ref, o2_ref)
    lses = [r[:, gw:] for r in refs]
    mx = jnp.maximum(jnp.maximum(lses[0], lses[1]), lses[2])
    ws = [jnp.exp(l - mx) for l in lses]
    den = ws[0] + ws[1] + ws[2]
    ws = [w / den for w in ws]
    heads = []
    for h in range(HEADS_PER_GROUP):
        cs = slice(h * dh, (h + 1) * dh)
        acc = ws[0][:, h:h + 1] * refs[0][:, cs]
        acc += ws[1][:, h:h + 1] * refs[1][:, cs]
        acc += ws[2][:, h:h + 1] * refs[2][:, cs]
        heads.append(acc)
    merged = jnp.concatenate(heads, axis=-1).astype(BF16)
    y = jnp.dot(merged, w_ref[...], preferred_element_type=F32)
    o_ref[...] = _layer_norm(alpha * x_ref[...] + y, g_ref[...], b_ref[...])


def _attn_merge_res_ln(outs, w, x, g, b, alpha):
    M, D = x.shape
    K = w.shape[0]
    tm = min(LN_TM, M)
    assert M % tm == 0
    ospec = pl.BlockSpec((tm, ATT_OUT_W), lambda i: (i, 0))
    return pl.pallas_call(
        functools.partial(_attn_merge_kernel, alpha=alpha),
        grid=(M // tm,),
        in_specs=[ospec, ospec, ospec,
                  pl.BlockSpec((K, D), lambda i: (0, 0)),
                  pl.BlockSpec((tm, D), lambda i: (i, 0)),
                  pl.BlockSpec((1, D), lambda i: (0, 0)),
                  pl.BlockSpec((1, D), lambda i: (0, 0))],
        out_specs=pl.BlockSpec((tm, D), lambda i: (i, 0)),
        out_shape=jax.ShapeDtypeStruct((M, D), F32),
        compiler_params=_params("parallel"),
        name="attn_merge_res_ln",
    )(*outs, w, x, g.reshape(1, D), b.reshape(1, D))


def _router_kernel(x_ref, w_ref, slab_ref, cnt_ref, run_ref):
    tb = x_ref.shape[0]

    @pl.when(pl.program_id(0) == 0)
    def _():
        run_ref[...] = jnp.zeros_like(run_ref)

    logits = jnp.dot(x_ref[...], w_ref[...], preferred_element_type=F32,
                     precision=lax.Precision.HIGHEST)
    lane = lax.broadcasted_iota(jnp.int32, (tb, LANES), 1)
    logits = jnp.where(lane < N_EXPERTS, logits, NEG_BIG)
    m1 = jnp.max(logits, axis=-1, keepdims=True)
    i1 = jnp.min(jnp.where(logits == m1, lane, LANES), axis=-1, keepdims=True)
    rest = jnp.where(lane == i1, NEG_BIG, logits)
    m2 = jnp.max(rest, axis=-1, keepdims=True)
    i2 = jnp.min(jnp.where(rest == m2, lane, LANES), axis=-1, keepdims=True)
    e = jnp.exp(m2 - m1)
    g1 = 1.0 / (1.0 + e)
    g2 = e / (1.0 + e)
    onehot = jnp.where((lane == i1) | (lane == i2), 1.0, 0.0)
    r = lax.broadcasted_iota(jnp.int32, (tb, tb), 0)
    c = lax.broadcasted_iota(jnp.int32, (tb, tb), 1)
    lower = jnp.where(c < r, 1.0, 0.0).astype(BF16)
    before = jnp.dot(lower, onehot.astype(BF16), preferred_element_type=F32) + run_ref[...]
    rank1 = jnp.sum(jnp.where(lane == i1, before, 0.0), axis=-1, keepdims=True)
    rank2 = jnp.sum(jnp.where(lane == i2, before, 0.0), axis=-1, keepdims=True)
    total = run_ref[...] + jnp.sum(onehot, axis=0, keepdims=True)
    run_ref[...] = total
    cnt_ref[...] = total
    slab = jnp.where(lane == 0, i1.astype(F32), 0.0)
    slab = jnp.where(lane == 1, i2.astype(F32), slab)
    slab = jnp.where(lane == 2, g1, slab)
    slab = jnp.where(lane == 3, g2, slab)
    slab = jnp.where(lane == 4, rank1, slab)
    slab = jnp.where(lane == 5, rank2, slab)
    slab_ref[...] = slab


def _router(x, w_router):
    T, D = x.shape
    tb = min(ROUTE_TB, T)
    assert T % tb == 0
    w = jnp.zeros((D, LANES), F32).at[:, :N_EXPERTS].set(w_router)
    return pl.pallas_call(
        _router_kernel,
        grid=(T // tb,),
        in_specs=[pl.BlockSpec((tb, D), lambda i: (i, 0)),
                  pl.BlockSpec((D, LANES), lambda i: (0, 0))],
        out_specs=[pl.BlockSpec((tb, LANES), lambda i: (i, 0)),
                   pl.BlockSpec((1, LANES), lambda i: (0, 0))],
        out_shape=[jax.ShapeDtypeStruct((T, LANES), F32),
                   jax.ShapeDtypeStruct((1, LANES), F32)],
        scratch_shapes=[pltpu.VMEM((1, LANES), F32)],
        compiler_params=_params("arbitrary"),
        name="moe_router",
    )(x, w)


def _expert_kernel(be_ref, na_ref, cur_ref, nxt_ref, x_hbm, wg_ref, wu_ref, wd_ref, o_ref,
                   xbuf_ref, xb_ref, acc_ref, sem, *, n_f):
    del be_ref
    i, f = pl.program_id(0), pl.program_id(1)
    n_active = na_ref[0]
    tm = o_ref.shape[0]
    per = tm // n_f
    slot = lax.rem(i, 2)

    def row_copy(tok, t, s):
        return pltpu.make_async_copy(x_hbm.at[pl.ds(tok, 1)], xbuf_ref.at[s, pl.ds(t, 1)], sem.at[s])

    def drain(s):
        def body(t, carry):
            row_copy(0, t, s).wait()
            return carry
        lax.fori_loop(0, tm, body, 0, unroll=DMA_UNROLL)

    @pl.when(i < n_active)
    def _():
        @pl.when((i == 0) & (f == 0))
        def _():
            def body(t, carry):
                row_copy(cur_ref[0, 0, t], t, 0).start()
                return carry
            lax.fori_loop(0, tm, body, 0, unroll=DMA_UNROLL)

        @pl.when(f == 0)
        def _():
            drain(slot)
            xb_ref[...] = xbuf_ref[slot].astype(BF16)
            acc_ref[...] = jnp.zeros_like(acc_ref)

        base = f * per
        for t in range(per):
            row_copy(nxt_ref[0, 0, base + t], base + t, 1 - slot).start()

        xb = xb_ref[...]
        gate = jnp.dot(xb, wg_ref[...], preferred_element_type=F32)
        up = jnp.dot(xb, wu_ref[...], preferred_element_type=F32)
        hid = (gate * jax.nn.sigmoid(gate) * up).astype(BF16)
        acc_ref[...] += jnp.dot(hid, wd_ref[...], preferred_element_type=F32)

        @pl.when(f == n_f - 1)
        def _():
            o_ref[...] = acc_ref[...]

        @pl.when((i == n_active - 1) & (f == n_f - 1))
        def _():
            drain(1 - slot)

    @pl.when((i >= n_active) & (f == n_f - 1))
    def _():
        o_ref[...] = jnp.zeros_like(o_ref)


def _experts(x, slot_tok, blk_exp, n_active, wg, wu, wd):
    D = x.shape[1]
    P = slot_tok.shape[0]
    F = wg.shape[2]
    tm, tf = MOE_BLOCK, FFN_TF
    assert P % tm == 0 and F % tf == 0 and tm % (F // tf) == 0
    nb, nf = P // tm, F // tf
    toks = slot_tok.reshape(nb, 1, tm)
    grid_spec = pltpu.PrefetchScalarGridSpec(
        num_scalar_prefetch=2,
        grid=(nb, nf),
        in_specs=[pl.BlockSpec((1, 1, tm), lambda i, f, be, na: (i, 0, 0), memory_space=pltpu.SMEM),
                  pl.BlockSpec((1, 1, tm), lambda i, f, be, na: (jnp.minimum(i + 1, nb - 1), 0, 0),
                               memory_space=pltpu.SMEM),
                  pl.BlockSpec(memory_space=pl.ANY),
                  pl.BlockSpec((None, D, tf), lambda i, f, be, na: (be[i], 0, f)),
                  pl.BlockSpec((None, D, tf), lambda i, f, be, na: (be[i], 0, f)),
                  pl.BlockSpec((None, tf, D), lambda i, f, be, na: (be[i], f, 0))],
        out_specs=pl.BlockSpec((tm, D), lambda i, f, be, na: (i, 0)),
        scratch_shapes=[pltpu.VMEM((2, tm, D), F32), pltpu.VMEM((tm, D), BF16), pltpu.VMEM((tm, D), F32),
                        pltpu.SemaphoreType.DMA((2,))],
    )
    return pl.pallas_call(
        functools.partial(_expert_kernel, n_f=nf),
        grid_spec=grid_spec,
        out_shape=jax.ShapeDtypeStruct((P, D), F32),
        compiler_params=_params("arbitrary", "arbitrary"),
        name="moe_experts",
    )(blk_exp, n_active, toks, toks, x, wg, wu, wd)


def _combine_kernel(d1_ref, d2_ref, slab_ref, x_ref, g_ref, b_ref, y_hbm, o_ref, buf_ref, sem, *, alpha):
    tb = x_ref.shape[0]

    def row_copy(t, src, slot):
        return pltpu.make_async_copy(y_hbm.at[pl.ds(src, 1)], buf_ref.at[slot, pl.ds(t, 1)], sem)

    def issue(t, carry):
        row_copy(t, d1_ref[0, 0, t], 0).start()
        row_copy(t, d2_ref[0, 0, t], 1).start()
        return carry

    def drain(t, carry):
        row_copy(t, d1_ref[0, 0, t], 0).wait()
        row_copy(t, d2_ref[0, 0, t], 1).wait()
        return carry

    lax.fori_loop(0, tb, issue, 0, unroll=DMA_UNROLL)
    lax.fori_loop(0, tb, drain, 0, unroll=DMA_UNROLL)
    slab = slab_ref[...]
    mix = slab[:, 2:3] * buf_ref[0] + slab[:, 3:4] * buf_ref[1]
    o_ref[...] = _layer_norm(alpha * x_ref[...] + mix, g_ref[...], b_ref[...])


def _combine_res_ln(y, d1, d2, slab, x, g, b, alpha):
    T, D = x.shape
    tb = min(COMB_TB, T)
    assert T % tb == 0
    nb = T // tb
    smem_spec = pl.BlockSpec((1, 1, tb), lambda i: (i, 0, 0), memory_space=pltpu.SMEM)
    return pl.pallas_call(
        functools.partial(_combine_kernel, alpha=alpha),
        grid=(nb,),
        in_specs=[smem_spec, smem_spec,
                  pl.BlockSpec((tb, LANES), lambda i: (i, 0)),
                  pl.BlockSpec((tb, D), lambda i: (i, 0)),
                  pl.BlockSpec((1, D), lambda i: (0, 0)),
                  pl.BlockSpec((1, D), lambda i: (0, 0)),
                  pl.BlockSpec(memory_space=pl.ANY)],
        out_specs=pl.BlockSpec((tb, D), lambda i: (i, 0)),
        out_shape=jax.ShapeDtypeStruct((T, D), F32),
        scratch_shapes=[pltpu.VMEM((2, tb, D), F32), pltpu.SemaphoreType.DMA(())],
        compiler_params=_params("arbitrary"),
        name="moe_combine_res_ln",
    )(d1.reshape(nb, 1, tb), d2.reshape(nb, 1, tb), slab, x, g.reshape(1, D), b.reshape(1, D), y)


def _moe_res_ln(x, w_router, wg, wu, wd, g, b, alpha):
    T, D = x.shape
    slab, counts = _router(x, w_router)
    counts = counts[0, :N_EXPERTS].astype(jnp.int32)
    padded = ((counts + MOE_BLOCK - 1) // MOE_BLOCK) * MOE_BLOCK
    pad_end = jnp.cumsum(padded)
    pad_start = pad_end - padded
    e1 = slab[:, 0].astype(jnp.int32)
    e2 = slab[:, 1].astype(jnp.int32)
    d1 = pad_start[e1] + slab[:, 4].astype(jnp.int32)
    d2 = pad_start[e2] + slab[:, 5].astype(jnp.int32)
    nblk = -(-(2 * T) // MOE_BLOCK) + N_EXPERTS
    blk_start = jnp.arange(nblk, dtype=jnp.int32) * MOE_BLOCK
    blk_exp = jnp.minimum(jnp.sum(pad_end[None, :] <= blk_start[:, None], axis=1), N_EXPERTS - 1)
    tok = jnp.arange(T, dtype=jnp.int32)
    slot_tok = jnp.zeros((nblk * MOE_BLOCK,), jnp.int32).at[jnp.concatenate([d1, d2])].set(
        jnp.concatenate([tok, tok]), unique_indices=True)
    n_active = (pad_end[-1:] // MOE_BLOCK).astype(jnp.int32)
    y = _experts(x, slot_tok, blk_exp.astype(jnp.int32), n_active, wg, wu, wd)
    return _combine_res_ln(y, d1, d2, slab, x, g, b, alpha)


def kernel(x, ln_gain, ln_bias, ret_w_in, ret_gn_gain, ret_w_out, att_w_qkv, att_w_out,
           ffn_w_gate, ffn_w_up, ffn_w_down, moe_w_router, moe_w_gate, moe_w_up, moe_w_down):
    B, S, D = x.shape
    depth = ln_gain.shape[0]
    alpha = (2 * depth) ** 0.25
    h = x.reshape(B * S, D)
    for i in range(depth):
        j = i // 2
        if i % 2 == 0:
            proj = _matmul(h, ret_w_in[j].astype(BF16), "ret_in_proj")
            gated = _retention_core(proj, B, S, ret_gn_gain[j])
            h = _mm_res_ln(gated, ret_w_out[j].astype(BF16), h, ln_gain[i, 0], ln_bias[i, 0], alpha,
                           "ret_out_res_ln")
            h = _ffn_res_ln(h, ffn_w_gate[j].astype(BF16), ffn_w_up[j].astype(BF16),
                            ffn_w_down[j].astype(BF16), ln_gain[i, 1], ln_bias[i, 1], alpha)
        else:
            gw = HEADS_PER_GROUP * ATT_HEAD_DIM
            outs = []
            for gi, (win, dil) in enumerate(DIL_PATTERN):
                w_g = jnp.concatenate(
                    [att_w_qkv[j][:, (c * N_GROUPS + gi) * gw:(c * N_GROUPS + gi + 1) * gw] for c in range(3)],
                    axis=1).astype(BF16)
                qkv = _matmul(h, w_g, f"att_qkv_proj_g{gi}")
                outs.append(_dilated_group(qkv, B, S, gi, win, dil))
            h = _attn_merge_res_ln(outs, att_w_out[j].astype(BF16), h, ln_gain[i, 0], ln_bias[i, 0], alpha)
            h = _moe_res_ln(h, moe_w_router[j], moe_w_gate[j].astype(BF16), moe_w_up[j].astype(BF16),
                            moe_w_down[j].astype(BF16), ln_gain[i, 1], ln_bias[i, 1], alpha)
    return h.reshape(B, S, D)
```

```python
import functools

import jax
import jax.numpy as jnp
from jax import lax
from jax.experimental import pallas as pl
from jax.experimental.pallas import tpu as pltpu

F32 = jnp.float32
BF16 = jnp.bfloat16

RET_HEADS = 4
RET_CHUNK = 128
ROPE_BASE = 10000.0
DIL_PATTERN = ((128, 1), (512, 4), (2048, 16))
N_GROUPS = len(DIL_PATTERN)
HEADS_PER_GROUP = 4
ATT_HEAD_DIM = 128
N_EXPERTS = 8
LN_EPS = 1e-5

LANES = 128
VMEM_LIMIT = 56 * 1024 * 1024

MM_TM = 1024
MM_TN = 2048
LN_TM = 512
FFN_TM = 1024
FFN_TF = 512
EXP_TF = 512
EXP_SUB = 512
RET_ROWS = 512
ATT_Q = 128
ATT_QB = 512
ATT_OUT_W = HEADS_PER_GROUP * ATT_HEAD_DIM + LANES
MOE_BLOCK = 896
ROUTE_TB = 512
COMB_TB = 512
DMA_UNROLL = 8
NEG_BIG = -1e30


def _params(*sem):
    return pltpu.CompilerParams(dimension_semantics=sem, vmem_limit_bytes=VMEM_LIMIT)


def _layer_norm(y, g, b):
    mu = jnp.mean(y, axis=-1, keepdims=True)
    d = y - mu
    var = jnp.mean(d * d, axis=-1, keepdims=True)
    return d * lax.rsqrt(var + LN_EPS) * g + b


def _mm_kernel(x_ref, w_ref, o_ref, xb_ref):
    @pl.when(pl.program_id(1) == 0)
    def _():
        xb_ref[...] = x_ref[...].astype(BF16)

    o_ref[...] = jnp.dot(xb_ref[...], w_ref[...], preferred_element_type=F32).astype(o_ref.dtype)


def _matmul(x, w, name):
    M, K = x.shape
    N = w.shape[1]
    tm = min(MM_TM, M)
    tn = max(t for t in range(LANES, MM_TN + 1, LANES) if N % t == 0)
    assert M % tm == 0
    return pl.pallas_call(
        _mm_kernel,
        grid=(M // tm, N // tn),
        in_specs=[pl.BlockSpec((tm, K), lambda i, j: (i, 0)),
                  pl.BlockSpec((K, tn), lambda i, j: (0, j))],
        out_specs=pl.BlockSpec((tm, tn), lambda i, j: (i, j)),
        out_shape=jax.ShapeDtypeStruct((M, N), BF16),
        scratch_shapes=[pltpu.VMEM((tm, K), BF16)],
        compiler_params=_params("parallel", "arbitrary"),
        name=name,
    )(x, w)


def _ret_kernel(q_ref, k_ref, v_ref, g_ref, cos_ref, sin_ref, dec_ref, qdec_ref, kdec_ref,
                cdec_ref, gain_ref, o_ref, state_ref, *, n_chunks, dk):
    C, H = RET_CHUNK, RET_HEADS
    half = dk // 2
    dv = 2 * dk

    @pl.when(pl.program_id(1) == 0)
    def _():
        state_ref[...] = jnp.zeros_like(state_ref)

    def rot(t, cos, sin):
        t1, t2 = t[:, :half], t[:, half:]
        return jnp.concatenate([t1 * cos - t2 * sin, t2 * cos + t1 * sin], axis=-1)

    def chunk(c, carry):
        rows = pl.ds(pl.multiple_of(c * C, C), C)
        cos, sin = cos_ref[rows, :], sin_ref[rows, :]
        for h in range(H):
            kc = slice(h * dk, (h + 1) * dk)
            vc = slice(h * dv, (h + 1) * dv)
            q = rot(q_ref[rows, kc].astype(F32), cos, sin)
            k = rot(k_ref[rows, kc].astype(F32), cos, sin) * (dk ** -0.5)
            v = v_ref[rows, vc]
            scores = lax.dot_general(q.astype(BF16), k.astype(BF16), (((1,), (1,)), ((), ())),
                                     preferred_element_type=F32) * dec_ref[h]
            intra = jnp.dot(scores.astype(BF16), v, preferred_element_type=F32)
            state = state_ref[h]
            cross = jnp.dot((q * qdec_ref[h]).astype(BF16), state.astype(BF16),
                            preferred_element_type=F32)
            kn_t = (k * kdec_ref[h]).T.astype(BF16)
            state_ref[h] = cdec_ref[h] * state + jnp.dot(kn_t, v, preferred_element_type=F32)
            r = intra + cross
            mu = jnp.mean(r, axis=-1, keepdims=True)
            d = r - mu
            var = jnp.mean(d * d, axis=-1, keepdims=True)
            normed = d * lax.rsqrt(var + LN_EPS) * gain_ref[:, vc]
            gate = g_ref[rows, vc].astype(F32)
            o_ref[rows, vc] = (gate * jax.nn.sigmoid(gate) * normed).astype(o_ref.dtype)
        return carry

    lax.fori_loop(0, n_chunks, chunk, 0)


def _retention_core(proj, B, S, gn_gain):
    H, C = RET_HEADS, RET_CHUNK
    cols = proj.shape[1]
    dk = cols // (6 * H)
    dv = 2 * dk
    rb = min(RET_ROWS, S)
    assert S % rb == 0 and rb % C == 0
    nr = S // rb

    half = dk // 2
    inv = ROPE_BASE ** (-jnp.arange(half, dtype=F32) / half)
    ang = jnp.arange(S).astype(F32)[:, None] * inv[None, :]
    cos, sin = jnp.cos(ang), jnp.sin(ang)
    log_gamma = jnp.log(1.0 - 2.0 ** (-5.0 - jnp.arange(H, dtype=F32)))
    idx = jnp.arange(C, dtype=F32)
    rel = idx[:, None] - idx[None, :]
    decay_intra = jnp.where(rel >= 0, jnp.exp(log_gamma[:, None, None] * jnp.maximum(rel, 0.0)), 0.0)
    q_dec = jnp.exp(log_gamma[None, :] * (idx[:, None] + 1.0))
    k_dec = jnp.exp(log_gamma[None, :] * (C - 1.0 - idx[:, None]))
    chunk_dec = jnp.exp(log_gamma * C)
    qdec_b = jnp.broadcast_to(q_dec.T[:, :, None], (H, C, dk))
    kdec_b = jnp.broadcast_to(k_dec.T[:, :, None], (H, C, dk))
    cdec_b = jnp.broadcast_to(chunk_dec[:, None, None], (H, 1, dv))
    gain = gn_gain.reshape(1, H * dv)

    full = lambda shape: pl.BlockSpec(shape, lambda b, i: (0,) * len(shape))
    kern = functools.partial(_ret_kernel, n_chunks=rb // C, dk=dk)
    return pl.pallas_call(
        kern,
        grid=(B, nr),
        in_specs=[
            pl.BlockSpec((rb, H * dk), lambda b, i: (b * nr + i, 0)),
            pl.BlockSpec((rb, H * dk), lambda b, i: (b * nr + i, 1)),
            pl.BlockSpec((rb, H * dv), lambda b, i: (b * nr + i, 1)),
            pl.BlockSpec((rb, H * dv), lambda b, i: (b * nr + i, 2)),
            pl.BlockSpec((rb, half), lambda b, i: (i, 0)),
            pl.BlockSpec((rb, half), lambda b, i: (i, 0)),
            full((H, C, C)),
            full((H, C, dk)),
            full((H, C, dk)),
            full((H, 1, dv)),
            full((1, H * dv)),
        ],
        out_specs=pl.BlockSpec((rb, H * dv), lambda b, i: (b * nr + i, 0)),
        out_shape=jax.ShapeDtypeStruct((B * S, H * dv), BF16),
        scratch_shapes=[pltpu.VMEM((H, dk, dv), F32)],
        compiler_params=_params("parallel", "arbitrary"),
        name="retention_core",
    )(proj, proj, proj, proj, cos, sin, decay_intra, qdec_b, kdec_b, cdec_b, gain)


def _mm_res_ln_kernel(a_ref, w_ref, x_ref, g_ref, b_ref, o_ref, *, alpha):
    y = jnp.dot(a_ref[...], w_ref[...], preferred_element_type=F32)
    o_ref[...] = _layer_norm(alpha * x_ref[...] + y, g_ref[...], b_ref[...])


def _mm_res_ln(a, w, x, g, b, alpha, name):
    M, K = a.shape
    D = w.shape[1]
    tm = min(LN_TM, M)
    assert M % tm == 0
    return pl.pallas_call(
        functools.partial(_mm_res_ln_kernel, alpha=alpha),
        grid=(M // tm,),
        in_specs=[pl.BlockSpec((tm, K), lambda i: (i, 0)),
                  pl.BlockSpec((K, D), lambda i: (0, 0)),
                  pl.BlockSpec((tm, D), lambda i: (i, 0)),
                  pl.BlockSpec((1, D), lambda i: (0, 0)),
                  pl.BlockSpec((1, D), lambda i: (0, 0))],
        out_specs=pl.BlockSpec((tm, D), lambda i: (i, 0)),
        out_shape=jax.ShapeDtypeStruct((M, D), F32),
        compiler_params=_params("parallel"),
        name=name,
    )(a, w, x, g.reshape(1, D), b.reshape(1, D))


def _ffn_kernel(x_ref, wg_ref, wu_ref, wd_ref, g_ref, b_ref, o_ref, oc_ref, xb_ref, acc_ref, *, alpha):
    f = pl.program_id(1)

    @pl.when(f == 0)
    def _():
        xb_ref[...] = x_ref[...].astype(BF16)
        acc_ref[...] = jnp.zeros_like(acc_ref)

    xb = xb_ref[...]
    gate = jnp.dot(xb, wg_ref[...], preferred_element_type=F32)
    up = jnp.dot(xb, wu_ref[...], preferred_element_type=F32)
    hid = (gate * jax.nn.sigmoid(gate) * up).astype(BF16)
    acc_ref[...] += jnp.dot(hid, wd_ref[...], preferred_element_type=F32)

    @pl.when(f == pl.num_programs(1) - 1)
    def _():
        y = _layer_norm(alpha * x_ref[...] + acc_ref[...], g_ref[...], b_ref[...])
        o_ref[...] = y
        for c in range(oc_ref.shape[0]):
            oc_ref[c] = y[:, c * LANES:(c + 1) * LANES]


def _ffn_res_ln(x, wg, wu, wd, g, b, alpha):
    M, D = x.shape
    F = wg.shape[1]
    tm, tf = min(FFN_TM, M), FFN_TF
    assert M % tm == 0 and F % tf == 0 and D % LANES == 0
    nc = D // LANES
    return pl.pallas_call(
        functools.partial(_ffn_kernel, alpha=alpha),
        grid=(M // tm, F // tf),
        in_specs=[pl.BlockSpec((tm, D), lambda i, f: (i, 0)),
                  pl.BlockSpec((D, tf), lambda i, f: (0, f)),
                  pl.BlockSpec((D, tf), lambda i, f: (0, f)),
                  pl.BlockSpec((tf, D), lambda i, f: (f, 0)),
                  pl.BlockSpec((1, D), lambda i, f: (0, 0)),
                  pl.BlockSpec((1, D), lambda i, f: (0, 0))],
        out_specs=[pl.BlockSpec((tm, D), lambda i, f: (i, 0)),
                   pl.BlockSpec((nc, tm, LANES), lambda i, f: (0, i, 0))],
        out_shape=[jax.ShapeDtypeStruct((M, D), F32),
                   jax.ShapeDtypeStruct((nc, M, LANES), F32)],
        scratch_shapes=[pltpu.VMEM((tm, D), BF16), pltpu.VMEM((tm, D), F32)],
        compiler_params=_params("parallel", "arbitrary"),
        name="ffn_res_ln",
    )(x, wg, wu, wd, g.reshape(1, D), b.reshape(1, D))


def _mm_residue_kernel(xc_ref, w_ref, o_ref, lhs_ref, *, dil):
    nc, tm, _ = xc_ref.shape
    n = tm // dil
    N = w_ref.shape[1]
    for r in range(dil):
        for c in range(nc):
            lhs_ref[r * n:(r + 1) * n, c * LANES:(c + 1) * LANES] = (
                xc_ref[c, pl.ds(r, n, stride=dil), :].astype(BF16))
    res = jnp.dot(lhs_ref[...], w_ref[...], preferred_element_type=F32).astype(o_ref.dtype)
    for r in range(dil):
        o_ref[:, r * N:(r + 1) * N] = res[r * n:(r + 1) * n, :]


def _matmul_residue_view(xc, w, dil, name):
    nc, M, _ = xc.shape
    K, N = w.shape
    tm = min(MM_TM, M)
    assert M % tm == 0 and tm % (16 * dil) == 0 and nc * LANES == K
    return pl.pallas_call(
        functools.partial(_mm_residue_kernel, dil=dil),
        grid=(M // tm,),
        in_specs=[pl.BlockSpec((nc, tm, LANES), lambda i: (0, i, 0)),
                  pl.BlockSpec((K, N), lambda i: (0, 0))],
        out_specs=pl.BlockSpec((tm // dil, dil * N), lambda i: (i, 0)),
        out_shape=jax.ShapeDtypeStruct((M // dil, dil * N), BF16),
        scratch_shapes=[pltpu.VMEM((tm, K), BF16)],
        compiler_params=_params("parallel"),
        name=name,
    )(xc, w)


def _attn_kernel(q_ref, kp_ref, kc_ref, vp_ref, vc_ref, o_ref):
    Q, dh = ATT_Q, ATT_HEAD_DIM
    n_sub = q_ref.shape[0] // Q
    qi = pl.program_id(2)
    row = lax.broadcasted_iota(jnp.int32, (Q, 2 * Q), 0)
    col = lax.broadcasted_iota(jnp.int32, (Q, 2 * Q), 1)
    band = (col >= row) & (col <= row + Q)
    first = jnp.where(qi > 0, 0, Q)
    lane = lax.broadcasted_iota(jnp.int32, (Q, LANES), 1)
    for j in range(n_sub):
        rs = slice(j * Q, (j + 1) * Q)
        ks = slice((j - 1) * Q, (j + 1) * Q)
        valid = (band & (col >= first)) if j == 0 else band
        lse_slab = jnp.zeros((Q, LANES), F32)
        for h in range(HEADS_PER_GROUP):
            cs = slice(h * dh, (h + 1) * dh)
            if j == 0:
                k = jnp.concatenate([kp_ref[:, cs], kc_ref[:Q, cs]], axis=0)
                v = jnp.concatenate([vp_ref[:, cs], vc_ref[:Q, cs]], axis=0)
            else:
                k, v = kc_ref[ks, cs], vc_ref[ks, cs]
            s = lax.dot_general(q_ref[rs, cs], k, (((1,), (1,)), ((), ())),
                                preferred_element_type=F32) * (dh ** -0.5)
            s = jnp.where(valid, s, NEG_BIG)
            m = jnp.max(s, axis=-1, keepdims=True)
            p = jnp.exp(s - m)
            l = jnp.sum(p, axis=-1, keepdims=True)
            o = jnp.dot(p.astype(BF16), v, preferred_element_type=F32)
            o_ref[rs, cs] = o / l
            lse_slab = jnp.where(lane == h, m + jnp.log(l), lse_slab)
        o_ref[rs, HEADS_PER_GROUP * dh:] = lse_slab


def _dilated_group(qkv_view, B, S, gi, win, dil):
    Q = ATT_Q
    gw = HEADS_PER_GROUP * ATT_HEAD_DIM
    nblk = qkv_view.shape[1] // (dil * gw)
    sd = S // dil
    qb = min(ATT_QB, sd)
    assert win // dil == Q and S % dil == 0 and sd % qb == 0 and qb % Q == 0
    n_sub = qb // Q
    view = qkv_view.reshape(B, sd, qkv_view.shape[1])
    cur = lambda comp: pl.BlockSpec((None, qb, gw), lambda b, r, qi: (b, qi, r * nblk + comp))
    prev = lambda comp: pl.BlockSpec(
        (None, Q, gw), lambda b, r, qi: (b, jnp.maximum(qi * n_sub - 1, 0), r * nblk + comp))
    out = pl.pallas_call(
        _attn_kernel,
        grid=(B, dil, sd // qb),
        in_specs=[cur(0), prev(1), cur(1), prev(2), cur(2)],
        out_specs=pl.BlockSpec((None, qb, ATT_OUT_W), lambda b, r, qi: (b, qi, r)),
        out_shape=jax.ShapeDtypeStruct((B, sd, dil * ATT_OUT_W), F32),
        compiler_params=_params("parallel", "parallel", "arbitrary"),
        name=f"dilated_attn_g{gi}",
    )(view, view, view, view, view)
    return out.reshape(B * sd, dil * ATT_OUT_W)


def _attn_merge_kernel(o0_ref, o1_ref, o2_ref, w_ref, x_ref, g_ref, b_ref, o_ref, *scratch, alpha):
    dh = ATT_HEAD_DIM
    n_slab = ATT_OUT_W // LANES
    tm = x_ref.shape[0]
    groups = []
    scratch = list(scratch)
    for ref, (_, dil) in zip((o0_ref, o1_ref, o2_ref), DIL_PATTERN):
        if dil == 1:
            groups.append(lambda c, ref=ref: ref[:, c * LANES:(c + 1) * LANES])
            continue
        scr = scratch.pop(0)
        n = tm // dil
        for r in range(dil):
            for c in range(n_slab):
                col = r * ATT_OUT_W + c * LANES
                scr[c, pl.ds(r, n, stride=dil), :] = ref[:, col:col + LANES]
        groups.append(lambda c, scr=scr: scr[c])
    lses = [grp(n_slab - 1) for grp in groups]
    mx = jnp.maximum(jnp.maximum(lses[0], lses[1]), lses[2])
    ws = [jnp.exp(l - mx) for l in lses]
    den = ws[0] + ws[1] + ws[2]
    ws = [w / den for w in ws]
    heads = []
    for h in range(HEADS_PER_GROUP):
        acc = ws[0][:, h:h + 1] * groups[0](h)
        acc += ws[1][:, h:h + 1] * groups[1](h)
        acc += ws[2][:, h:h + 1] * groups[2](h)
        heads.append(acc)
    merged = jnp.concatenate(heads, axis=-1).astype(BF16)
    y = jnp.dot(merged, w_ref[...], preferred_element_type=F32)
    o_ref[...] = _layer_norm(alpha * x_ref[...] + y, g_ref[...], b_ref[...])


def _attn_merge_res_ln(outs, w, x, g, b, alpha):
    M, D = x.shape
    K = w.shape[0]
    tm = min(LN_TM, M)
    dils = [dil for _, dil in DIL_PATTERN]
    assert M % tm == 0 and all(tm % (8 * dil) == 0 for dil in dils)
    ospecs = [pl.BlockSpec((tm // dil, dil * ATT_OUT_W), lambda i: (i, 0)) for dil in dils]
    return pl.pallas_call(
        functools.partial(_attn_merge_kernel, alpha=alpha),
        grid=(M // tm,),
        in_specs=ospecs + [
                  pl.BlockSpec((K, D), lambda i: (0, 0)),
                  pl.BlockSpec((tm, D), lambda i: (i, 0)),
                  pl.BlockSpec((1, D), lambda i: (0, 0)),
                  pl.BlockSpec((1, D), lambda i: (0, 0))],
        out_specs=pl.BlockSpec((tm, D), lambda i: (i, 0)),
        out_shape=jax.ShapeDtypeStruct((M, D), F32),
        scratch_shapes=[pltpu.VMEM((ATT_OUT_W // LANES, tm, LANES), F32) for dil in dils if dil > 1],
        compiler_params=_params("parallel"),
        name="attn_merge_res_ln",
    )(*outs, w, x, g.reshape(1, D), b.reshape(1, D))


def _router_kernel(x_ref, w_ref, slab_ref, cnt_ref, run_ref):
    tb = x_ref.shape[0]

    @pl.when(pl.program_id(0) == 0)
    def _():
        run_ref[...] = jnp.zeros_like(run_ref)

    logits = jnp.dot(x_ref[...], w_ref[...], preferred_element_type=F32,
                     precision=lax.Precision.HIGHEST)
    lane = lax.broadcasted_iota(jnp.int32, (tb, LANES), 1)
    logits = jnp.where(lane < N_EXPERTS, logits, NEG_BIG)
    m1 = jnp.max(logits, axis=-1, keepdims=True)
    i1 = jnp.min(jnp.where(logits == m1, lane, LANES), axis=-1, keepdims=True)
    rest = jnp.where(lane == i1, NEG_BIG, logits)
    m2 = jnp.max(rest, axis=-1, keepdims=True)
    i2 = jnp.min(jnp.where(rest == m2, lane, LANES), axis=-1, keepdims=True)
    e = jnp.exp(m2 - m1)
    g1 = 1.0 / (1.0 + e)
    g2 = e / (1.0 + e)
    onehot = jnp.where((lane == i1) | (lane == i2), 1.0, 0.0)
    r = lax.broadcasted_iota(jnp.int32, (tb, tb), 0)
    c = lax.broadcasted_iota(jnp.int32, (tb, tb), 1)
    lower = jnp.where(c < r, 1.0, 0.0).astype(BF16)
    before = jnp.dot(lower, onehot.astype(BF16), preferred_element_type=F32) + run_ref[...]
    rank1 = jnp.sum(jnp.where(lane == i1, before, 0.0), axis=-1, keepdims=True)
    rank2 = jnp.sum(jnp.where(lane == i2, before, 0.0), axis=-1, keepdims=True)
    total = run_ref[...] + jnp.sum(onehot, axis=0, keepdims=True)
    run_ref[...] = total
    cnt_ref[...] = total
    slab = jnp.where(lane == 0, i1.astype(F32), 0.0)
    slab = jnp.where(lane == 1, i2.astype(F32), slab)
    slab = jnp.where(lane == 2, g1, slab)
    slab = jnp.where(lane == 3, g2, slab)
    slab = jnp.where(lane == 4, rank1, slab)
    slab = jnp.where(lane == 5, rank2, slab)
    slab_ref[...] = slab


def _router(x, w_router):
    T, D = x.shape
    tb = min(ROUTE_TB, T)
    assert T % tb == 0
    w = jnp.zeros((D, LANES), F32).at[:, :N_EXPERTS].set(w_router)
    return pl.pallas_call(
        _router_kernel,
        grid=(T // tb,),
        in_specs=[pl.BlockSpec((tb, D), lambda i: (i, 0)),
                  pl.BlockSpec((D, LANES), lambda i: (0, 0))],
        out_specs=[pl.BlockSpec((tb, LANES), lambda i: (i, 0)),
                   pl.BlockSpec((1, LANES), lambda i: (0, 0))],
        out_shape=[jax.ShapeDtypeStruct((T, LANES), F32),
                   jax.ShapeDtypeStruct((1, LANES), F32)],
        scratch_shapes=[pltpu.VMEM((1, LANES), F32)],
        compiler_params=_params("arbitrary"),
        name="moe_router",
    )(x, w)


def _expert_kernel(be_ref, na_ref, tok_ref, tok_nxt_ref, dst_prv_ref, dst_ref, x_hbm,
                   wg_ref, wu_ref, wd_ref, y_hbm, xbuf_ref, xb_ref, acc_ref, ybuf_ref,
                   sem_in, sem_out, *, n_f, subs):
    del be_ref
    i, f = pl.program_id(0), pl.program_id(1)
    n_active = na_ref[0]
    tm = xb_ref.shape[0]
    per = tm // (n_f * len(subs))
    slot = lax.rem(i, 2)

    def gather(tok, t, s):
        return pltpu.make_async_copy(x_hbm.at[pl.ds(tok, 1)], xbuf_ref.at[s, pl.ds(t, 1)], sem_in.at[s])

    def scatter(dst, t, s):
        return pltpu.make_async_copy(ybuf_ref.at[s, pl.ds(t, 1)], y_hbm.at[pl.ds(dst, 1)], sem_out.at[s])

    def for_rows(fn):
        def body(t, carry):
            fn(t)
            return carry
        lax.fori_loop(0, tm, body, 0, unroll=DMA_UNROLL)

    @pl.when(i < n_active)
    def _():
        @pl.when((i == 0) & (f == 0))
        def _():
            for_rows(lambda t: gather(tok_ref[0, 0, t], t, 0).start())
            ybuf_ref[1] = jnp.zeros(ybuf_ref.shape[1:], F32)

        @pl.when(f == 0)
        def _():
            for_rows(lambda t: gather(0, t, slot).wait())
            xb_ref[...] = xbuf_ref[slot].astype(BF16)
            acc_ref[...] = jnp.zeros_like(acc_ref)

        xb = xb_ref[...]
        off = 0
        for c, width in enumerate(subs):
            base = (f * len(subs) + c) * per
            for t in range(per):
                gather(tok_nxt_ref[0, 0, base + t], base + t, 1 - slot).start()
                scatter(dst_prv_ref[0, 0, base + t], base + t, 1 - slot).start()
            cs = slice(off, off + width)
            gate = jnp.dot(xb, wg_ref[:, cs], preferred_element_type=F32)
            up = jnp.dot(xb, wu_ref[:, cs], preferred_element_type=F32)
            hid = (gate * jax.nn.sigmoid(gate) * up).astype(BF16)
            acc_ref[...] += jnp.dot(hid, wd_ref[cs, :], preferred_element_type=F32)
            off += width

        @pl.when(f == n_f - 1)
        def _():
            @pl.when(i > 0)
            def _():
                for_rows(lambda t: scatter(0, t, slot).wait())
            ybuf_ref[slot] = acc_ref[...]

        @pl.when((i == n_active - 1) & (f == n_f - 1))
        def _():
            for_rows(lambda t: scatter(dst_ref[0, 0, t], t, slot).start())
            for_rows(lambda t: gather(0, t, 1 - slot).wait())
            for_rows(lambda t: scatter(0, t, 1 - slot).wait())
            for_rows(lambda t: scatter(0, t, slot).wait())

    @pl.when((i >= n_active) & (f == n_f - 1))
    def _():
        ybuf_ref[0] = jnp.zeros(ybuf_ref.shape[1:], F32)
        for_rows(lambda t: scatter(dst_ref[0, 0, t], t, 0).start())
        for_rows(lambda t: scatter(0, t, 0).wait())


def _experts(x, slot_tok, slot_dst, n_rows_out, blk_exp, n_active, wg, wu, wd):
    D = x.shape[1]
    P = slot_tok.shape[0]
    F = wg.shape[2]
    tm, tf = MOE_BLOCK, EXP_TF
    subs = (EXP_SUB,) * (tf // EXP_SUB) + ((tf % EXP_SUB,) if tf % EXP_SUB else ())
    assert P % tm == 0 and F % tf == 0 and tm % ((F // tf) * len(subs)) == 0
    nb, nf = P // tm, F // tf
    toks = slot_tok.reshape(nb, 1, tm)
    spare = n_rows_out - tm + jnp.arange(tm, dtype=jnp.int32)
    dsts = jnp.concatenate([spare, slot_dst]).reshape(nb + 1, 1, tm)
    smem = lambda fn: pl.BlockSpec((1, 1, tm), fn, memory_space=pltpu.SMEM)
    grid_spec = pltpu.PrefetchScalarGridSpec(
        num_scalar_prefetch=2,
        grid=(nb, nf),
        in_specs=[smem(lambda i, f, be, na: (i, 0, 0)),
                  smem(lambda i, f, be, na: (jnp.minimum(i + 1, nb - 1), 0, 0)),
                  smem(lambda i, f, be, na: (i, 0, 0)),
                  smem(lambda i, f, be, na: (i + 1, 0, 0)),
                  pl.BlockSpec(memory_space=pl.ANY),
                  pl.BlockSpec((None, D, tf), lambda i, f, be, na: (be[i], 0, f)),
                  pl.BlockSpec((None, D, tf), lambda i, f, be, na: (be[i], 0, f)),
                  pl.BlockSpec((None, tf, D), lambda i, f, be, na: (be[i], f, 0))],
        out_specs=pl.BlockSpec(memory_space=pl.ANY),
        scratch_shapes=[pltpu.VMEM((2, tm, D), F32), pltpu.VMEM((tm, D), BF16), pltpu.VMEM((tm, D), F32),
                        pltpu.VMEM((2, tm, D), F32),
                        pltpu.SemaphoreType.DMA((2,)), pltpu.SemaphoreType.DMA((2,))],
    )
    return pl.pallas_call(
        functools.partial(_expert_kernel, n_f=nf, subs=subs),
        grid_spec=grid_spec,
        out_shape=jax.ShapeDtypeStruct((n_rows_out, D), F32),
        compiler_params=_params("arbitrary", "arbitrary"),
        name="moe_experts",
    )(blk_exp, n_active, toks, toks, dsts, dsts, x, wg, wu, wd)


def _combine_kernel(y1_ref, y2_ref, slab_ref, x_ref, g_ref, b_ref, o_ref, *, alpha):
    slab = slab_ref[...]
    mix = slab[:, 2:3] * y1_ref[...] + slab[:, 3:4] * y2_ref[...]
    o_ref[...] = _layer_norm(alpha * x_ref[...] + mix, g_ref[...], b_ref[...])


def _combine_res_ln(y, slab, x, g, b, alpha):
    T, D = x.shape
    tb = min(COMB_TB, T)
    assert T % tb == 0
    nb = T // tb
    return pl.pallas_call(
        functools.partial(_combine_kernel, alpha=alpha),
        grid=(nb,),
        in_specs=[pl.BlockSpec((tb, D), lambda i: (i, 0)),
                  pl.BlockSpec((tb, D), lambda i: (i + nb, 0)),
                  pl.BlockSpec((tb, LANES), lambda i: (i, 0)),
                  pl.BlockSpec((tb, D), lambda i: (i, 0)),
                  pl.BlockSpec((1, D), lambda i: (0, 0)),
                  pl.BlockSpec((1, D), lambda i: (0, 0))],
        out_specs=pl.BlockSpec((tb, D), lambda i: (i, 0)),
        out_shape=jax.ShapeDtypeStruct((T, D), F32),
        compiler_params=_params("parallel"),
        name="moe_combine_res_ln",
    )(y, y, slab, x, g.reshape(1, D), b.reshape(1, D))


def _moe_res_ln(x, w_router, wg, wu, wd, g, b, alpha):
    T, D = x.shape
    slab, counts = _router(x, w_router)
    counts = counts[0, :N_EXPERTS].astype(jnp.int32)
    padded = ((counts + MOE_BLOCK - 1) // MOE_BLOCK) * MOE_BLOCK
    pad_end = jnp.cumsum(padded)
    pad_start = pad_end - padded
    e1 = slab[:, 0].astype(jnp.int32)
    e2 = slab[:, 1].astype(jnp.int32)
    d1 = pad_start[e1] + slab[:, 4].astype(jnp.int32)
    d2 = pad_start[e2] + slab[:, 5].astype(jnp.int32)
    nblk = -(-(2 * T) // MOE_BLOCK) + N_EXPERTS
    n_slots = nblk * MOE_BLOCK
    blk_start = jnp.arange(nblk, dtype=jnp.int32) * MOE_BLOCK
    blk_exp = jnp.minimum(jnp.sum(pad_end[None, :] <= blk_start[:, None], axis=1), N_EXPERTS - 1)
    n_active = (pad_end[-1:] // MOE_BLOCK).astype(jnp.int32)
    slot = jnp.arange(n_slots, dtype=jnp.int32)
    slot_exp = jnp.sum(pad_end[None, :] <= slot[:, None], axis=1)
    first_tok = jnp.concatenate([jnp.cumsum(counts) - counts, jnp.array([2 * T], jnp.int32)])
    first_slot = jnp.concatenate([pad_start, pad_end[-1:]])
    n_tok = jnp.concatenate([counts, jnp.array([0], jnp.int32)])
    filled_before = first_tok[slot_exp] + jnp.minimum(slot - first_slot[slot_exp], n_tok[slot_exp])
    tok = jnp.arange(T, dtype=jnp.int32)
    slot_dst = (2 * T + slot - filled_before).at[jnp.concatenate([d1, d2])].set(
        jnp.concatenate([tok, T + tok]), unique_indices=True)
    slot_tok = jnp.where(slot_dst < 2 * T, slot_dst % T, 0)
    n_rows_out = n_slots + MOE_BLOCK
    y = _experts(x, slot_tok, slot_dst, n_rows_out, blk_exp.astype(jnp.int32), n_active, wg, wu, wd)
    return _combine_res_ln(y, slab, x, g, b, alpha)


def kernel(x, ln_gain, ln_bias, ret_w_in, ret_gn_gain, ret_w_out, att_w_qkv, att_w_out,
           ffn_w_gate, ffn_w_up, ffn_w_down, moe_w_router, moe_w_gate, moe_w_up, moe_w_down):
    B, S, D = x.shape
    depth = ln_gain.shape[0]
    alpha = (2 * depth) ** 0.25
    h = x.reshape(B * S, D)
    h_blocks = None
    for i in range(depth):
        j = i // 2
        if i % 2 == 0:
            proj = _matmul(h, ret_w_in[j].astype(BF16), "ret_in_proj")
            gated = _retention_core(proj, B, S, ret_gn_gain[j])
            h = _mm_res_ln(gated, ret_w_out[j].astype(BF16), h, ln_gain[i, 0], ln_bias[i, 0], alpha,
                           "ret_out_res_ln")
            h, h_blocks = _ffn_res_ln(h, ffn_w_gate[j].astype(BF16), ffn_w_up[j].astype(BF16),
                                      ffn_w_down[j].astype(BF16), ln_gain[i, 1], ln_bias[i, 1], alpha)
        else:
            gw = HEADS_PER_GROUP * ATT_HEAD_DIM
            outs = []
            for gi, (win, dil) in enumerate(DIL_PATTERN):
                w_g = jnp.concatenate(
                    [att_w_qkv[j][:, (c * N_GROUPS + gi) * gw:(c * N_GROUPS + gi + 1) * gw] for c in range(3)],
                    axis=1).astype(BF16)
                if dil == 1:
                    qkv = _matmul(h, w_g, f"att_qkv_proj_g{gi}")
                else:
                    if h_blocks is None:
                        h_blocks = jnp.transpose(h.reshape(B * S, D // LANES, LANES), (1, 0, 2))
                    qkv = _matmul_residue_view(h_blocks, w_g, dil, f"att_qkv_proj_g{gi}")
                outs.append(_dilated_group(qkv, B, S, gi, win, dil))
            h_blocks = None
            h = _attn_merge_res_ln(outs, att_w_out[j].astype(BF16), h, ln_gain[i, 0], ln_bias[i, 0], alpha)
            h = _moe_res_ln(h, moe_w_router[j], moe_w_gate[j].astype(BF16), moe_w_up[j].astype(BF16),
                            moe_w_down[j].astype(BF16), ln_gain[i, 1], ln_bias[i, 1], alpha)
    return h.reshape(B, S, D)
```

```python
import functools

import jax
import jax.numpy as jnp
from jax import lax
from jax.experimental import pallas as pl
from jax.experimental.pallas import tpu as pltpu

F32 = jnp.float32
BF16 = jnp.bfloat16

RET_HEADS = 4
RET_CHUNK = 128
ROPE_BASE = 10000.0
DIL_PATTERN = ((128, 1), (512, 4), (2048, 16))
N_GROUPS = len(DIL_PATTERN)
HEADS_PER_GROUP = 4
ATT_HEAD_DIM = 128
N_EXPERTS = 8
LN_EPS = 1e-5

LANES = 128
SUBLANES = 8
VMEM_LIMIT = 56 * 1024 * 1024

MM_TM = 1024
MM_TN = 2048
LN_TM = 512
FFN_TM = 1024
FFN_TF = 512
EXP_TF = 512
EXP_SUB = 512
RET_ROWS = 512
ATT_Q = 128
ATT_QB = 512
ATT_OUT_W = HEADS_PER_GROUP * ATT_HEAD_DIM + LANES
MOE_BLOCK = 896
ROUTE_TB = 512
COMB_TB = 512
DMA_UNROLL = 8
NEG_BIG = -1e30


def _params(*sem):
    return pltpu.CompilerParams(dimension_semantics=sem, vmem_limit_bytes=VMEM_LIMIT)


def _layer_norm(y, g, b):
    mu = jnp.mean(y, axis=-1, keepdims=True)
    d = y - mu
    var = jnp.mean(d * d, axis=-1, keepdims=True)
    return d * lax.rsqrt(var + LN_EPS) * g + b


def _store_row_tiles(ref, y):
    m = y.shape[0]
    for c in range(SUBLANES):
        ref[pl.ds(c, m, stride=SUBLANES), :] = y[:, c * LANES:(c + 1) * LANES]


def _load_row_tiles(ref, m):
    return jnp.concatenate([ref[pl.ds(c, m, stride=SUBLANES), :] for c in range(SUBLANES)], axis=-1)


def _mm_kernel(x_ref, w_ref, o_ref, xb_ref):
    @pl.when(pl.program_id(1) == 0)
    def _():
        xb_ref[...] = x_ref[...].astype(BF16)

    o_ref[...] = jnp.dot(xb_ref[...], w_ref[...], preferred_element_type=F32).astype(o_ref.dtype)


def _matmul(x, w, name):
    M, K = x.shape
    N = w.shape[1]
    tm = min(MM_TM, M)
    tn = max(t for t in range(LANES, MM_TN + 1, LANES) if N % t == 0)
    assert M % tm == 0
    return pl.pallas_call(
        _mm_kernel,
        grid=(M // tm, N // tn),
        in_specs=[pl.BlockSpec((tm, K), lambda i, j: (i, 0)),
                  pl.BlockSpec((K, tn), lambda i, j: (0, j))],
        out_specs=pl.BlockSpec((tm, tn), lambda i, j: (i, j)),
        out_shape=jax.ShapeDtypeStruct((M, N), BF16),
        scratch_shapes=[pltpu.VMEM((tm, K), BF16)],
        compiler_params=_params("parallel", "arbitrary"),
        name=name,
    )(x, w)


def _ret_kernel(q_ref, k_ref, v_ref, g_ref, cos_ref, sin_ref, dec_ref, qdec_ref, kdec_ref,
                cdec_ref, gain_ref, o_ref, state_ref, *, n_chunks, dk):
    C, H = RET_CHUNK, RET_HEADS
    half = dk // 2
    dv = 2 * dk

    @pl.when(pl.program_id(1) == 0)
    def _():
        state_ref[...] = jnp.zeros_like(state_ref)

    def rot(t, cos, sin):
        t1, t2 = t[:, :half], t[:, half:]
        return jnp.concatenate([t1 * cos - t2 * sin, t2 * cos + t1 * sin], axis=-1)

    def chunk(c, carry):
        rows = pl.ds(pl.multiple_of(c * C, C), C)
        cos, sin = cos_ref[rows, :], sin_ref[rows, :]
        for h in range(H):
            kc = slice(h * dk, (h + 1) * dk)
            vc = slice(h * dv, (h + 1) * dv)
            q = rot(q_ref[rows, kc].astype(F32), cos, sin)
            k = rot(k_ref[rows, kc].astype(F32), cos, sin) * (dk ** -0.5)
            v = v_ref[rows, vc]
            scores = lax.dot_general(q.astype(BF16), k.astype(BF16), (((1,), (1,)), ((), ())),
                                     preferred_element_type=F32) * dec_ref[h]
            intra = jnp.dot(scores.astype(BF16), v, preferred_element_type=F32)
            state = state_ref[h]
            cross = jnp.dot((q * qdec_ref[h]).astype(BF16), state.astype(BF16),
                            preferred_element_type=F32)
            kn_t = (k * kdec_ref[h]).T.astype(BF16)
            state_ref[h] = cdec_ref[h] * state + jnp.dot(kn_t, v, preferred_element_type=F32)
            r = intra + cross
            mu = jnp.mean(r, axis=-1, keepdims=True)
            d = r - mu
            var = jnp.mean(d * d, axis=-1, keepdims=True)
            normed = d * lax.rsqrt(var + LN_EPS) * gain_ref[:, vc]
            gate = g_ref[rows, vc].astype(F32)
            o_ref[rows, vc] = (gate * jax.nn.sigmoid(gate) * normed).astype(o_ref.dtype)
        return carry

    lax.fori_loop(0, n_chunks, chunk, 0)


def _retention_core(proj, B, S, gn_gain):
    H, C = RET_HEADS, RET_CHUNK
    cols = proj.shape[1]
    dk = cols // (6 * H)
    dv = 2 * dk
    rb = min(RET_ROWS, S)
    assert S % rb == 0 and rb % C == 0
    nr = S // rb

    half = dk // 2
    inv = ROPE_BASE ** (-jnp.arange(half, dtype=F32) / half)
    ang = jnp.arange(S).astype(F32)[:, None] * inv[None, :]
    cos, sin = jnp.cos(ang), jnp.sin(ang)
    log_gamma = jnp.log(1.0 - 2.0 ** (-5.0 - jnp.arange(H, dtype=F32)))
    idx = jnp.arange(C, dtype=F32)
    rel = idx[:, None] - idx[None, :]
    decay_intra = jnp.where(rel >= 0, jnp.exp(log_gamma[:, None, None] * jnp.maximum(rel, 0.0)), 0.0)
    q_dec = jnp.exp(log_gamma[None, :] * (idx[:, None] + 1.0))
    k_dec = jnp.exp(log_gamma[None, :] * (C - 1.0 - idx[:, None]))
    chunk_dec = jnp.exp(log_gamma * C)
    qdec_b = jnp.broadcast_to(q_dec.T[:, :, None], (H, C, dk))
    kdec_b = jnp.broadcast_to(k_dec.T[:, :, None], (H, C, dk))
    cdec_b = jnp.broadcast_to(chunk_dec[:, None, None], (H, 1, dv))
    gain = gn_gain.reshape(1, H * dv)

    full = lambda shape: pl.BlockSpec(shape, lambda b, i: (0,) * len(shape))
    kern = functools.partial(_ret_kernel, n_chunks=rb // C, dk=dk)
    return pl.pallas_call(
        kern,
        grid=(B, nr),
        in_specs=[
            pl.BlockSpec((rb, H * dk), lambda b, i: (b * nr + i, 0)),
            pl.BlockSpec((rb, H * dk), lambda b, i: (b * nr + i, 1)),
            pl.BlockSpec((rb, H * dv), lambda b, i: (b * nr + i, 1)),
            pl.BlockSpec((rb, H * dv), lambda b, i: (b * nr + i, 2)),
            pl.BlockSpec((rb, half), lambda b, i: (i, 0)),
            pl.BlockSpec((rb, half), lambda b, i: (i, 0)),
            full((H, C, C)),
            full((H, C, dk)),
            full((H, C, dk)),
            full((H, 1, dv)),
            full((1, H * dv)),
        ],
        out_specs=pl.BlockSpec((rb, H * dv), lambda b, i: (b * nr + i, 0)),
        out_shape=jax.ShapeDtypeStruct((B * S, H * dv), BF16),
        scratch_shapes=[pltpu.VMEM((H, dk, dv), F32)],
        compiler_params=_params("parallel", "arbitrary"),
        name="retention_core",
    )(proj, proj, proj, proj, cos, sin, decay_intra, qdec_b, kdec_b, cdec_b, gain)


def _mm_res_ln_kernel(a_ref, w_ref, x_ref, g_ref, b_ref, o_ref, *, alpha):
    y = jnp.dot(a_ref[...], w_ref[...], preferred_element_type=F32)
    o_ref[...] = _layer_norm(alpha * x_ref[...] + y, g_ref[...], b_ref[...])


def _mm_res_ln(a, w, x, g, b, alpha, name):
    M, K = a.shape
    D = w.shape[1]
    tm = min(LN_TM, M)
    assert M % tm == 0
    return pl.pallas_call(
        functools.partial(_mm_res_ln_kernel, alpha=alpha),
        grid=(M // tm,),
        in_specs=[pl.BlockSpec((tm, K), lambda i: (i, 0)),
                  pl.BlockSpec((K, D), lambda i: (0, 0)),
                  pl.BlockSpec((tm, D), lambda i: (i, 0)),
                  pl.BlockSpec((1, D), lambda i: (0, 0)),
                  pl.BlockSpec((1, D), lambda i: (0, 0))],
        out_specs=pl.BlockSpec((tm, D), lambda i: (i, 0)),
        out_shape=jax.ShapeDtypeStruct((M, D), F32),
        compiler_params=_params("parallel"),
        name=name,
    )(a, w, x, g.reshape(1, D), b.reshape(1, D))


def _ffn_kernel(x_ref, wg_ref, wu_ref, wd_ref, g_ref, b_ref, o_ref, oc_ref, xb_ref, acc_ref, *, alpha):
    f = pl.program_id(1)

    @pl.when(f == 0)
    def _():
        xb_ref[...] = x_ref[...].astype(BF16)
        acc_ref[...] = jnp.zeros_like(acc_ref)

    xb = xb_ref[...]
    gate = jnp.dot(xb, wg_ref[...], preferred_element_type=F32)
    up = jnp.dot(xb, wu_ref[...], preferred_element_type=F32)
    hid = (gate * jax.nn.sigmoid(gate) * up).astype(BF16)
    acc_ref[...] += jnp.dot(hid, wd_ref[...], preferred_element_type=F32)

    @pl.when(f == pl.num_programs(1) - 1)
    def _():
        y = _layer_norm(alpha * x_ref[...] + acc_ref[...], g_ref[...], b_ref[...])
        o_ref[...] = y
        for c in range(oc_ref.shape[0]):
            oc_ref[c] = y[:, c * LANES:(c + 1) * LANES]


def _ffn_res_ln(x, wg, wu, wd, g, b, alpha):
    M, D = x.shape
    F = wg.shape[1]
    tm, tf = min(FFN_TM, M), FFN_TF
    assert M % tm == 0 and F % tf == 0 and D % LANES == 0
    nc = D // LANES
    return pl.pallas_call(
        functools.partial(_ffn_kernel, alpha=alpha),
        grid=(M // tm, F // tf),
        in_specs=[pl.BlockSpec((tm, D), lambda i, f: (i, 0)),
                  pl.BlockSpec((D, tf), lambda i, f: (0, f)),
                  pl.BlockSpec((D, tf), lambda i, f: (0, f)),
                  pl.BlockSpec((tf, D), lambda i, f: (f, 0)),
                  pl.BlockSpec((1, D), lambda i, f: (0, 0)),
                  pl.BlockSpec((1, D), lambda i, f: (0, 0))],
        out_specs=[pl.BlockSpec((tm, D), lambda i, f: (i, 0)),
                   pl.BlockSpec((nc, tm, LANES), lambda i, f: (0, i, 0))],
        out_shape=[jax.ShapeDtypeStruct((M, D), F32),
                   jax.ShapeDtypeStruct((nc, M, LANES), F32)],
        scratch_shapes=[pltpu.VMEM((tm, D), BF16), pltpu.VMEM((tm, D), F32)],
        compiler_params=_params("parallel", "arbitrary"),
        name="ffn_res_ln",
    )(x, wg, wu, wd, g.reshape(1, D), b.reshape(1, D))


def _mm_residue_kernel(xc_ref, w_ref, o_ref, lhs_ref, *, dil):
    nc, tm, _ = xc_ref.shape
    n = tm // dil
    N = w_ref.shape[1]
    for r in range(dil):
        for c in range(nc):
            lhs_ref[r * n:(r + 1) * n, c * LANES:(c + 1) * LANES] = (
                xc_ref[c, pl.ds(r, n, stride=dil), :].astype(BF16))
    res = jnp.dot(lhs_ref[...], w_ref[...], preferred_element_type=F32).astype(o_ref.dtype)
    for r in range(dil):
        o_ref[:, r * N:(r + 1) * N] = res[r * n:(r + 1) * n, :]


def _matmul_residue_view(xc, w, dil, name):
    nc, M, _ = xc.shape
    K, N = w.shape
    tm = min(MM_TM, M)
    assert M % tm == 0 and tm % (16 * dil) == 0 and nc * LANES == K
    return pl.pallas_call(
        functools.partial(_mm_residue_kernel, dil=dil),
        grid=(M // tm,),
        in_specs=[pl.BlockSpec((nc, tm, LANES), lambda i: (0, i, 0)),
                  pl.BlockSpec((K, N), lambda i: (0, 0))],
        out_specs=pl.BlockSpec((tm // dil, dil * N), lambda i: (i, 0)),
        out_shape=jax.ShapeDtypeStruct((M // dil, dil * N), BF16),
        scratch_shapes=[pltpu.VMEM((tm, K), BF16)],
        compiler_params=_params("parallel"),
        name=name,
    )(xc, w)


def _attn_kernel(q_ref, kp_ref, kc_ref, vp_ref, vc_ref, o_ref):
    Q, dh = ATT_Q, ATT_HEAD_DIM
    n_sub = q_ref.shape[0] // Q
    qi = pl.program_id(2)
    row = lax.broadcasted_iota(jnp.int32, (Q, 2 * Q), 0)
    col = lax.broadcasted_iota(jnp.int32, (Q, 2 * Q), 1)
    band = (col >= row) & (col <= row + Q)
    first = jnp.where(qi > 0, 0, Q)
    lane = lax.broadcasted_iota(jnp.int32, (Q, LANES), 1)
    for j in range(n_sub):
        rs = slice(j * Q, (j + 1) * Q)
        ks = slice((j - 1) * Q, (j + 1) * Q)
        valid = (band & (col >= first)) if j == 0 else band
        lse_slab = jnp.zeros((Q, LANES), F32)
        for h in range(HEADS_PER_GROUP):
            cs = slice(h * dh, (h + 1) * dh)
            if j == 0:
                k = jnp.concatenate([kp_ref[:, cs], kc_ref[:Q, cs]], axis=0)
                v = jnp.concatenate([vp_ref[:, cs], vc_ref[:Q, cs]], axis=0)
            else:
                k, v = kc_ref[ks, cs], vc_ref[ks, cs]
            s = lax.dot_general(q_ref[rs, cs], k, (((1,), (1,)), ((), ())),
                                preferred_element_type=F32) * (dh ** -0.5)
            s = jnp.where(valid, s, NEG_BIG)
            m = jnp.max(s, axis=-1, keepdims=True)
            p = jnp.exp(s - m)
            l = jnp.sum(p, axis=-1, keepdims=True)
            o = jnp.dot(p.astype(BF16), v, preferred_element_type=F32)
            o_ref[rs, cs] = o / l
            lse_slab = jnp.where(lane == h, m + jnp.log(l), lse_slab)
        o_ref[rs, HEADS_PER_GROUP * dh:] = lse_slab


def _dilated_group(qkv_view, B, S, gi, win, dil):
    Q = ATT_Q
    gw = HEADS_PER_GROUP * ATT_HEAD_DIM
    nblk = qkv_view.shape[1] // (dil * gw)
    sd = S // dil
    qb = min(ATT_QB, sd)
    assert win // dil == Q and S % dil == 0 and sd % qb == 0 and qb % Q == 0
    n_sub = qb // Q
    view = qkv_view.reshape(B, sd, qkv_view.shape[1])
    cur = lambda comp: pl.BlockSpec((None, qb, gw), lambda b, r, qi: (b, qi, r * nblk + comp))
    prev = lambda comp: pl.BlockSpec(
        (None, Q, gw), lambda b, r, qi: (b, jnp.maximum(qi * n_sub - 1, 0), r * nblk + comp))
    out = pl.pallas_call(
        _attn_kernel,
        grid=(B, dil, sd // qb),
        in_specs=[cur(0), prev(1), cur(1), prev(2), cur(2)],
        out_specs=pl.BlockSpec((None, qb, ATT_OUT_W), lambda b, r, qi: (b, qi, r)),
        out_shape=jax.ShapeDtypeStruct((B, sd, dil * ATT_OUT_W), F32),
        compiler_params=_params("parallel", "parallel", "arbitrary"),
        name=f"dilated_attn_g{gi}",
    )(view, view, view, view, view)
    return out.reshape(B * sd, dil * ATT_OUT_W)


def _attn_merge_kernel(o0_ref, o1_ref, o2_ref, w_ref, x_ref, g_ref, b_ref, o_ref, ot_ref, *scratch, alpha):
    dh = ATT_HEAD_DIM
    n_slab = ATT_OUT_W // LANES
    tm = x_ref.shape[0]
    groups = []
    scratch = list(scratch)
    for ref, (_, dil) in zip((o0_ref, o1_ref, o2_ref), DIL_PATTERN):
        if dil == 1:
            groups.append(lambda c, ref=ref: ref[:, c * LANES:(c + 1) * LANES])
            continue
        scr = scratch.pop(0)
        n = tm // dil
        for r in range(dil):
            for c in range(n_slab):
                col = r * ATT_OUT_W + c * LANES
                scr[c, pl.ds(r, n, stride=dil), :] = ref[:, col:col + LANES]
        groups.append(lambda c, scr=scr: scr[c])
    lses = [grp(n_slab - 1) for grp in groups]
    mx = jnp.maximum(jnp.maximum(lses[0], lses[1]), lses[2])
    ws = [jnp.exp(l - mx) for l in lses]
    den = ws[0] + ws[1] + ws[2]
    ws = [w / den for w in ws]
    heads = []
    for h in range(HEADS_PER_GROUP):
        acc = ws[0][:, h:h + 1] * groups[0](h)
        acc += ws[1][:, h:h + 1] * groups[1](h)
        acc += ws[2][:, h:h + 1] * groups[2](h)
        heads.append(acc)
    merged = jnp.concatenate(heads, axis=-1).astype(BF16)
    y = jnp.dot(merged, w_ref[...], preferred_element_type=F32)
    out = _layer_norm(alpha * x_ref[...] + y, g_ref[...], b_ref[...])
    o_ref[...] = out
    _store_row_tiles(ot_ref, out)


def _attn_merge_res_ln(outs, w, x, g, b, alpha):
    M, D = x.shape
    K = w.shape[0]
    tm = min(LN_TM, M)
    dils = [dil for _, dil in DIL_PATTERN]
    assert M % tm == 0 and all(tm % (8 * dil) == 0 for dil in dils) and D == SUBLANES * LANES
    ospecs = [pl.BlockSpec((tm // dil, dil * ATT_OUT_W), lambda i: (i, 0)) for dil in dils]
    return pl.pallas_call(
        functools.partial(_attn_merge_kernel, alpha=alpha),
        grid=(M // tm,),
        in_specs=ospecs + [
                  pl.BlockSpec((K, D), lambda i: (0, 0)),
                  pl.BlockSpec((tm, D), lambda i: (i, 0)),
                  pl.BlockSpec((1, D), lambda i: (0, 0)),
                  pl.BlockSpec((1, D), lambda i: (0, 0))],
        out_specs=[pl.BlockSpec((tm, D), lambda i: (i, 0)),
                   pl.BlockSpec((tm * SUBLANES, LANES), lambda i: (i, 0))],
        out_shape=[jax.ShapeDtypeStruct((M, D), F32),
                   jax.ShapeDtypeStruct((M * SUBLANES, LANES), F32)],
        scratch_shapes=[pltpu.VMEM((ATT_OUT_W // LANES, tm, LANES), F32) for dil in dils if dil > 1],
        compiler_params=_params("parallel"),
        name="attn_merge_res_ln",
    )(*outs, w, x, g.reshape(1, D), b.reshape(1, D))


def _router_kernel(x_ref, w_ref, slab_ref, cnt_ref, run_ref):
    tb = x_ref.shape[0]

    @pl.when(pl.program_id(0) == 0)
    def _():
        run_ref[...] = jnp.zeros_like(run_ref)

    logits = jnp.dot(x_ref[...], w_ref[...], preferred_element_type=F32,
                     precision=lax.Precision.HIGHEST)
    lane = lax.broadcasted_iota(jnp.int32, (tb, LANES), 1)
    logits = jnp.where(lane < N_EXPERTS, logits, NEG_BIG)
    m1 = jnp.max(logits, axis=-1, keepdims=True)
    i1 = jnp.min(jnp.where(logits == m1, lane, LANES), axis=-1, keepdims=True)
    rest = jnp.where(lane == i1, NEG_BIG, logits)
    m2 = jnp.max(rest, axis=-1, keepdims=True)
    i2 = jnp.min(jnp.where(rest == m2, lane, LANES), axis=-1, keepdims=True)
    e = jnp.exp(m2 - m1)
    g1 = 1.0 / (1.0 + e)
    g2 = e / (1.0 + e)
    onehot = jnp.where((lane == i1) | (lane == i2), 1.0, 0.0)
    r = lax.broadcasted_iota(jnp.int32, (tb, tb), 0)
    c = lax.broadcasted_iota(jnp.int32, (tb, tb), 1)
    lower = jnp.where(c < r, 1.0, 0.0).astype(BF16)
    before = jnp.dot(lower, onehot.astype(BF16), preferred_element_type=F32) + run_ref[...]
    rank1 = jnp.sum(jnp.where(lane == i1, before, 0.0), axis=-1, keepdims=True)
    rank2 = jnp.sum(jnp.where(lane == i2, before, 0.0), axis=-1, keepdims=True)
    total = run_ref[...] + jnp.sum(onehot, axis=0, keepdims=True)
    run_ref[...] = total
    cnt_ref[...] = total
    slab = jnp.where(lane == 0, i1.astype(F32), 0.0)
    slab = jnp.where(lane == 1, i2.astype(F32), slab)
    slab = jnp.where(lane == 2, g1, slab)
    slab = jnp.where(lane == 3, g2, slab)
    slab = jnp.where(lane == 4, rank1, slab)
    slab = jnp.where(lane == 5, rank2, slab)
    slab_ref[...] = slab


def _router(x, w_router):
    T, D = x.shape
    tb = min(ROUTE_TB, T)
    assert T % tb == 0
    w = jnp.zeros((D, LANES), F32).at[:, :N_EXPERTS].set(w_router)
    return pl.pallas_call(
        _router_kernel,
        grid=(T // tb,),
        in_specs=[pl.BlockSpec((tb, D), lambda i: (i, 0)),
                  pl.BlockSpec((D, LANES), lambda i: (0, 0))],
        out_specs=[pl.BlockSpec((tb, LANES), lambda i: (i, 0)),
                   pl.BlockSpec((1, LANES), lambda i: (0, 0))],
        out_shape=[jax.ShapeDtypeStruct((T, LANES), F32),
                   jax.ShapeDtypeStruct((1, LANES), F32)],
        scratch_shapes=[pltpu.VMEM((1, LANES), F32)],
        compiler_params=_params("arbitrary"),
        name="moe_router",
    )(x, w)


def _expert_kernel(be_ref, na_ref, tok_ref, tok_nxt_ref, dst_prv_ref, dst_ref, x_hbm,
                   wg_ref, wu_ref, wd_ref, y_hbm, xbuf_ref, xb_ref, acc_ref, ybuf_ref,
                   sem_in, sem_out, *, n_f, subs):
    del be_ref
    i, f = pl.program_id(0), pl.program_id(1)
    n_active = na_ref[0]
    tm = xb_ref.shape[0]
    per = tm // (n_f * len(subs))
    slot = lax.rem(i, 2)

    def tile(row):
        return pl.ds(pl.multiple_of(row * SUBLANES, SUBLANES), SUBLANES)

    def gather(tok, t, s):
        return pltpu.make_async_copy(x_hbm.at[tile(tok)], xbuf_ref.at[s, tile(t)], sem_in.at[s])

    def scatter(dst, t, s):
        return pltpu.make_async_copy(ybuf_ref.at[s, tile(t)], y_hbm.at[tile(dst)], sem_out.at[s])

    def for_rows(fn):
        def body(t, carry):
            fn(t)
            return carry
        lax.fori_loop(0, tm, body, 0, unroll=DMA_UNROLL)

    @pl.when(i < n_active)
    def _():
        @pl.when((i == 0) & (f == 0))
        def _():
            for_rows(lambda t: gather(tok_ref[0, 0, t], t, 0).start())
            ybuf_ref[1] = jnp.zeros(ybuf_ref.shape[1:], F32)

        @pl.when(f == 0)
        def _():
            for_rows(lambda t: gather(0, t, slot).wait())
            xb_ref[...] = _load_row_tiles(xbuf_ref.at[slot], tm).astype(BF16)
            acc_ref[...] = jnp.zeros_like(acc_ref)

        xb = xb_ref[...]
        off = 0
        for c, width in enumerate(subs):
            base = (f * len(subs) + c) * per
            for t in range(per):
                gather(tok_nxt_ref[0, 0, base + t], base + t, 1 - slot).start()
                scatter(dst_prv_ref[0, 0, base + t], base + t, 1 - slot).start()
            cs = slice(off, off + width)
            gate = jnp.dot(xb, wg_ref[:, cs], preferred_element_type=F32)
            up = jnp.dot(xb, wu_ref[:, cs], preferred_element_type=F32)
            hid = (gate * jax.nn.sigmoid(gate) * up).astype(BF16)
            acc_ref[...] += jnp.dot(hid, wd_ref[cs, :], preferred_element_type=F32)
            off += width

        @pl.when(f == n_f - 1)
        def _():
            @pl.when(i > 0)
            def _():
                for_rows(lambda t: scatter(0, t, slot).wait())
            _store_row_tiles(ybuf_ref.at[slot], acc_ref[...])

        @pl.when((i == n_active - 1) & (f == n_f - 1))
        def _():
            for_rows(lambda t: scatter(dst_ref[0, 0, t], t, slot).start())
            for_rows(lambda t: gather(0, t, 1 - slot).wait())
            for_rows(lambda t: scatter(0, t, 1 - slot).wait())
            for_rows(lambda t: scatter(0, t, slot).wait())

    @pl.when((i >= n_active) & (f == n_f - 1))
    def _():
        ybuf_ref[0] = jnp.zeros(ybuf_ref.shape[1:], F32)
        for_rows(lambda t: scatter(dst_ref[0, 0, t], t, 0).start())
        for_rows(lambda t: scatter(0, t, 0).wait())


def _experts(x, slot_tok, slot_dst, n_rows_out, blk_exp, n_active, wg, wu, wd):
    D = x.shape[1] * SUBLANES
    P = slot_tok.shape[0]
    F = wg.shape[2]
    tm, tf = MOE_BLOCK, EXP_TF
    subs = (EXP_SUB,) * (tf // EXP_SUB) + ((tf % EXP_SUB,) if tf % EXP_SUB else ())
    assert P % tm == 0 and F % tf == 0 and tm % ((F // tf) * len(subs)) == 0
    nb, nf = P // tm, F // tf
    toks = slot_tok.reshape(nb, 1, tm)
    spare = n_rows_out - tm + jnp.arange(tm, dtype=jnp.int32)
    dsts = jnp.concatenate([spare, slot_dst]).reshape(nb + 1, 1, tm)
    smem = lambda fn: pl.BlockSpec((1, 1, tm), fn, memory_space=pltpu.SMEM)
    grid_spec = pltpu.PrefetchScalarGridSpec(
        num_scalar_prefetch=2,
        grid=(nb, nf),
        in_specs=[smem(lambda i, f, be, na: (i, 0, 0)),
                  smem(lambda i, f, be, na: (jnp.minimum(i + 1, nb - 1), 0, 0)),
                  smem(lambda i, f, be, na: (i, 0, 0)),
                  smem(lambda i, f, be, na: (i + 1, 0, 0)),
                  pl.BlockSpec(memory_space=pl.ANY),
                  pl.BlockSpec((None, D, tf), lambda i, f, be, na: (be[i], 0, f)),
                  pl.BlockSpec((None, D, tf), lambda i, f, be, na: (be[i], 0, f)),
                  pl.BlockSpec((None, tf, D), lambda i, f, be, na: (be[i], f, 0))],
        out_specs=pl.BlockSpec(memory_space=pl.ANY),
        scratch_shapes=[pltpu.VMEM((2, tm * SUBLANES, LANES), F32), pltpu.VMEM((tm, D), BF16),
                        pltpu.VMEM((tm, D), F32), pltpu.VMEM((2, tm * SUBLANES, LANES), F32),
                        pltpu.SemaphoreType.DMA((2,)), pltpu.SemaphoreType.DMA((2,))],
    )
    return pl.pallas_call(
        functools.partial(_expert_kernel, n_f=nf, subs=subs),
        grid_spec=grid_spec,
        out_shape=jax.ShapeDtypeStruct((n_rows_out * SUBLANES, LANES), F32),
        compiler_params=_params("arbitrary", "arbitrary"),
        name="moe_experts",
    )(blk_exp, n_active, toks, toks, dsts, dsts, x, wg, wu, wd)


def _combine_kernel(y1_ref, y2_ref, slab_ref, x_ref, g_ref, b_ref, o_ref, *, alpha):
    slab = slab_ref[...]
    tb = x_ref.shape[0]
    mix = slab[:, 2:3] * _load_row_tiles(y1_ref, tb) + slab[:, 3:4] * _load_row_tiles(y2_ref, tb)
    o_ref[...] = _layer_norm(alpha * x_ref[...] + mix, g_ref[...], b_ref[...])


def _combine_res_ln(y, slab, x, g, b, alpha):
    T, D = x.shape
    tb = min(COMB_TB, T)
    assert T % tb == 0
    nb = T // tb
    return pl.pallas_call(
        functools.partial(_combine_kernel, alpha=alpha),
        grid=(nb,),
        in_specs=[pl.BlockSpec((tb * SUBLANES, LANES), lambda i: (i, 0)),
                  pl.BlockSpec((tb * SUBLANES, LANES), lambda i: (i + nb, 0)),
                  pl.BlockSpec((tb, LANES), lambda i: (i, 0)),
                  pl.BlockSpec((tb, D), lambda i: (i, 0)),
                  pl.BlockSpec((1, D), lambda i: (0, 0)),
                  pl.BlockSpec((1, D), lambda i: (0, 0))],
        out_specs=pl.BlockSpec((tb, D), lambda i: (i, 0)),
        out_shape=jax.ShapeDtypeStruct((T, D), F32),
        compiler_params=_params("parallel"),
        name="moe_combine_res_ln",
    )(y, y, slab, x, g.reshape(1, D), b.reshape(1, D))


def _moe_res_ln(x, x_tiles, w_router, wg, wu, wd, g, b, alpha):
    T, D = x.shape
    slab, counts = _router(x, w_router)
    counts = counts[0, :N_EXPERTS].astype(jnp.int32)
    padded = ((counts + MOE_BLOCK - 1) // MOE_BLOCK) * MOE_BLOCK
    pad_end = jnp.cumsum(padded)
    pad_start = pad_end - padded
    e1 = slab[:, 0].astype(jnp.int32)
    e2 = slab[:, 1].astype(jnp.int32)
    d1 = pad_start[e1] + slab[:, 4].astype(jnp.int32)
    d2 = pad_start[e2] + slab[:, 5].astype(jnp.int32)
    nblk = -(-(2 * T) // MOE_BLOCK) + N_EXPERTS
    n_slots = nblk * MOE_BLOCK
    blk_start = jnp.arange(nblk, dtype=jnp.int32) * MOE_BLOCK
    blk_exp = jnp.minimum(jnp.sum(pad_end[None, :] <= blk_start[:, None], axis=1), N_EXPERTS - 1)
    n_active = (pad_end[-1:] // MOE_BLOCK).astype(jnp.int32)
    slot = jnp.arange(n_slots, dtype=jnp.int32)
    slot_exp = jnp.sum(pad_end[None, :] <= slot[:, None], axis=1)
    first_tok = jnp.concatenate([jnp.cumsum(counts) - counts, jnp.array([2 * T], jnp.int32)])
    first_slot = jnp.concatenate([pad_start, pad_end[-1:]])
    n_tok = jnp.concatenate([counts, jnp.array([0], jnp.int32)])
    filled_before = first_tok[slot_exp] + jnp.minimum(slot - first_slot[slot_exp], n_tok[slot_exp])
    tok = jnp.arange(T, dtype=jnp.int32)
    slot_dst = (2 * T + slot - filled_before).at[jnp.concatenate([d1, d2])].set(
        jnp.concatenate([tok, T + tok]), unique_indices=True)
    slot_tok = jnp.where(slot_dst < 2 * T, slot_dst % T, 0)
    n_rows_out = n_slots + MOE_BLOCK
    y = _experts(x_tiles, slot_tok, slot_dst, n_rows_out, blk_exp.astype(jnp.int32), n_active, wg, wu, wd)
    return _combine_res_ln(y, slab, x, g, b, alpha)


def kernel(x, ln_gain, ln_bias, ret_w_in, ret_gn_gain, ret_w_out, att_w_qkv, att_w_out,
           ffn_w_gate, ffn_w_up, ffn_w_down, moe_w_router, moe_w_gate, moe_w_up, moe_w_down):
    B, S, D = x.shape
    depth = ln_gain.shape[0]
    alpha = (2 * depth) ** 0.25
    h = x.reshape(B * S, D)
    h_blocks = None
    for i in range(depth):
        j = i // 2
        if i % 2 == 0:
            proj = _matmul(h, ret_w_in[j].astype(BF16), "ret_in_proj")
            gated = _retention_core(proj, B, S, ret_gn_gain[j])
            h = _mm_res_ln(gated, ret_w_out[j].astype(BF16), h, ln_gain[i, 0], ln_bias[i, 0], alpha,
                           "ret_out_res_ln")
            h, h_blocks = _ffn_res_ln(h, ffn_w_gate[j].astype(BF16), ffn_w_up[j].astype(BF16),
                                      ffn_w_down[j].astype(BF16), ln_gain[i, 1], ln_bias[i, 1], alpha)
        else:
            gw = HEADS_PER_GROUP * ATT_HEAD_DIM
            outs = []
            for gi, (win, dil) in enumerate(DIL_PATTERN):
                w_g = jnp.concatenate(
                    [att_w_qkv[j][:, (c * N_GROUPS + gi) * gw:(c * N_GROUPS + gi + 1) * gw] for c in range(3)],
                    axis=1).astype(BF16)
                if dil == 1:
                    qkv = _matmul(h, w_g, f"att_qkv_proj_g{gi}")
                else:
                    if h_blocks is None:
                        h_blocks = jnp.transpose(h.reshape(B * S, D // LANES, LANES), (1, 0, 2))
                    qkv = _matmul_residue_view(h_blocks, w_g, dil, f"att_qkv_proj_g{gi}")
                outs.append(_dilated_group(qkv, B, S, gi, win, dil))
            h_blocks = None
            h, h_tiles = _attn_merge_res_ln(outs, att_w_out[j].astype(BF16), h, ln_gain[i, 0], ln_bias[i, 0],
                                            alpha)
            h = _moe_res_ln(h, h_tiles, moe_w_router[j], moe_w_gate[j].astype(BF16), moe_w_up[j].astype(BF16),
                            moe_w_down[j].astype(BF16), ln_gain[i, 1], ln_bias[i, 1], alpha)
    return h.reshape(B, S, D)
```

```python
import functools

import jax
import jax.numpy as jnp
from jax import lax
from jax.experimental import pallas as pl
from jax.experimental.pallas import tpu as pltpu

F32 = jnp.float32
BF16 = jnp.bfloat16

RET_HEADS = 4
RET_CHUNK = 128
ROPE_BASE = 10000.0
DIL_PATTERN = ((128, 1), (512, 4), (2048, 16))
N_GROUPS = len(DIL_PATTERN)
HEADS_PER_GROUP = 4
ATT_HEAD_DIM = 128
N_EXPERTS = 8
LN_EPS = 1e-5

LANES = 128
SUBLANES = 8
VMEM_LIMIT = 56 * 1024 * 1024

MM_TM = 1024
MM_TN = 2048
LN_TM = 512
FFN_TM = 1024
FFN_TF = 512
EXP_TF = 1792
EXP_SUB = 512
RET_ROWS = 512
ATT_Q = 128
ATT_QB = 512
ATT_OUT_W = HEADS_PER_GROUP * ATT_HEAD_DIM + LANES
MOE_BLOCK = 896
ROUTE_TB = 512
COMB_TB = 512
DMA_UNROLL = 8
NEG_BIG = -1e30


def _params(*sem):
    return pltpu.CompilerParams(dimension_semantics=sem, vmem_limit_bytes=VMEM_LIMIT)


def _layer_norm(y, g, b):
    mu = jnp.mean(y, axis=-1, keepdims=True)
    d = y - mu
    var = jnp.mean(d * d, axis=-1, keepdims=True)
    return d * lax.rsqrt(var + LN_EPS) * g + b


def _store_row_tiles(ref, y):
    m = y.shape[0]
    for c in range(SUBLANES):
        ref[pl.ds(c, m, stride=SUBLANES), :] = y[:, c * LANES:(c + 1) * LANES]


def _load_row_tiles(ref, m):
    return jnp.concatenate([ref[pl.ds(c, m, stride=SUBLANES), :] for c in range(SUBLANES)], axis=-1)


def _mm_kernel(x_ref, w_ref, o_ref, xb_ref):
    @pl.when(pl.program_id(1) == 0)
    def _():
        xb_ref[...] = x_ref[...].astype(BF16)

    o_ref[...] = jnp.dot(xb_ref[...], w_ref[...], preferred_element_type=F32).astype(o_ref.dtype)


def _matmul(x, w, name):
    M, K = x.shape
    N = w.shape[1]
    tm = min(MM_TM, M)
    tn = max(t for t in range(LANES, MM_TN + 1, LANES) if N % t == 0)
    assert M % tm == 0
    return pl.pallas_call(
        _mm_kernel,
        grid=(M // tm, N // tn),
        in_specs=[pl.BlockSpec((tm, K), lambda i, j: (i, 0)),
                  pl.BlockSpec((K, tn), lambda i, j: (0, j))],
        out_specs=pl.BlockSpec((tm, tn), lambda i, j: (i, j)),
        out_shape=jax.ShapeDtypeStruct((M, N), BF16),
        scratch_shapes=[pltpu.VMEM((tm, K), BF16)],
        compiler_params=_params("parallel", "arbitrary"),
        name=name,
    )(x, w)


def _ret_kernel(q_ref, k_ref, v_ref, g_ref, cos_ref, sin_ref, dec_ref, qdec_ref, kdec_ref,
                cdec_ref, gain_ref, o_ref, state_ref, *, n_chunks, dk):
    C, H = RET_CHUNK, RET_HEADS
    half = dk // 2
    dv = 2 * dk

    @pl.when(pl.program_id(1) == 0)
    def _():
        state_ref[...] = jnp.zeros_like(state_ref)

    def rot(t, cos, sin):
        t1, t2 = t[:, :half], t[:, half:]
        return jnp.concatenate([t1 * cos - t2 * sin, t2 * cos + t1 * sin], axis=-1)

    def chunk(c, carry):
        rows = pl.ds(pl.multiple_of(c * C, C), C)
        cos, sin = cos_ref[rows, :], sin_ref[rows, :]
        for h in range(H):
            kc = slice(h * dk, (h + 1) * dk)
            vc = slice(h * dv, (h + 1) * dv)
            q = rot(q_ref[rows, kc].astype(F32), cos, sin)
            k = rot(k_ref[rows, kc].astype(F32), cos, sin) * (dk ** -0.5)
            v = v_ref[rows, vc]
            scores = lax.dot_general(q.astype(BF16), k.astype(BF16), (((1,), (1,)), ((), ())),
                                     preferred_element_type=F32) * dec_ref[h]
            intra = jnp.dot(scores.astype(BF16), v, preferred_element_type=F32)
            state = state_ref[h]
            cross = jnp.dot((q * qdec_ref[h]).astype(BF16), state.astype(BF16),
                            preferred_element_type=F32)
            kn_t = (k * kdec_ref[h]).T.astype(BF16)
            state_ref[h] = cdec_ref[h] * state + jnp.dot(kn_t, v, preferred_element_type=F32)
            r = intra + cross
            mu = jnp.mean(r, axis=-1, keepdims=True)
            d = r - mu
            var = jnp.mean(d * d, axis=-1, keepdims=True)
            normed = d * lax.rsqrt(var + LN_EPS) * gain_ref[:, vc]
            gate = g_ref[rows, vc].astype(F32)
            o_ref[rows, vc] = (gate * jax.nn.sigmoid(gate) * normed).astype(o_ref.dtype)
        return carry

    lax.fori_loop(0, n_chunks, chunk, 0)


def _retention_core(proj, B, S, gn_gain):
    H, C = RET_HEADS, RET_CHUNK
    cols = proj.shape[1]
    dk = cols // (6 * H)
    dv = 2 * dk
    rb = min(RET_ROWS, S)
    assert S % rb == 0 and rb % C == 0
    nr = S // rb

    half = dk // 2
    inv = ROPE_BASE ** (-jnp.arange(half, dtype=F32) / half)
    ang = jnp.arange(S).astype(F32)[:, None] * inv[None, :]
    cos, sin = jnp.cos(ang), jnp.sin(ang)
    log_gamma = jnp.log(1.0 - 2.0 ** (-5.0 - jnp.arange(H, dtype=F32)))
    idx = jnp.arange(C, dtype=F32)
    rel = idx[:, None] - idx[None, :]
    decay_intra = jnp.where(rel >= 0, jnp.exp(log_gamma[:, None, None] * jnp.maximum(rel, 0.0)), 0.0)
    q_dec = jnp.exp(log_gamma[None, :] * (idx[:, None] + 1.0))
    k_dec = jnp.exp(log_gamma[None, :] * (C - 1.0 - idx[:, None]))
    chunk_dec = jnp.exp(log_gamma * C)
    qdec_b = jnp.broadcast_to(q_dec.T[:, :, None], (H, C, dk))
    kdec_b = jnp.broadcast_to(k_dec.T[:, :, None], (H, C, dk))
    cdec_b = jnp.broadcast_to(chunk_dec[:, None, None], (H, 1, dv))
    gain = gn_gain.reshape(1, H * dv)

    full = lambda shape: pl.BlockSpec(shape, lambda b, i: (0,) * len(shape))
    kern = functools.partial(_ret_kernel, n_chunks=rb // C, dk=dk)
    return pl.pallas_call(
        kern,
        grid=(B, nr),
        in_specs=[
            pl.BlockSpec((rb, H * dk), lambda b, i: (b * nr + i, 0)),
            pl.BlockSpec((rb, H * dk), lambda b, i: (b * nr + i, 1)),
            pl.BlockSpec((rb, H * dv), lambda b, i: (b * nr + i, 1)),
            pl.BlockSpec((rb, H * dv), lambda b, i: (b * nr + i, 2)),
            pl.BlockSpec((rb, half), lambda b, i: (i, 0)),
            pl.BlockSpec((rb, half), lambda b, i: (i, 0)),
            full((H, C, C)),
            full((H, C, dk)),
            full((H, C, dk)),
            full((H, 1, dv)),
            full((1, H * dv)),
        ],
        out_specs=pl.BlockSpec((rb, H * dv), lambda b, i: (b * nr + i, 0)),
        out_shape=jax.ShapeDtypeStruct((B * S, H * dv), BF16),
        scratch_shapes=[pltpu.VMEM((H, dk, dv), F32)],
        compiler_params=_params("parallel", "arbitrary"),
        name="retention_core",
    )(proj, proj, proj, proj, cos, sin, decay_intra, qdec_b, kdec_b, cdec_b, gain)


def _mm_res_ln_kernel(a_ref, w_ref, x_ref, g_ref, b_ref, o_ref, *, alpha):
    y = jnp.dot(a_ref[...], w_ref[...], preferred_element_type=F32)
    o_ref[...] = _layer_norm(alpha * x_ref[...] + y, g_ref[...], b_ref[...])


def _mm_res_ln(a, w, x, g, b, alpha, name):
    M, K = a.shape
    D = w.shape[1]
    tm = min(LN_TM, M)
    assert M % tm == 0
    return pl.pallas_call(
        functools.partial(_mm_res_ln_kernel, alpha=alpha),
        grid=(M // tm,),
        in_specs=[pl.BlockSpec((tm, K), lambda i: (i, 0)),
                  pl.BlockSpec((K, D), lambda i: (0, 0)),
                  pl.BlockSpec((tm, D), lambda i: (i, 0)),
                  pl.BlockSpec((1, D), lambda i: (0, 0)),
                  pl.BlockSpec((1, D), lambda i: (0, 0))],
        out_specs=pl.BlockSpec((tm, D), lambda i: (i, 0)),
        out_shape=jax.ShapeDtypeStruct((M, D), F32),
        compiler_params=_params("parallel"),
        name=name,
    )(a, w, x, g.reshape(1, D), b.reshape(1, D))


def _ffn_kernel(x_ref, wg_ref, wu_ref, wd_ref, g_ref, b_ref, o_ref, oc_ref, xb_ref, acc_ref, *, alpha):
    f = pl.program_id(1)

    @pl.when(f == 0)
    def _():
        xb_ref[...] = x_ref[...].astype(BF16)
        acc_ref[...] = jnp.zeros_like(acc_ref)

    xb = xb_ref[...]
    gate = jnp.dot(xb, wg_ref[...], preferred_element_type=F32)
    up = jnp.dot(xb, wu_ref[...], preferred_element_type=F32)
    hid = (gate * jax.nn.sigmoid(gate) * up).astype(BF16)
    acc_ref[...] += jnp.dot(hid, wd_ref[...], preferred_element_type=F32)

    @pl.when(f == pl.num_programs(1) - 1)
    def _():
        y = _layer_norm(alpha * x_ref[...] + acc_ref[...], g_ref[...], b_ref[...])
        o_ref[...] = y
        for c in range(oc_ref.shape[0]):
            oc_ref[c] = y[:, c * LANES:(c + 1) * LANES]


def _ffn_res_ln(x, wg, wu, wd, g, b, alpha):
    M, D = x.shape
    F = wg.shape[1]
    tm, tf = min(FFN_TM, M), FFN_TF
    assert M % tm == 0 and F % tf == 0 and D % LANES == 0
    nc = D // LANES
    return pl.pallas_call(
        functools.partial(_ffn_kernel, alpha=alpha),
        grid=(M // tm, F // tf),
        in_specs=[pl.BlockSpec((tm, D), lambda i, f: (i, 0)),
                  pl.BlockSpec((D, tf), lambda i, f: (0, f)),
                  pl.BlockSpec((D, tf), lambda i, f: (0, f)),
                  pl.BlockSpec((tf, D), lambda i, f: (f, 0)),
                  pl.BlockSpec((1, D), lambda i, f: (0, 0)),
                  pl.BlockSpec((1, D), lambda i, f: (0, 0))],
        out_specs=[pl.BlockSpec((tm, D), lambda i, f: (i, 0)),
                   pl.BlockSpec((nc, tm, LANES), lambda i, f: (0, i, 0))],
        out_shape=[jax.ShapeDtypeStruct((M, D), F32),
                   jax.ShapeDtypeStruct((nc, M, LANES), F32)],
        scratch_shapes=[pltpu.VMEM((tm, D), BF16), pltpu.VMEM((tm, D), F32)],
        compiler_params=_params("parallel", "arbitrary"),
        name="ffn_res_ln",
    )(x, wg, wu, wd, g.reshape(1, D), b.reshape(1, D))


def _mm_residue_kernel(xc_ref, w_ref, o_ref, lhs_ref, *, dil):
    nc, tm, _ = xc_ref.shape
    n = tm // dil
    N = w_ref.shape[1]
    for r in range(dil):
        for c in range(nc):
            lhs_ref[r * n:(r + 1) * n, c * LANES:(c + 1) * LANES] = (
                xc_ref[c, pl.ds(r, n, stride=dil), :].astype(BF16))
    res = jnp.dot(lhs_ref[...], w_ref[...], preferred_element_type=F32).astype(o_ref.dtype)
    for r in range(dil):
        o_ref[:, r * N:(r + 1) * N] = res[r * n:(r + 1) * n, :]


def _matmul_residue_view(xc, w, dil, name):
    nc, M, _ = xc.shape
    K, N = w.shape
    tm = min(MM_TM, M)
    assert M % tm == 0 and tm % (16 * dil) == 0 and nc * LANES == K
    return pl.pallas_call(
        functools.partial(_mm_residue_kernel, dil=dil),
        grid=(M // tm,),
        in_specs=[pl.BlockSpec((nc, tm, LANES), lambda i: (0, i, 0)),
                  pl.BlockSpec((K, N), lambda i: (0, 0))],
        out_specs=pl.BlockSpec((tm // dil, dil * N), lambda i: (i, 0)),
        out_shape=jax.ShapeDtypeStruct((M // dil, dil * N), BF16),
        scratch_shapes=[pltpu.VMEM((tm, K), BF16)],
        compiler_params=_params("parallel"),
        name=name,
    )(xc, w)


def _attn_kernel(q_ref, kp_ref, kc_ref, vp_ref, vc_ref, o_ref):
    Q, dh = ATT_Q, ATT_HEAD_DIM
    n_sub = q_ref.shape[0] // Q
    qi = pl.program_id(2)
    row = lax.broadcasted_iota(jnp.int32, (Q, 2 * Q), 0)
    col = lax.broadcasted_iota(jnp.int32, (Q, 2 * Q), 1)
    band = (col >= row) & (col <= row + Q)
    first = jnp.where(qi > 0, 0, Q)
    lane = lax.broadcasted_iota(jnp.int32, (Q, LANES), 1)
    for j in range(n_sub):
        rs = slice(j * Q, (j + 1) * Q)
        ks = slice((j - 1) * Q, (j + 1) * Q)
        valid = (band & (col >= first)) if j == 0 else band
        lse_slab = jnp.zeros((Q, LANES), F32)
        for h in range(HEADS_PER_GROUP):
            cs = slice(h * dh, (h + 1) * dh)
            if j == 0:
                k = jnp.concatenate([kp_ref[:, cs], kc_ref[:Q, cs]], axis=0)
                v = jnp.concatenate([vp_ref[:, cs], vc_ref[:Q, cs]], axis=0)
            else:
                k, v = kc_ref[ks, cs], vc_ref[ks, cs]
            s = lax.dot_general(q_ref[rs, cs], k, (((1,), (1,)), ((), ())),
                                preferred_element_type=F32) * (dh ** -0.5)
            s = jnp.where(valid, s, NEG_BIG)
            m = jnp.max(s, axis=-1, keepdims=True)
            p = jnp.exp(s - m)
            l = jnp.sum(p, axis=-1, keepdims=True)
            o = jnp.dot(p.astype(BF16), v, preferred_element_type=F32)
            o_ref[rs, cs] = o / l
            lse_slab = jnp.where(lane == h, m + jnp.log(l), lse_slab)
        o_ref[rs, HEADS_PER_GROUP * dh:] = lse_slab


def _dilated_group(qkv_view, B, S, gi, win, dil):
    Q = ATT_Q
    gw = HEADS_PER_GROUP * ATT_HEAD_DIM
    nblk = qkv_view.shape[1] // (dil * gw)
    sd = S // dil
    qb = min(ATT_QB, sd)
    assert win // dil == Q and S % dil == 0 and sd % qb == 0 and qb % Q == 0
    n_sub = qb // Q
    view = qkv_view.reshape(B, sd, qkv_view.shape[1])
    cur = lambda comp: pl.BlockSpec((None, qb, gw), lambda b, r, qi: (b, qi, r * nblk + comp))
    prev = lambda comp: pl.BlockSpec(
        (None, Q, gw), lambda b, r, qi: (b, jnp.maximum(qi * n_sub - 1, 0), r * nblk + comp))
    out = pl.pallas_call(
        _attn_kernel,
        grid=(B, dil, sd // qb),
        in_specs=[cur(0), prev(1), cur(1), prev(2), cur(2)],
        out_specs=pl.BlockSpec((None, qb, ATT_OUT_W), lambda b, r, qi: (b, qi, r)),
        out_shape=jax.ShapeDtypeStruct((B, sd, dil * ATT_OUT_W), F32),
        compiler_params=_params("parallel", "parallel", "arbitrary"),
        name=f"dilated_attn_g{gi}",
    )(view, view, view, view, view)
    return out.reshape(B * sd, dil * ATT_OUT_W)


def _attn_merge_kernel(o0_ref, o1_ref, o2_ref, w_ref, x_ref, g_ref, b_ref, o_ref, ot_ref, *scratch, alpha):
    dh = ATT_HEAD_DIM
    n_slab = ATT_OUT_W // LANES
    tm = x_ref.shape[0]
    groups = []
    scratch = list(scratch)
    for ref, (_, dil) in zip((o0_ref, o1_ref, o2_ref), DIL_PATTERN):
        if dil == 1:
            groups.append(lambda c, ref=ref: ref[:, c * LANES:(c + 1) * LANES])
            continue
        scr = scratch.pop(0)
        n = tm // dil
        for r in range(dil):
            for c in range(n_slab):
                col = r * ATT_OUT_W + c * LANES
                scr[c, pl.ds(r, n, stride=dil), :] = ref[:, col:col + LANES]
        groups.append(lambda c, scr=scr: scr[c])
    lses = [grp(n_slab - 1) for grp in groups]
    mx = jnp.maximum(jnp.maximum(lses[0], lses[1]), lses[2])
    ws = [jnp.exp(l - mx) for l in lses]
    den = ws[0] + ws[1] + ws[2]
    ws = [w / den for w in ws]
    heads = []
    for h in range(HEADS_PER_GROUP):
        acc = ws[0][:, h:h + 1] * groups[0](h)
        acc += ws[1][:, h:h + 1] * groups[1](h)
        acc += ws[2][:, h:h + 1] * groups[2](h)
        heads.append(acc)
    merged = jnp.concatenate(heads, axis=-1).astype(BF16)
    y = jnp.dot(merged, w_ref[...], preferred_element_type=F32)
    out = _layer_norm(alpha * x_ref[...] + y, g_ref[...], b_ref[...])
    o_ref[...] = out
    _store_row_tiles(ot_ref, out)


def _attn_merge_res_ln(outs, w, x, g, b, alpha):
    M, D = x.shape
    K = w.shape[0]
    tm = min(LN_TM, M)
    dils = [dil for _, dil in DIL_PATTERN]
    assert M % tm == 0 and all(tm % (8 * dil) == 0 for dil in dils) and D == SUBLANES * LANES
    ospecs = [pl.BlockSpec((tm // dil, dil * ATT_OUT_W), lambda i: (i, 0)) for dil in dils]
    return pl.pallas_call(
        functools.partial(_attn_merge_kernel, alpha=alpha),
        grid=(M // tm,),
        in_specs=ospecs + [
                  pl.BlockSpec((K, D), lambda i: (0, 0)),
                  pl.BlockSpec((tm, D), lambda i: (i, 0)),
                  pl.BlockSpec((1, D), lambda i: (0, 0)),
                  pl.BlockSpec((1, D), lambda i: (0, 0))],
        out_specs=[pl.BlockSpec((tm, D), lambda i: (i, 0)),
                   pl.BlockSpec((tm * SUBLANES, LANES), lambda i: (i, 0))],
        out_shape=[jax.ShapeDtypeStruct((M, D), F32),
                   jax.ShapeDtypeStruct((M * SUBLANES, LANES), F32)],
        scratch_shapes=[pltpu.VMEM((ATT_OUT_W // LANES, tm, LANES), F32) for dil in dils if dil > 1],
        compiler_params=_params("parallel"),
        name="attn_merge_res_ln",
    )(*outs, w, x, g.reshape(1, D), b.reshape(1, D))


def _router_kernel(x_ref, wh_ref, wl_ref, slab_ref, cnt_ref, run_ref):
    tb = x_ref.shape[0]

    @pl.when(pl.program_id(0) == 0)
    def _():
        run_ref[...] = jnp.zeros_like(run_ref)

    x = x_ref[...]
    xh = x.astype(BF16)
    xl = (x - xh.astype(F32)).astype(BF16)
    logits = (jnp.dot(xh, wh_ref[...], preferred_element_type=F32)
              + jnp.dot(xh, wl_ref[...], preferred_element_type=F32)
              + jnp.dot(xl, wh_ref[...], preferred_element_type=F32))
    lane = lax.broadcasted_iota(jnp.int32, (tb, LANES), 1)
    logits = jnp.where(lane < N_EXPERTS, logits, NEG_BIG)
    m1 = jnp.max(logits, axis=-1, keepdims=True)
    i1 = jnp.min(jnp.where(logits == m1, lane, LANES), axis=-1, keepdims=True)
    rest = jnp.where(lane == i1, NEG_BIG, logits)
    m2 = jnp.max(rest, axis=-1, keepdims=True)
    i2 = jnp.min(jnp.where(rest == m2, lane, LANES), axis=-1, keepdims=True)
    e = jnp.exp(m2 - m1)
    g1 = 1.0 / (1.0 + e)
    g2 = e / (1.0 + e)
    onehot = jnp.where((lane == i1) | (lane == i2), 1.0, 0.0)
    r = lax.broadcasted_iota(jnp.int32, (tb, tb), 0)
    c = lax.broadcasted_iota(jnp.int32, (tb, tb), 1)
    lower = jnp.where(c < r, 1.0, 0.0).astype(BF16)
    before = jnp.dot(lower, onehot.astype(BF16), preferred_element_type=F32) + run_ref[...]
    rank1 = jnp.sum(jnp.where(lane == i1, before, 0.0), axis=-1, keepdims=True)
    rank2 = jnp.sum(jnp.where(lane == i2, before, 0.0), axis=-1, keepdims=True)
    total = run_ref[...] + jnp.sum(onehot, axis=0, keepdims=True)
    run_ref[...] = total
    cnt_ref[...] = total
    slab = jnp.where(lane == 0, i1.astype(F32), 0.0)
    slab = jnp.where(lane == 1, i2.astype(F32), slab)
    slab = jnp.where(lane == 2, g1, slab)
    slab = jnp.where(lane == 3, g2, slab)
    slab = jnp.where(lane == 4, rank1, slab)
    slab = jnp.where(lane == 5, rank2, slab)
    slab_ref[...] = slab


def _router(x, w_router):
    T, D = x.shape
    tb = min(ROUTE_TB, T)
    assert T % tb == 0
    w = jnp.zeros((D, LANES), F32).at[:, :N_EXPERTS].set(w_router)
    wh = w.astype(BF16)
    wl = (w - wh.astype(F32)).astype(BF16)
    return pl.pallas_call(
        _router_kernel,
        grid=(T // tb,),
        in_specs=[pl.BlockSpec((tb, D), lambda i: (i, 0)),
                  pl.BlockSpec((D, LANES), lambda i: (0, 0)),
                  pl.BlockSpec((D, LANES), lambda i: (0, 0))],
        out_specs=[pl.BlockSpec((tb, LANES), lambda i: (i, 0)),
                   pl.BlockSpec((1, LANES), lambda i: (0, 0))],
        out_shape=[jax.ShapeDtypeStruct((T, LANES), F32),
                   jax.ShapeDtypeStruct((1, LANES), F32)],
        scratch_shapes=[pltpu.VMEM((1, LANES), F32)],
        compiler_params=_params("arbitrary"),
        name="moe_router",
    )(x, wh, wl)


def _expert_kernel(be_ref, na_ref, tok_ref, tok_nxt_ref, dst_prv_ref, dst_ref, x_hbm,
                   wg_ref, wu_ref, wd_ref, y_hbm, xbuf_ref, xb_ref, acc_ref, ybuf_ref,
                   hid_ref, sem_in, sem_out, *, n_f, subs):
    del be_ref
    i, f = pl.program_id(0), pl.program_id(1)
    n_active = na_ref[0]
    tm = xb_ref.shape[0]
    per = tm // (n_f * len(subs))
    slot = lax.rem(i, 2)

    def tile(row):
        return pl.ds(pl.multiple_of(row * SUBLANES, SUBLANES), SUBLANES)

    def gather(tok, t, s):
        return pltpu.make_async_copy(x_hbm.at[tile(tok)], xbuf_ref.at[s, tile(t)], sem_in.at[s])

    def scatter(dst, t, s):
        return pltpu.make_async_copy(ybuf_ref.at[s, tile(t)], y_hbm.at[tile(dst)], sem_out.at[s])

    def for_rows(fn):
        def body(t, carry):
            fn(t)
            return carry
        lax.fori_loop(0, tm, body, 0, unroll=DMA_UNROLL)

    @pl.when(i < n_active)
    def _():
        @pl.when((i == 0) & (f == 0))
        def _():
            for_rows(lambda t: gather(tok_ref[0, 0, t], t, 0).start())
            ybuf_ref[1] = jnp.zeros(ybuf_ref.shape[1:], F32)

        @pl.when(f == 0)
        def _():
            for_rows(lambda t: gather(0, t, slot).wait())
            xb_ref[...] = _load_row_tiles(xbuf_ref.at[slot], tm).astype(BF16)
            acc_ref[...] = jnp.zeros_like(acc_ref)

        xb = xb_ref[...]
        off = 0
        for c, width in enumerate(subs):
            base = (f * len(subs) + c) * per
            for t in range(per):
                gather(tok_nxt_ref[0, 0, base + t], base + t, 1 - slot).start()
                scatter(dst_prv_ref[0, 0, base + t], base + t, 1 - slot).start()
            cs = slice(off, off + width)
            gate = jnp.dot(xb, wg_ref[:, cs], preferred_element_type=F32)
            up = jnp.dot(xb, wu_ref[:, cs], preferred_element_type=F32)
            hid_ref[:, cs] = (gate * jax.nn.sigmoid(gate) * up).astype(BF16)
            off += width
        acc_ref[...] += jnp.dot(hid_ref[...], wd_ref[...], preferred_element_type=F32)

        @pl.when(f == n_f - 1)
        def _():
            @pl.when(i > 0)
            def _():
                for_rows(lambda t: scatter(0, t, slot).wait())
            _store_row_tiles(ybuf_ref.at[slot], acc_ref[...])

        @pl.when((i == n_active - 1) & (f == n_f - 1))
        def _():
            for_rows(lambda t: scatter(dst_ref[0, 0, t], t, slot).start())
            for_rows(lambda t: gather(0, t, 1 - slot).wait())
            for_rows(lambda t: scatter(0, t, 1 - slot).wait())
            for_rows(lambda t: scatter(0, t, slot).wait())

    @pl.when((i >= n_active) & (f == n_f - 1))
    def _():
        ybuf_ref[0] = jnp.zeros(ybuf_ref.shape[1:], F32)
        for_rows(lambda t: scatter(dst_ref[0, 0, t], t, 0).start())
        for_rows(lambda t: scatter(0, t, 0).wait())


def _experts(x, slot_tok, slot_dst, n_rows_out, blk_exp, n_active, wg, wu, wd, layer):
    D = x.shape[1] * SUBLANES
    P = slot_tok.shape[0]
    F = wg.shape[3]
    tm, tf = MOE_BLOCK, EXP_TF
    subs = (EXP_SUB,) * (tf // EXP_SUB) + ((tf % EXP_SUB,) if tf % EXP_SUB else ())
    assert P % tm == 0 and F % tf == 0 and tm % ((F // tf) * len(subs)) == 0
    nb, nf = P // tm, F // tf
    toks = slot_tok.reshape(nb, 1, tm)
    spare = n_rows_out - tm + jnp.arange(tm, dtype=jnp.int32)
    dsts = jnp.concatenate([spare, slot_dst]).reshape(nb + 1, 1, tm)
    smem = lambda fn: pl.BlockSpec((1, 1, tm), fn, memory_space=pltpu.SMEM)
    grid_spec = pltpu.PrefetchScalarGridSpec(
        num_scalar_prefetch=2,
        grid=(nb, nf),
        in_specs=[smem(lambda i, f, be, na: (i, 0, 0)),
                  smem(lambda i, f, be, na: (jnp.minimum(i + 1, nb - 1), 0, 0)),
                  smem(lambda i, f, be, na: (i, 0, 0)),
                  smem(lambda i, f, be, na: (i + 1, 0, 0)),
                  pl.BlockSpec(memory_space=pl.ANY),
                  pl.BlockSpec((None, None, D, tf), lambda i, f, be, na: (layer, be[i], 0, f)),
                  pl.BlockSpec((None, None, D, tf), lambda i, f, be, na: (layer, be[i], 0, f)),
                  pl.BlockSpec((None, None, tf, D), lambda i, f, be, na: (layer, be[i], f, 0))],
        out_specs=pl.BlockSpec(memory_space=pl.ANY),
        scratch_shapes=[pltpu.VMEM((2, tm * SUBLANES, LANES), F32), pltpu.VMEM((tm, D), BF16),
                        pltpu.VMEM((tm, D), F32), pltpu.VMEM((2, tm * SUBLANES, LANES), F32),
                        pltpu.VMEM((tm, tf), BF16),
                        pltpu.SemaphoreType.DMA((2,)), pltpu.SemaphoreType.DMA((2,))],
    )
    return pl.pallas_call(
        functools.partial(_expert_kernel, n_f=nf, subs=subs),
        grid_spec=grid_spec,
        out_shape=jax.ShapeDtypeStruct((n_rows_out * SUBLANES, LANES), F32),
        compiler_params=_params("arbitrary", "arbitrary"),
        name="moe_experts",
    )(blk_exp, n_active, toks, toks, dsts, dsts, x, wg, wu, wd)


def _combine_kernel(y1_ref, y2_ref, slab_ref, x_ref, g_ref, b_ref, o_ref, *, alpha):
    slab = slab_ref[...]
    tb = x_ref.shape[0]
    mix = slab[:, 2:3] * _load_row_tiles(y1_ref, tb) + slab[:, 3:4] * _load_row_tiles(y2_ref, tb)
    o_ref[...] = _layer_norm(alpha * x_ref[...] + mix, g_ref[...], b_ref[...])


def _combine_res_ln(y, slab, x, g, b, alpha):
    T, D = x.shape
    tb = min(COMB_TB, T)
    assert T % tb == 0
    nb = T // tb
    return pl.pallas_call(
        functools.partial(_combine_kernel, alpha=alpha),
        grid=(nb,),
        in_specs=[pl.BlockSpec((tb * SUBLANES, LANES), lambda i: (i, 0)),
                  pl.BlockSpec((tb * SUBLANES, LANES), lambda i: (i + nb, 0)),
                  pl.BlockSpec((tb, LANES), lambda i: (i, 0)),
                  pl.BlockSpec((tb, D), lambda i: (i, 0)),
                  pl.BlockSpec((1, D), lambda i: (0, 0)),
                  pl.BlockSpec((1, D), lambda i: (0, 0))],
        out_specs=pl.BlockSpec((tb, D), lambda i: (i, 0)),
        out_shape=jax.ShapeDtypeStruct((T, D), F32),
        compiler_params=_params("parallel"),
        name="moe_combine_res_ln",
    )(y, y, slab, x, g.reshape(1, D), b.reshape(1, D))


def _moe_res_ln(x, x_tiles, w_router, wg, wu, wd, layer, g, b, alpha):
    T, D = x.shape
    slab, counts = _router(x, w_router)
    counts = counts[0, :N_EXPERTS].astype(jnp.int32)
    padded = ((counts + MOE_BLOCK - 1) // MOE_BLOCK) * MOE_BLOCK
    pad_end = jnp.cumsum(padded)
    pad_start = pad_end - padded
    e1 = slab[:, 0].astype(jnp.int32)
    e2 = slab[:, 1].astype(jnp.int32)
    d1 = pad_start[e1] + slab[:, 4].astype(jnp.int32)
    d2 = pad_start[e2] + slab[:, 5].astype(jnp.int32)
    nblk = -(-(2 * T) // MOE_BLOCK) + N_EXPERTS
    n_slots = nblk * MOE_BLOCK
    blk_start = jnp.arange(nblk, dtype=jnp.int32) * MOE_BLOCK
    blk_exp = jnp.minimum(jnp.sum(pad_end[None, :] <= blk_start[:, None], axis=1), N_EXPERTS - 1)
    n_active = (pad_end[-1:] // MOE_BLOCK).astype(jnp.int32)
    slot = jnp.arange(n_slots, dtype=jnp.int32)
    slot_exp = jnp.sum(pad_end[None, :] <= slot[:, None], axis=1)
    first_tok = jnp.concatenate([jnp.cumsum(counts) - counts, jnp.array([2 * T], jnp.int32)])
    first_slot = jnp.concatenate([pad_start, pad_end[-1:]])
    n_tok = jnp.concatenate([counts, jnp.array([0], jnp.int32)])
    filled_before = first_tok[slot_exp] + jnp.minimum(slot - first_slot[slot_exp], n_tok[slot_exp])
    tok = jnp.arange(T, dtype=jnp.int32)
    slot_dst = (2 * T + slot - filled_before).at[jnp.concatenate([d1, d2])].set(
        jnp.concatenate([tok, T + tok]), unique_indices=True)
    slot_tok = jnp.where(slot_dst < 2 * T, slot_dst % T, 0)
    n_rows_out = n_slots + MOE_BLOCK
    y = _experts(x_tiles, slot_tok, slot_dst, n_rows_out, blk_exp.astype(jnp.int32), n_active,
                 wg, wu, wd, layer)
    return _combine_res_ln(y, slab, x, g, b, alpha)


def kernel(x, ln_gain, ln_bias, ret_w_in, ret_gn_gain, ret_w_out, att_w_qkv, att_w_out,
           ffn_w_gate, ffn_w_up, ffn_w_down, moe_w_router, moe_w_gate, moe_w_up, moe_w_down):
    B, S, D = x.shape
    depth = ln_gain.shape[0]
    alpha = (2 * depth) ** 0.25
    moe_wg, moe_wu, moe_wd = (w.astype(BF16) for w in (moe_w_gate, moe_w_up, moe_w_down))
    h = x.reshape(B * S, D)
    h_blocks = None
    for i in range(depth):
        j = i // 2
        if i % 2 == 0:
            proj = _matmul(h, ret_w_in[j].astype(BF16), "ret_in_proj")
            gated = _retention_core(proj, B, S, ret_gn_gain[j])
            h = _mm_res_ln(gated, ret_w_out[j].astype(BF16), h, ln_gain[i, 0], ln_bias[i, 0], alpha,
                           "ret_out_res_ln")
            h, h_blocks = _ffn_res_ln(h, ffn_w_gate[j].astype(BF16), ffn_w_up[j].astype(BF16),
                                      ffn_w_down[j].astype(BF16), ln_gain[i, 1], ln_bias[i, 1], alpha)
        else:
            gw = HEADS_PER_GROUP * ATT_HEAD_DIM
            outs = []
            for gi, (win, dil) in enumerate(DIL_PATTERN):
                w_g = jnp.concatenate(
                    [att_w_qkv[j][:, (c * N_GROUPS + gi) * gw:(c * N_GROUPS + gi + 1) * gw] for c in range(3)],
                    axis=1).astype(BF16)
                if dil == 1:
                    qkv = _matmul(h, w_g, f"att_qkv_proj_g{gi}")
                else:
                    if h_blocks is None:
                        h_blocks = jnp.transpose(h.reshape(B * S, D // LANES, LANES), (1, 0, 2))
                    qkv = _matmul_residue_view(h_blocks, w_g, dil, f"att_qkv_proj_g{gi}")
                outs.append(_dilated_group(qkv, B, S, gi, win, dil))
            h_blocks = None
            h, h_tiles = _attn_merge_res_ln(outs, att_w_out[j].astype(BF16), h, ln_gain[i, 0], ln_bias[i, 0],
                                            alpha)
            h = _moe_res_ln(h, h_tiles, moe_w_router[j], moe_wg, moe_wu, moe_wd, j,
                            ln_gain[i, 1], ln_bias[i, 1], alpha)
    return h.reshape(B, S, D)
```

```python
import functools

import jax
import jax.numpy as jnp
from jax import lax
from jax.experimental import pallas as pl
from jax.experimental.pallas import tpu as pltpu

F32 = jnp.float32
BF16 = jnp.bfloat16

RET_HEADS = 4
RET_CHUNK = 128
ROPE_BASE = 10000.0
DIL_PATTERN = ((128, 1), (512, 4), (2048, 16))
N_GROUPS = len(DIL_PATTERN)
HEADS_PER_GROUP = 4
ATT_HEAD_DIM = 128
N_EXPERTS = 8
LN_EPS = 1e-5

LANES = 128
SUBLANES = 8
VMEM_LIMIT = 56 * 1024 * 1024

MM_TM = 1024
MM_TN = 2048
LN_TM = 512
FFN_TM = 1024
FFN_TF = 512
EXP_TF = 512
EXP_SUB = 512
RET_ROWS = 512
ATT_Q = 128
ATT_QB = 512
ATT_OUT_W = HEADS_PER_GROUP * ATT_HEAD_DIM + LANES
MOE_BLOCK = 896
ROUTE_TB = 512
COMB_TB = 512
DMA_UNROLL = 8
NEG_BIG = -1e30


def _params(*sem):
    return pltpu.CompilerParams(dimension_semantics=sem, vmem_limit_bytes=VMEM_LIMIT)


def _layer_norm(y, g, b):
    mu = jnp.mean(y, axis=-1, keepdims=True)
    d = y - mu
    var = jnp.mean(d * d, axis=-1, keepdims=True)
    return d * lax.rsqrt(var + LN_EPS) * g + b


def _store_row_tiles(ref, y):
    m = y.shape[0]
    for c in range(SUBLANES):
        ref[pl.ds(c, m, stride=SUBLANES), :] = y[:, c * LANES:(c + 1) * LANES]


def _load_row_tiles(ref, m):
    return jnp.concatenate([ref[pl.ds(c, m, stride=SUBLANES), :] for c in range(SUBLANES)], axis=-1)


def _mm_kernel(x_ref, w_ref, o_ref, xb_ref):
    @pl.when(pl.program_id(1) == 0)
    def _():
        xb_ref[...] = x_ref[...].astype(BF16)

    o_ref[...] = jnp.dot(xb_ref[...], w_ref[...], preferred_element_type=F32).astype(o_ref.dtype)


def _matmul(x, w, name):
    M, K = x.shape
    N = w.shape[1]
    tm = min(MM_TM, M)
    tn = max(t for t in range(LANES, MM_TN + 1, LANES) if N % t == 0)
    assert M % tm == 0
    return pl.pallas_call(
        _mm_kernel,
        grid=(M // tm, N // tn),
        in_specs=[pl.BlockSpec((tm, K), lambda i, j: (i, 0)),
                  pl.BlockSpec((K, tn), lambda i, j: (0, j))],
        out_specs=pl.BlockSpec((tm, tn), lambda i, j: (i, j)),
        out_shape=jax.ShapeDtypeStruct((M, N), BF16),
        scratch_shapes=[pltpu.VMEM((tm, K), BF16)],
        compiler_params=_params("parallel", "arbitrary"),
        name=name,
    )(x, w)


def _ret_kernel(q_ref, k_ref, v_ref, g_ref, cos_ref, sin_ref, dec_ref, qdec_ref, kdec_ref,
                cdec_ref, gain_ref, o_ref, state_ref, *, n_chunks, dk):
    C, H = RET_CHUNK, RET_HEADS
    half = dk // 2
    dv = 2 * dk

    @pl.when(pl.program_id(1) == 0)
    def _():
        state_ref[...] = jnp.zeros_like(state_ref)

    def rot(t, cos, sin):
        t1, t2 = t[:, :half], t[:, half:]
        return jnp.concatenate([t1 * cos - t2 * sin, t2 * cos + t1 * sin], axis=-1)

    def chunk(c, carry):
        rows = pl.ds(pl.multiple_of(c * C, C), C)
        cos, sin = cos_ref[rows, :], sin_ref[rows, :]
        for h in range(H):
            kc = slice(h * dk, (h + 1) * dk)
            vc = slice(h * dv, (h + 1) * dv)
            q = rot(q_ref[rows, kc].astype(F32), cos, sin)
            k = rot(k_ref[rows, kc].astype(F32), cos, sin) * (dk ** -0.5)
            v = v_ref[rows, vc]
            scores = lax.dot_general(q.astype(BF16), k.astype(BF16), (((1,), (1,)), ((), ())),
                                     preferred_element_type=F32) * dec_ref[h]
            intra = jnp.dot(scores.astype(BF16), v, preferred_element_type=F32)
            state = state_ref[h]
            cross = jnp.dot((q * qdec_ref[h]).astype(BF16), state.astype(BF16),
                            preferred_element_type=F32)
            kn_t = (k * kdec_ref[h]).T.astype(BF16)
            state_ref[h] = cdec_ref[h] * state + jnp.dot(kn_t, v, preferred_element_type=F32)
            r = intra + cross
            mu = jnp.mean(r, axis=-1, keepdims=True)
            d = r - mu
            var = jnp.mean(d * d, axis=-1, keepdims=True)
            normed = d * lax.rsqrt(var + LN_EPS) * gain_ref[:, vc]
            gate = g_ref[rows, vc].astype(F32)
            o_ref[rows, vc] = (gate * jax.nn.sigmoid(gate) * normed).astype(o_ref.dtype)
        return carry

    lax.fori_loop(0, n_chunks, chunk, 0)


def _retention_core(proj, B, S, gn_gain):
    H, C = RET_HEADS, RET_CHUNK
    cols = proj.shape[1]
    dk = cols // (6 * H)
    dv = 2 * dk
    rb = min(RET_ROWS, S)
    assert S % rb == 0 and rb % C == 0
    nr = S // rb

    half = dk // 2
    inv = ROPE_BASE ** (-jnp.arange(half, dtype=F32) / half)
    ang = jnp.arange(S).astype(F32)[:, None] * inv[None, :]
    cos, sin = jnp.cos(ang), jnp.sin(ang)
    log_gamma = jnp.log(1.0 - 2.0 ** (-5.0 - jnp.arange(H, dtype=F32)))
    idx = jnp.arange(C, dtype=F32)
    rel = idx[:, None] - idx[None, :]
    decay_intra = jnp.where(rel >= 0, jnp.exp(log_gamma[:, None, None] * jnp.maximum(rel, 0.0)), 0.0)
    q_dec = jnp.exp(log_gamma[None, :] * (idx[:, None] + 1.0))
    k_dec = jnp.exp(log_gamma[None, :] * (C - 1.0 - idx[:, None]))
    chunk_dec = jnp.exp(log_gamma * C)
    qdec_b = jnp.broadcast_to(q_dec.T[:, :, None], (H, C, dk))
    kdec_b = jnp.broadcast_to(k_dec.T[:, :, None], (H, C, dk))
    cdec_b = jnp.broadcast_to(chunk_dec[:, None, None], (H, 1, dv))
    gain = gn_gain.reshape(1, H * dv)

    full = lambda shape: pl.BlockSpec(shape, lambda b, i: (0,) * len(shape))
    kern = functools.partial(_ret_kernel, n_chunks=rb // C, dk=dk)
    return pl.pallas_call(
        kern,
        grid=(B, nr),
        in_specs=[
            pl.BlockSpec((rb, H * dk), lambda b, i: (b * nr + i, 0)),
            pl.BlockSpec((rb, H * dk), lambda b, i: (b * nr + i, 1)),
            pl.BlockSpec((rb, H * dv), lambda b, i: (b * nr + i, 1)),
            pl.BlockSpec((rb, H * dv), lambda b, i: (b * nr + i, 2)),
            pl.BlockSpec((rb, half), lambda b, i: (i, 0)),
            pl.BlockSpec((rb, half), lambda b, i: (i, 0)),
            full((H, C, C)),
            full((H, C, dk)),
            full((H, C, dk)),
            full((H, 1, dv)),
            full((1, H * dv)),
        ],
        out_specs=pl.BlockSpec((rb, H * dv), lambda b, i: (b * nr + i, 0)),
        out_shape=jax.ShapeDtypeStruct((B * S, H * dv), BF16),
        scratch_shapes=[pltpu.VMEM((H, dk, dv), F32)],
        compiler_params=_params("parallel", "arbitrary"),
        name="retention_core",
    )(proj, proj, proj, proj, cos, sin, decay_intra, qdec_b, kdec_b, cdec_b, gain)


def _mm_res_ln_kernel(a_ref, w_ref, x_ref, g_ref, b_ref, o_ref, *, alpha):
    y = jnp.dot(a_ref[...], w_ref[...], preferred_element_type=F32)
    o_ref[...] = _layer_norm(alpha * x_ref[...] + y, g_ref[...], b_ref[...])


def _mm_res_ln(a, w, x, g, b, alpha, name):
    M, K = a.shape
    D = w.shape[1]
    tm = min(LN_TM, M)
    assert M % tm == 0
    return pl.pallas_call(
        functools.partial(_mm_res_ln_kernel, alpha=alpha),
        grid=(M // tm,),
        in_specs=[pl.BlockSpec((tm, K), lambda i: (i, 0)),
                  pl.BlockSpec((K, D), lambda i: (0, 0)),
                  pl.BlockSpec((tm, D), lambda i: (i, 0)),
                  pl.BlockSpec((1, D), lambda i: (0, 0)),
                  pl.BlockSpec((1, D), lambda i: (0, 0))],
        out_specs=pl.BlockSpec((tm, D), lambda i: (i, 0)),
        out_shape=jax.ShapeDtypeStruct((M, D), F32),
        compiler_params=_params("parallel"),
        name=name,
    )(a, w, x, g.reshape(1, D), b.reshape(1, D))


def _ffn_kernel(x_ref, wg_ref, wu_ref, wd_ref, g_ref, b_ref, o_ref, oc_ref, xb_ref, acc_ref, *, alpha):
    f = pl.program_id(1)

    @pl.when(f == 0)
    def _():
        xb_ref[...] = x_ref[...].astype(BF16)
        acc_ref[...] = jnp.zeros_like(acc_ref)

    xb = xb_ref[...]
    gate = jnp.dot(xb, wg_ref[...], preferred_element_type=F32)
    up = jnp.dot(xb, wu_ref[...], preferred_element_type=F32)
    hid = (gate * jax.nn.sigmoid(gate) * up).astype(BF16)
    acc_ref[...] += jnp.dot(hid, wd_ref[...], preferred_element_type=F32)

    @pl.when(f == pl.num_programs(1) - 1)
    def _():
        y = _layer_norm(alpha * x_ref[...] + acc_ref[...], g_ref[...], b_ref[...])
        o_ref[...] = y
        for c in range(oc_ref.shape[0]):
            oc_ref[c] = y[:, c * LANES:(c + 1) * LANES]


def _ffn_res_ln(x, wg, wu, wd, g, b, alpha):
    M, D = x.shape
    F = wg.shape[1]
    tm, tf = min(FFN_TM, M), FFN_TF
    assert M % tm == 0 and F % tf == 0 and D % LANES == 0
    nc = D // LANES
    return pl.pallas_call(
        functools.partial(_ffn_kernel, alpha=alpha),
        grid=(M // tm, F // tf),
        in_specs=[pl.BlockSpec((tm, D), lambda i, f: (i, 0)),
                  pl.BlockSpec((D, tf), lambda i, f: (0, f)),
                  pl.BlockSpec((D, tf), lambda i, f: (0, f)),
                  pl.BlockSpec((tf, D), lambda i, f: (f, 0)),
                  pl.BlockSpec((1, D), lambda i, f: (0, 0)),
                  pl.BlockSpec((1, D), lambda i, f: (0, 0))],
        out_specs=[pl.BlockSpec((tm, D), lambda i, f: (i, 0)),
                   pl.BlockSpec((nc, tm, LANES), lambda i, f: (0, i, 0))],
        out_shape=[jax.ShapeDtypeStruct((M, D), F32),
                   jax.ShapeDtypeStruct((nc, M, LANES), F32)],
        scratch_shapes=[pltpu.VMEM((tm, D), BF16), pltpu.VMEM((tm, D), F32)],
        compiler_params=_params("parallel", "arbitrary"),
        name="ffn_res_ln",
    )(x, wg, wu, wd, g.reshape(1, D), b.reshape(1, D))


def _mm_residue_kernel(xc_ref, w_ref, o_ref, lhs_ref, *, dil):
    nc, tm, _ = xc_ref.shape
    n = tm // dil
    N = w_ref.shape[1]
    for r in range(dil):
        for c in range(nc):
            lhs_ref[r * n:(r + 1) * n, c * LANES:(c + 1) * LANES] = (
                xc_ref[c, pl.ds(r, n, stride=dil), :].astype(BF16))
    res = jnp.dot(lhs_ref[...], w_ref[...], preferred_element_type=F32).astype(o_ref.dtype)
    for r in range(dil):
        o_ref[:, r * N:(r + 1) * N] = res[r * n:(r + 1) * n, :]


def _matmul_residue_view(xc, w, dil, name):
    nc, M, _ = xc.shape
    K, N = w.shape
    tm = min(MM_TM, M)
    assert M % tm == 0 and tm % (16 * dil) == 0 and nc * LANES == K
    return pl.pallas_call(
        functools.partial(_mm_residue_kernel, dil=dil),
        grid=(M // tm,),
        in_specs=[pl.BlockSpec((nc, tm, LANES), lambda i: (0, i, 0)),
                  pl.BlockSpec((K, N), lambda i: (0, 0))],
        out_specs=pl.BlockSpec((tm // dil, dil * N), lambda i: (i, 0)),
        out_shape=jax.ShapeDtypeStruct((M // dil, dil * N), BF16),
        scratch_shapes=[pltpu.VMEM((tm, K), BF16)],
        compiler_params=_params("parallel"),
        name=name,
    )(xc, w)


def _attn_kernel(q_ref, kp_ref, kc_ref, vp_ref, vc_ref, o_ref):
    Q, dh = ATT_Q, ATT_HEAD_DIM
    n_sub = q_ref.shape[0] // Q
    qi = pl.program_id(2)
    row = lax.broadcasted_iota(jnp.int32, (Q, 2 * Q), 0)
    col = lax.broadcasted_iota(jnp.int32, (Q, 2 * Q), 1)
    band = (col >= row) & (col <= row + Q)
    first = jnp.where(qi > 0, 0, Q)
    lane = lax.broadcasted_iota(jnp.int32, (Q, LANES), 1)
    for j in range(n_sub):
        rs = slice(j * Q, (j + 1) * Q)
        ks = slice((j - 1) * Q, (j + 1) * Q)
        valid = (band & (col >= first)) if j == 0 else band
        lse_slab = jnp.zeros((Q, LANES), F32)
        for h in range(HEADS_PER_GROUP):
            cs = slice(h * dh, (h + 1) * dh)
            if j == 0:
                k = jnp.concatenate([kp_ref[:, cs], kc_ref[:Q, cs]], axis=0)
                v = jnp.concatenate([vp_ref[:, cs], vc_ref[:Q, cs]], axis=0)
            else:
                k, v = kc_ref[ks, cs], vc_ref[ks, cs]
            s = lax.dot_general(q_ref[rs, cs], k, (((1,), (1,)), ((), ())),
                                preferred_element_type=F32) * (dh ** -0.5)
            s = jnp.where(valid, s, NEG_BIG)
            m = jnp.max(s, axis=-1, keepdims=True)
            p = jnp.exp(s - m)
            l = jnp.sum(p, axis=-1, keepdims=True)
            o = jnp.dot(p.astype(BF16), v, preferred_element_type=F32)
            o_ref[rs, cs] = o / l
            lse_slab = jnp.where(lane == h, m + jnp.log(l), lse_slab)
        o_ref[rs, HEADS_PER_GROUP * dh:] = lse_slab


def _dilated_group(qkv_view, B, S, gi, win, dil):
    Q = ATT_Q
    gw = HEADS_PER_GROUP * ATT_HEAD_DIM
    nblk = qkv_view.shape[1] // (dil * gw)
    sd = S // dil
    qb = min(ATT_QB, sd)
    assert win // dil == Q and S % dil == 0 and sd % qb == 0 and qb % Q == 0
    n_sub = qb // Q
    view = qkv_view.reshape(B, sd, qkv_view.shape[1])
    cur = lambda comp: pl.BlockSpec((None, qb, gw), lambda b, r, qi: (b, qi, r * nblk + comp))
    prev = lambda comp: pl.BlockSpec(
        (None, Q, gw), lambda b, r, qi: (b, jnp.maximum(qi * n_sub - 1, 0), r * nblk + comp))
    out = pl.pallas_call(
        _attn_kernel,
        grid=(B, dil, sd // qb),
        in_specs=[cur(0), prev(1), cur(1), prev(2), cur(2)],
        out_specs=pl.BlockSpec((None, qb, ATT_OUT_W), lambda b, r, qi: (b, qi, r)),
        out_shape=jax.ShapeDtypeStruct((B, sd, dil * ATT_OUT_W), F32),
        compiler_params=_params("parallel", "parallel", "arbitrary"),
        name=f"dilated_attn_g{gi}",
    )(view, view, view, view, view)
    return out.reshape(B * sd, dil * ATT_OUT_W)


def _attn_merge_kernel(o0_ref, o1_ref, o2_ref, w_ref, x_ref, g_ref, b_ref, o_ref, ot_ref, *scratch, alpha):
    dh = ATT_HEAD_DIM
    n_slab = ATT_OUT_W // LANES
    tm = x_ref.shape[0]
    groups = []
    scratch = list(scratch)
    for ref, (_, dil) in zip((o0_ref, o1_ref, o2_ref), DIL_PATTERN):
        if dil == 1:
            groups.append(lambda c, ref=ref: ref[:, c * LANES:(c + 1) * LANES])
            continue
        scr = scratch.pop(0)
        n = tm // dil
        for r in range(dil):
            for c in range(n_slab):
                col = r * ATT_OUT_W + c * LANES
                scr[c, pl.ds(r, n, stride=dil), :] = ref[:, col:col + LANES]
        groups.append(lambda c, scr=scr: scr[c])
    lses = [grp(n_slab - 1) for grp in groups]
    mx = jnp.maximum(jnp.maximum(lses[0], lses[1]), lses[2])
    ws = [jnp.exp(l - mx) for l in lses]
    den = ws[0] + ws[1] + ws[2]
    ws = [w / den for w in ws]
    heads = []
    for h in range(HEADS_PER_GROUP):
        acc = ws[0][:, h:h + 1] * groups[0](h)
        acc += ws[1][:, h:h + 1] * groups[1](h)
        acc += ws[2][:, h:h + 1] * groups[2](h)
        heads.append(acc)
    merged = jnp.concatenate(heads, axis=-1).astype(BF16)
    y = jnp.dot(merged, w_ref[...], preferred_element_type=F32)
    out = _layer_norm(alpha * x_ref[...] + y, g_ref[...], b_ref[...])
    o_ref[...] = out
    _store_row_tiles(ot_ref, out)


def _attn_merge_res_ln(outs, w, x, g, b, alpha):
    M, D = x.shape
    K = w.shape[0]
    tm = min(LN_TM, M)
    dils = [dil for _, dil in DIL_PATTERN]
    assert M % tm == 0 and all(tm % (8 * dil) == 0 for dil in dils) and D == SUBLANES * LANES
    ospecs = [pl.BlockSpec((tm // dil, dil * ATT_OUT_W), lambda i: (i, 0)) for dil in dils]
    return pl.pallas_call(
        functools.partial(_attn_merge_kernel, alpha=alpha),
        grid=(M // tm,),
        in_specs=ospecs + [
                  pl.BlockSpec((K, D), lambda i: (0, 0)),
                  pl.BlockSpec((tm, D), lambda i: (i, 0)),
                  pl.BlockSpec((1, D), lambda i: (0, 0)),
                  pl.BlockSpec((1, D), lambda i: (0, 0))],
        out_specs=[pl.BlockSpec((tm, D), lambda i: (i, 0)),
                   pl.BlockSpec((tm * SUBLANES, LANES), lambda i: (i, 0))],
        out_shape=[jax.ShapeDtypeStruct((M, D), F32),
                   jax.ShapeDtypeStruct((M * SUBLANES, LANES), F32)],
        scratch_shapes=[pltpu.VMEM((ATT_OUT_W // LANES, tm, LANES), F32) for dil in dils if dil > 1],
        compiler_params=_params("parallel"),
        name="attn_merge_res_ln",
    )(*outs, w, x, g.reshape(1, D), b.reshape(1, D))


def _router_kernel(x_ref, wh_ref, wl_ref, slab_ref, cnt_ref, run_ref):
    tb = x_ref.shape[0]

    @pl.when(pl.program_id(0) == 0)
    def _():
        run_ref[...] = jnp.zeros_like(run_ref)

    x = x_ref[...]
    xh = x.astype(BF16)
    xl = (x - xh.astype(F32)).astype(BF16)
    logits = (jnp.dot(xh, wh_ref[...], preferred_element_type=F32)
              + jnp.dot(xh, wl_ref[...], preferred_element_type=F32)
              + jnp.dot(xl, wh_ref[...], preferred_element_type=F32))
    lane = lax.broadcasted_iota(jnp.int32, (tb, LANES), 1)
    logits = jnp.where(lane < N_EXPERTS, logits, NEG_BIG)
    m1 = jnp.max(logits, axis=-1, keepdims=True)
    i1 = jnp.min(jnp.where(logits == m1, lane, LANES), axis=-1, keepdims=True)
    rest = jnp.where(lane == i1, NEG_BIG, logits)
    m2 = jnp.max(rest, axis=-1, keepdims=True)
    i2 = jnp.min(jnp.where(rest == m2, lane, LANES), axis=-1, keepdims=True)
    e = jnp.exp(m2 - m1)
    g1 = 1.0 / (1.0 + e)
    g2 = e / (1.0 + e)
    onehot = jnp.where((lane == i1) | (lane == i2), 1.0, 0.0)
    r = lax.broadcasted_iota(jnp.int32, (tb, tb), 0)
    c = lax.broadcasted_iota(jnp.int32, (tb, tb), 1)
    lower = jnp.where(c < r, 1.0, 0.0).astype(BF16)
    before = jnp.dot(lower, onehot.astype(BF16), preferred_element_type=F32) + run_ref[...]
    rank1 = jnp.sum(jnp.where(lane == i1, before, 0.0), axis=-1, keepdims=True)
    rank2 = jnp.sum(jnp.where(lane == i2, before, 0.0), axis=-1, keepdims=True)
    total = run_ref[...] + jnp.sum(onehot, axis=0, keepdims=True)
    run_ref[...] = total
    cnt_ref[...] = total
    slab = jnp.where(lane == 0, i1.astype(F32), 0.0)
    slab = jnp.where(lane == 1, i2.astype(F32), slab)
    slab = jnp.where(lane == 2, g1, slab)
    slab = jnp.where(lane == 3, g2, slab)
    slab = jnp.where(lane == 4, rank1, slab)
    slab = jnp.where(lane == 5, rank2, slab)
    slab_ref[...] = slab


def _router(x, w_router):
    T, D = x.shape
    tb = min(ROUTE_TB, T)
    assert T % tb == 0
    w = jnp.zeros((D, LANES), F32).at[:, :N_EXPERTS].set(w_router)
    wh = w.astype(BF16)
    wl = (w - wh.astype(F32)).astype(BF16)
    return pl.pallas_call(
        _router_kernel,
        grid=(T // tb,),
        in_specs=[pl.BlockSpec((tb, D), lambda i: (i, 0)),
                  pl.BlockSpec((D, LANES), lambda i: (0, 0)),
                  pl.BlockSpec((D, LANES), lambda i: (0, 0))],
        out_specs=[pl.BlockSpec((tb, LANES), lambda i: (i, 0)),
                   pl.BlockSpec((1, LANES), lambda i: (0, 0))],
        out_shape=[jax.ShapeDtypeStruct((T, LANES), F32),
                   jax.ShapeDtypeStruct((1, LANES), F32)],
        scratch_shapes=[pltpu.VMEM((1, LANES), F32)],
        compiler_params=_params("arbitrary"),
        name="moe_router",
    )(x, wh, wl)


def _expert_kernel(be_ref, na_ref, tok_ref, tok_nxt_ref, dst_prv_ref, dst_ref, x_hbm,
                   wg_ref, wu_ref, wd_ref, y_hbm, xbuf_ref, xb_ref, acc_ref, ybuf_ref,
                   hid_ref, sem_in, sem_out, *, n_f, subs):
    del be_ref
    i, f = pl.program_id(0), pl.program_id(1)
    n_active = na_ref[0]
    tm = xb_ref.shape[0]
    per = tm // (n_f * len(subs))
    slot = lax.rem(i, 2)

    def tile(row):
        return pl.ds(pl.multiple_of(row * SUBLANES, SUBLANES), SUBLANES)

    def gather(tok, t, s):
        return pltpu.make_async_copy(x_hbm.at[tile(tok)], xbuf_ref.at[s, tile(t)], sem_in.at[s])

    def scatter(dst, t, s):
        return pltpu.make_async_copy(ybuf_ref.at[s, tile(t)], y_hbm.at[tile(dst)], sem_out.at[s])

    def gathered(s):
        return pltpu.make_async_copy(x_hbm.at[pl.ds(0, tm * SUBLANES)], xbuf_ref.at[s], sem_in.at[s])

    def scattered(s):
        return pltpu.make_async_copy(ybuf_ref.at[s], y_hbm.at[pl.ds(0, tm * SUBLANES)], sem_out.at[s])

    def for_rows(fn):
        def body(t, carry):
            fn(t)
            return carry
        lax.fori_loop(0, tm, body, 0, unroll=DMA_UNROLL)

    @pl.when(i < n_active)
    def _():
        @pl.when((i == 0) & (f == 0))
        def _():
            for_rows(lambda t: gather(tok_ref[0, 0, t], t, 0).start())
            ybuf_ref[1] = jnp.zeros(ybuf_ref.shape[1:], F32)

        @pl.when(f == 0)
        def _():
            gathered(slot).wait()
            xb_ref[...] = _load_row_tiles(xbuf_ref.at[slot], tm).astype(BF16)
            acc_ref[...] = jnp.zeros_like(acc_ref)

        xb = xb_ref[...]
        off = 0
        for c, width in enumerate(subs):
            base = (f * len(subs) + c) * per
            for t in range(per):
                gather(tok_nxt_ref[0, 0, base + t], base + t, 1 - slot).start()
                scatter(dst_prv_ref[0, 0, base + t], base + t, 1 - slot).start()
            cs = slice(off, off + width)
            gate = jnp.dot(xb, wg_ref[:, cs], preferred_element_type=F32)
            up = jnp.dot(xb, wu_ref[:, cs], preferred_element_type=F32)
            hid_ref[:, cs] = (gate * jax.nn.sigmoid(gate) * up).astype(BF16)
            off += width
        acc_ref[...] += jnp.dot(hid_ref[...], wd_ref[...], preferred_element_type=F32)

        @pl.when(f == n_f - 1)
        def _():
            @pl.when(i > 0)
            def _():
                scattered(slot).wait()
            _store_row_tiles(ybuf_ref.at[slot], acc_ref[...])

        @pl.when((i == n_active - 1) & (f == n_f - 1))
        def _():
            for_rows(lambda t: scatter(dst_ref[0, 0, t], t, slot).start())
            gathered(1 - slot).wait()
            scattered(1 - slot).wait()
            scattered(slot).wait()

    @pl.when((i >= n_active) & (f == n_f - 1))
    def _():
        ybuf_ref[0] = jnp.zeros(ybuf_ref.shape[1:], F32)
        for_rows(lambda t: scatter(dst_ref[0, 0, t], t, 0).start())
        scattered(0).wait()


def _experts(x, slot_tok, slot_dst, n_rows_out, blk_exp, n_active, wg, wu, wd, layer):
    D = x.shape[1] * SUBLANES
    P = slot_tok.shape[0]
    F = wg.shape[3]
    tm, tf = MOE_BLOCK, EXP_TF
    subs = (EXP_SUB,) * (tf // EXP_SUB) + ((tf % EXP_SUB,) if tf % EXP_SUB else ())
    assert P % tm == 0 and F % tf == 0 and tm % ((F // tf) * len(subs)) == 0
    nb, nf = P // tm, F // tf
    toks = slot_tok.reshape(nb, 1, tm)
    spare = n_rows_out - tm + jnp.arange(tm, dtype=jnp.int32)
    dsts = jnp.concatenate([spare, slot_dst]).reshape(nb + 1, 1, tm)
    smem = lambda fn: pl.BlockSpec((1, 1, tm), fn, memory_space=pltpu.SMEM)
    grid_spec = pltpu.PrefetchScalarGridSpec(
        num_scalar_prefetch=2,
        grid=(nb, nf),
        in_specs=[smem(lambda i, f, be, na: (i, 0, 0)),
                  smem(lambda i, f, be, na: (jnp.minimum(i + 1, nb - 1), 0, 0)),
                  smem(lambda i, f, be, na: (i, 0, 0)),
                  smem(lambda i, f, be, na: (i + 1, 0, 0)),
                  pl.BlockSpec(memory_space=pl.ANY),
                  pl.BlockSpec((None, None, D, tf), lambda i, f, be, na: (layer, be[i], 0, f)),
                  pl.BlockSpec((None, None, D, tf), lambda i, f, be, na: (layer, be[i], 0, f)),
                  pl.BlockSpec((None, None, tf, D), lambda i, f, be, na: (layer, be[i], f, 0))],
        out_specs=pl.BlockSpec(memory_space=pl.ANY),
        scratch_shapes=[pltpu.VMEM((2, tm * SUBLANES, LANES), F32), pltpu.VMEM((tm, D), BF16),
                        pltpu.VMEM((tm, D), F32), pltpu.VMEM((2, tm * SUBLANES, LANES), F32),
                        pltpu.VMEM((tm, tf), BF16),
                        pltpu.SemaphoreType.DMA((2,)), pltpu.SemaphoreType.DMA((2,))],
    )
    return pl.pallas_call(
        functools.partial(_expert_kernel, n_f=nf, subs=subs),
        grid_spec=grid_spec,
        out_shape=jax.ShapeDtypeStruct((n_rows_out * SUBLANES, LANES), F32),
        compiler_params=_params("arbitrary", "arbitrary"),
        name="moe_experts",
    )(blk_exp, n_active, toks, toks, dsts, dsts, x, wg, wu, wd)


def _combine_kernel(y1_ref, y2_ref, slab_ref, x_ref, g_ref, b_ref, o_ref, *, alpha):
    slab = slab_ref[...]
    tb = x_ref.shape[0]
    mix = slab[:, 2:3] * _load_row_tiles(y1_ref, tb) + slab[:, 3:4] * _load_row_tiles(y2_ref, tb)
    o_ref[...] = _layer_norm(alpha * x_ref[...] + mix, g_ref[...], b_ref[...])


def _combine_res_ln(y, slab, x, g, b, alpha):
    T, D = x.shape
    tb = min(COMB_TB, T)
    assert T % tb == 0
    nb = T // tb
    return pl.pallas_call(
        functools.partial(_combine_kernel, alpha=alpha),
        grid=(nb,),
        in_specs=[pl.BlockSpec((tb * SUBLANES, LANES), lambda i: (i, 0)),
                  pl.BlockSpec((tb * SUBLANES, LANES), lambda i: (i + nb, 0)),
                  pl.BlockSpec((tb, LANES), lambda i: (i, 0)),
                  pl.BlockSpec((tb, D), lambda i: (i, 0)),
                  pl.BlockSpec((1, D), lambda i: (0, 0)),
                  pl.BlockSpec((1, D), lambda i: (0, 0))],
        out_specs=pl.BlockSpec((tb, D), lambda i: (i, 0)),
        out_shape=jax.ShapeDtypeStruct((T, D), F32),
        compiler_params=_params("parallel"),
        name="moe_combine_res_ln",
    )(y, y, slab, x, g.reshape(1, D), b.reshape(1, D))


def _moe_res_ln(x, x_tiles, w_router, wg, wu, wd, layer, g, b, alpha):
    T, D = x.shape
    slab, counts = _router(x, w_router)
    counts = counts[0, :N_EXPERTS].astype(jnp.int32)
    padded = ((counts + MOE_BLOCK - 1) // MOE_BLOCK) * MOE_BLOCK
    pad_end = jnp.cumsum(padded)
    pad_start = pad_end - padded
    e1 = slab[:, 0].astype(jnp.int32)
    e2 = slab[:, 1].astype(jnp.int32)
    d1 = pad_start[e1] + slab[:, 4].astype(jnp.int32)
    d2 = pad_start[e2] + slab[:, 5].astype(jnp.int32)
    nblk = -(-(2 * T) // MOE_BLOCK) + N_EXPERTS
    n_slots = nblk * MOE_BLOCK
    blk_start = jnp.arange(nblk, dtype=jnp.int32) * MOE_BLOCK
    blk_exp = jnp.minimum(jnp.sum(pad_end[None, :] <= blk_start[:, None], axis=1), N_EXPERTS - 1)
    n_active = (pad_end[-1:] // MOE_BLOCK).astype(jnp.int32)
    slot = jnp.arange(n_slots, dtype=jnp.int32)
    slot_exp = jnp.sum(pad_end[None, :] <= slot[:, None], axis=1)
    first_tok = jnp.concatenate([jnp.cumsum(counts) - counts, jnp.array([2 * T], jnp.int32)])
    first_slot = jnp.concatenate([pad_start, pad_end[-1:]])
    n_tok = jnp.concatenate([counts, jnp.array([0], jnp.int32)])
    filled_before = first_tok[slot_exp] + jnp.minimum(slot - first_slot[slot_exp], n_tok[slot_exp])
    tok = jnp.arange(T, dtype=jnp.int32)
    slot_dst = (2 * T + slot - filled_before).at[jnp.concatenate([d1, d2])].set(
        jnp.concatenate([tok, T + tok]), unique_indices=True)
    slot_tok = jnp.where(slot_dst < 2 * T, slot_dst % T, 0)
    n_rows_out = n_slots + MOE_BLOCK
    y = _experts(x_tiles, slot_tok, slot_dst, n_rows_out, blk_exp.astype(jnp.int32), n_active,
                 wg, wu, wd, layer)
    return _combine_res_ln(y, slab, x, g, b, alpha)


def kernel(x, ln_gain, ln_bias, ret_w_in, ret_gn_gain, ret_w_out, att_w_qkv, att_w_out,
           ffn_w_gate, ffn_w_up, ffn_w_down, moe_w_router, moe_w_gate, moe_w_up, moe_w_down):
    B, S, D = x.shape
    depth = ln_gain.shape[0]
    alpha = (2 * depth) ** 0.25
    moe_wg, moe_wu, moe_wd = (w.astype(BF16) for w in (moe_w_gate, moe_w_up, moe_w_down))
    h = x.reshape(B * S, D)
    h_blocks = None
    for i in range(depth):
        j = i // 2
        if i % 2 == 0:
            proj = _matmul(h, ret_w_in[j].astype(BF16), "ret_in_proj")
            gated = _retention_core(proj, B, S, ret_gn_gain[j])
            h = _mm_res_ln(gated, ret_w_out[j].astype(BF16), h, ln_gain[i, 0], ln_bias[i, 0], alpha,
                           "ret_out_res_ln")
            h, h_blocks = _ffn_res_ln(h, ffn_w_gate[j].astype(BF16), ffn_w_up[j].astype(BF16),
                                      ffn_w_down[j].astype(BF16), ln_gain[i, 1], ln_bias[i, 1], alpha)
        else:
            gw = HEADS_PER_GROUP * ATT_HEAD_DIM
            outs = []
            for gi, (win, dil) in enumerate(DIL_PATTERN):
                w_g = jnp.concatenate(
                    [att_w_qkv[j][:, (c * N_GROUPS + gi) * gw:(c * N_GROUPS + gi + 1) * gw] for c in range(3)],
                    axis=1).astype(BF16)
                if dil == 1:
                    qkv = _matmul(h, w_g, f"att_qkv_proj_g{gi}")
                else:
                    if h_blocks is None:
                        h_blocks = jnp.transpose(h.reshape(B * S, D // LANES, LANES), (1, 0, 2))
                    qkv = _matmul_residue_view(h_blocks, w_g, dil, f"att_qkv_proj_g{gi}")
                outs.append(_dilated_group(qkv, B, S, gi, win, dil))
            h_blocks = None
            h, h_tiles = _attn_merge_res_ln(outs, att_w_out[j].astype(BF16), h, ln_gain[i, 0], ln_bias[i, 0],
                                            alpha)
            h = _moe_res_ln(h, h_tiles, moe_w_router[j], moe_wg, moe_wu, moe_wd, j,
                            ln_gain[i, 1], ln_bias[i, 1], alpha)
    return h.reshape(B, S, D)
```

```python
import functools

import jax
import jax.numpy as jnp
from jax import lax
from jax.experimental import pallas as pl
from jax.experimental.pallas import tpu as pltpu

F32 = jnp.float32
BF16 = jnp.bfloat16

RET_HEADS = 4
RET_CHUNK = 128
ROPE_BASE = 10000.0
DIL_PATTERN = ((128, 1), (512, 4), (2048, 16))
N_GROUPS = len(DIL_PATTERN)
HEADS_PER_GROUP = 4
ATT_HEAD_DIM = 128
N_EXPERTS = 8
LN_EPS = 1e-5

LANES = 128
SUBLANES = 8
VMEM_LIMIT = 56 * 1024 * 1024

MM_TM = 1024
MM_TN = 2048
LN_TM = 512
FFN_TM = 1024
FFN_TF = 512
EXP_TF = 512
EXP_SUB = 512
RET_ROWS = 512
ATT_Q = 128
ATT_QB = 512
ATT_OUT_W = HEADS_PER_GROUP * ATT_HEAD_DIM + LANES
MOE_BLOCK = 896
ROUTE_TB = 512
COMB_TB = 512
DMA_UNROLL = 8
NEG_BIG = -1e30


def _params(*sem):
    return pltpu.CompilerParams(dimension_semantics=sem, vmem_limit_bytes=VMEM_LIMIT)


def _layer_norm(y, g, b):
    mu = jnp.mean(y, axis=-1, keepdims=True)
    d = y - mu
    var = jnp.mean(d * d, axis=-1, keepdims=True)
    return d * lax.rsqrt(var + LN_EPS) * g + b


def _store_row_tiles(ref, y):
    m = y.shape[0]
    for c in range(SUBLANES):
        ref[pl.ds(c, m, stride=SUBLANES), :] = y[:, c * LANES:(c + 1) * LANES]


def _load_row_tiles(ref, m):
    return jnp.concatenate([ref[pl.ds(c, m, stride=SUBLANES), :] for c in range(SUBLANES)], axis=-1)


def _mm_kernel(x_ref, w_ref, o_ref, xb_ref):
    @pl.when(pl.program_id(1) == 0)
    def _():
        xb_ref[...] = x_ref[...].astype(BF16)

    o_ref[...] = jnp.dot(xb_ref[...], w_ref[...], preferred_element_type=F32).astype(o_ref.dtype)


def _matmul(x, w, name):
    M, K = x.shape
    N = w.shape[1]
    tm = min(MM_TM, M)
    tn = max(t for t in range(LANES, MM_TN + 1, LANES) if N % t == 0)
    assert M % tm == 0
    return pl.pallas_call(
        _mm_kernel,
        grid=(M // tm, N // tn),
        in_specs=[pl.BlockSpec((tm, K), lambda i, j: (i, 0)),
                  pl.BlockSpec((K, tn), lambda i, j: (0, j))],
        out_specs=pl.BlockSpec((tm, tn), lambda i, j: (i, j)),
        out_shape=jax.ShapeDtypeStruct((M, N), BF16),
        scratch_shapes=[pltpu.VMEM((tm, K), BF16)],
        compiler_params=_params("parallel", "arbitrary"),
        name=name,
    )(x, w)


def _ret_kernel(q_ref, k_ref, v_ref, g_ref, cos_ref, sin_ref, dec_ref, qdec_ref, kdec_ref,
                cdec_ref, gain_ref, o_ref, state_ref, *, n_chunks, dk):
    C, H = RET_CHUNK, RET_HEADS
    half = dk // 2
    dv = 2 * dk

    @pl.when(pl.program_id(1) == 0)
    def _():
        state_ref[...] = jnp.zeros_like(state_ref)

    def rot(t, cos, sin):
        t1, t2 = t[:, :half], t[:, half:]
        return jnp.concatenate([t1 * cos - t2 * sin, t2 * cos + t1 * sin], axis=-1)

    def chunk(c, carry):
        rows = pl.ds(pl.multiple_of(c * C, C), C)
        cos, sin = cos_ref[rows, :], sin_ref[rows, :]
        for h in range(H):
            kc = slice(h * dk, (h + 1) * dk)
            vc = slice(h * dv, (h + 1) * dv)
            q = rot(q_ref[rows, kc].astype(F32), cos, sin)
            k = rot(k_ref[rows, kc].astype(F32), cos, sin) * (dk ** -0.5)
            v = v_ref[rows, vc]
            scores = lax.dot_general(q.astype(BF16), k.astype(BF16), (((1,), (1,)), ((), ())),
                                     preferred_element_type=F32) * dec_ref[h]
            intra = jnp.dot(scores.astype(BF16), v, preferred_element_type=F32)
            state = state_ref[h]
            cross = jnp.dot((q * qdec_ref[h]).astype(BF16), state.astype(BF16),
                            preferred_element_type=F32)
            kn_t = (k * kdec_ref[h]).T.astype(BF16)
            state_ref[h] = cdec_ref[h] * state + jnp.dot(kn_t, v, preferred_element_type=F32)
            r = intra + cross
            mu = jnp.mean(r, axis=-1, keepdims=True)
            d = r - mu
            var = jnp.mean(d * d, axis=-1, keepdims=True)
            normed = d * lax.rsqrt(var + LN_EPS) * gain_ref[:, vc]
            gate = g_ref[rows, vc].astype(F32)
            o_ref[rows, vc] = (gate * jax.nn.sigmoid(gate) * normed).astype(o_ref.dtype)
        return carry

    lax.fori_loop(0, n_chunks, chunk, 0)


def _retention_core(proj, B, S, gn_gain):
    H, C = RET_HEADS, RET_CHUNK
    cols = proj.shape[1]
    dk = cols // (6 * H)
    dv = 2 * dk
    rb = min(RET_ROWS, S)
    assert S % rb == 0 and rb % C == 0
    nr = S // rb

    half = dk // 2
    inv = ROPE_BASE ** (-jnp.arange(half, dtype=F32) / half)
    ang = jnp.arange(S).astype(F32)[:, None] * inv[None, :]
    cos, sin = jnp.cos(ang), jnp.sin(ang)
    log_gamma = jnp.log(1.0 - 2.0 ** (-5.0 - jnp.arange(H, dtype=F32)))
    idx = jnp.arange(C, dtype=F32)
    rel = idx[:, None] - idx[None, :]
    decay_intra = jnp.where(rel >= 0, jnp.exp(log_gamma[:, None, None] * jnp.maximum(rel, 0.0)), 0.0)
    q_dec = jnp.exp(log_gamma[None, :] * (idx[:, None] + 1.0))
    k_dec = jnp.exp(log_gamma[None, :] * (C - 1.0 - idx[:, None]))
    chunk_dec = jnp.exp(log_gamma * C)
    qdec_b = jnp.broadcast_to(q_dec.T[:, :, None], (H, C, dk))
    kdec_b = jnp.broadcast_to(k_dec.T[:, :, None], (H, C, dk))
    cdec_b = jnp.broadcast_to(chunk_dec[:, None, None], (H, 1, dv))
    gain = gn_gain.reshape(1, H * dv)

    full = lambda shape: pl.BlockSpec(shape, lambda b, i: (0,) * len(shape))
    kern = functools.partial(_ret_kernel, n_chunks=rb // C, dk=dk)
    return pl.pallas_call(
        kern,
        grid=(B, nr),
        in_specs=[
            pl.BlockSpec((rb, H * dk), lambda b, i: (b * nr + i, 0)),
            pl.BlockSpec((rb, H * dk), lambda b, i: (b * nr + i, 1)),
            pl.BlockSpec((rb, H * dv), lambda b, i: (b * nr + i, 1)),
            pl.BlockSpec((rb, H * dv), lambda b, i: (b * nr + i, 2)),
            pl.BlockSpec((rb, half), lambda b, i: (i, 0)),
            pl.BlockSpec((rb, half), lambda b, i: (i, 0)),
            full((H, C, C)),
            full((H, C, dk)),
            full((H, C, dk)),
            full((H, 1, dv)),
            full((1, H * dv)),
        ],
        out_specs=pl.BlockSpec((rb, H * dv), lambda b, i: (b * nr + i, 0)),
        out_shape=jax.ShapeDtypeStruct((B * S, H * dv), BF16),
        scratch_shapes=[pltpu.VMEM((H, dk, dv), F32)],
        compiler_params=_params("parallel", "arbitrary"),
        name="retention_core",
    )(proj, proj, proj, proj, cos, sin, decay_intra, qdec_b, kdec_b, cdec_b, gain)


def _mm_res_ln_kernel(a_ref, w_ref, x_ref, g_ref, b_ref, o_ref, *, alpha):
    y = jnp.dot(a_ref[...], w_ref[...], preferred_element_type=F32)
    o_ref[...] = _layer_norm(alpha * x_ref[...] + y, g_ref[...], b_ref[...])


def _mm_res_ln(a, w, x, g, b, alpha, name):
    M, K = a.shape
    D = w.shape[1]
    tm = min(LN_TM, M)
    assert M % tm == 0
    return pl.pallas_call(
        functools.partial(_mm_res_ln_kernel, alpha=alpha),
        grid=(M // tm,),
        in_specs=[pl.BlockSpec((tm, K), lambda i: (i, 0)),
                  pl.BlockSpec((K, D), lambda i: (0, 0)),
                  pl.BlockSpec((tm, D), lambda i: (i, 0)),
                  pl.BlockSpec((1, D), lambda i: (0, 0)),
                  pl.BlockSpec((1, D), lambda i: (0, 0))],
        out_specs=pl.BlockSpec((tm, D), lambda i: (i, 0)),
        out_shape=jax.ShapeDtypeStruct((M, D), F32),
        compiler_params=_params("parallel"),
        name=name,
    )(a, w, x, g.reshape(1, D), b.reshape(1, D))


def _ffn_kernel(x_ref, wg_ref, wu_ref, wd_ref, g_ref, b_ref, o_ref, oc_ref, xb_ref, acc_ref, *, alpha):
    f = pl.program_id(1)

    @pl.when(f == 0)
    def _():
        xb_ref[...] = x_ref[...].astype(BF16)
        acc_ref[...] = jnp.zeros_like(acc_ref)

    xb = xb_ref[...]
    gate = jnp.dot(xb, wg_ref[...], preferred_element_type=F32)
    up = jnp.dot(xb, wu_ref[...], preferred_element_type=F32)
    hid = (gate * jax.nn.sigmoid(gate) * up).astype(BF16)
    acc_ref[...] += jnp.dot(hid, wd_ref[...], preferred_element_type=F32)

    @pl.when(f == pl.num_programs(1) - 1)
    def _():
        y = _layer_norm(alpha * x_ref[...] + acc_ref[...], g_ref[...], b_ref[...])
        o_ref[...] = y
        for c in range(oc_ref.shape[0]):
            oc_ref[c] = y[:, c * LANES:(c + 1) * LANES]


def _ffn_res_ln(x, wg, wu, wd, g, b, alpha):
    M, D = x.shape
    F = wg.shape[1]
    tm, tf = min(FFN_TM, M), FFN_TF
    assert M % tm == 0 and F % tf == 0 and D % LANES == 0
    nc = D // LANES
    return pl.pallas_call(
        functools.partial(_ffn_kernel, alpha=alpha),
        grid=(M // tm, F // tf),
        in_specs=[pl.BlockSpec((tm, D), lambda i, f: (i, 0)),
                  pl.BlockSpec((D, tf), lambda i, f: (0, f)),
                  pl.BlockSpec((D, tf), lambda i, f: (0, f)),
                  pl.BlockSpec((tf, D), lambda i, f: (f, 0)),
                  pl.BlockSpec((1, D), lambda i, f: (0, 0)),
                  pl.BlockSpec((1, D), lambda i, f: (0, 0))],
        out_specs=[pl.BlockSpec((tm, D), lambda i, f: (i, 0)),
                   pl.BlockSpec((nc, tm, LANES), lambda i, f: (0, i, 0))],
        out_shape=[jax.ShapeDtypeStruct((M, D), F32),
                   jax.ShapeDtypeStruct((nc, M, LANES), F32)],
        scratch_shapes=[pltpu.VMEM((tm, D), BF16), pltpu.VMEM((tm, D), F32)],
        compiler_params=_params("parallel", "arbitrary"),
        name="ffn_res_ln",
    )(x, wg, wu, wd, g.reshape(1, D), b.reshape(1, D))


def _mm_residue_kernel(xc_ref, w_ref, o_ref, lhs_ref, *, dil):
    nc, tm, _ = xc_ref.shape
    n = tm // dil
    N = w_ref.shape[1]
    for r in range(dil):
        for c in range(nc):
            lhs_ref[r * n:(r + 1) * n, c * LANES:(c + 1) * LANES] = (
                xc_ref[c, pl.ds(r, n, stride=dil), :].astype(BF16))
    res = jnp.dot(lhs_ref[...], w_ref[...], preferred_element_type=F32).astype(o_ref.dtype)
    for r in range(dil):
        o_ref[:, r * N:(r + 1) * N] = res[r * n:(r + 1) * n, :]


def _matmul_residue_view(xc, w, dil, name):
    nc, M, _ = xc.shape
    K, N = w.shape
    tm = min(MM_TM, M)
    assert M % tm == 0 and tm % (16 * dil) == 0 and nc * LANES == K
    return pl.pallas_call(
        functools.partial(_mm_residue_kernel, dil=dil),
        grid=(M // tm,),
        in_specs=[pl.BlockSpec((nc, tm, LANES), lambda i: (0, i, 0)),
                  pl.BlockSpec((K, N), lambda i: (0, 0))],
        out_specs=pl.BlockSpec((tm // dil, dil * N), lambda i: (i, 0)),
        out_shape=jax.ShapeDtypeStruct((M // dil, dil * N), BF16),
        scratch_shapes=[pltpu.VMEM((tm, K), BF16)],
        compiler_params=_params("parallel"),
        name=name,
    )(xc, w)


def _attn_kernel(q_ref, kp_ref, kc_ref, vp_ref, vc_ref, o_ref):
    Q, dh = ATT_Q, ATT_HEAD_DIM
    n_sub = q_ref.shape[0] // Q
    qi = pl.program_id(2)
    row = lax.broadcasted_iota(jnp.int32, (Q, 2 * Q), 0)
    col = lax.broadcasted_iota(jnp.int32, (Q, 2 * Q), 1)
    band = (col >= row) & (col <= row + Q)
    first = jnp.where(qi > 0, 0, Q)
    lane = lax.broadcasted_iota(jnp.int32, (Q, LANES), 1)
    for j in range(n_sub):
        rs = slice(j * Q, (j + 1) * Q)
        ks = slice((j - 1) * Q, (j + 1) * Q)
        valid = (band & (col >= first)) if j == 0 else band
        lse_slab = jnp.zeros((Q, LANES), F32)
        for h in range(HEADS_PER_GROUP):
            cs = slice(h * dh, (h + 1) * dh)
            if j == 0:
                k = jnp.concatenate([kp_ref[:, cs], kc_ref[:Q, cs]], axis=0)
                v = jnp.concatenate([vp_ref[:, cs], vc_ref[:Q, cs]], axis=0)
            else:
                k, v = kc_ref[ks, cs], vc_ref[ks, cs]
            s = lax.dot_general(q_ref[rs, cs], k, (((1,), (1,)), ((), ())),
                                preferred_element_type=F32) * (dh ** -0.5)
            s = jnp.where(valid, s, NEG_BIG)
            m = jnp.max(s, axis=-1, keepdims=True)
            p = jnp.exp(s - m)
            l = jnp.sum(p, axis=-1, keepdims=True)
            o = jnp.dot(p.astype(BF16), v, preferred_element_type=F32)
            o_ref[rs, cs] = o / l
            lse_slab = jnp.where(lane == h, m + jnp.log(l), lse_slab)
        o_ref[rs, HEADS_PER_GROUP * dh:] = lse_slab


def _dilated_group(qkv_view, B, S, gi, win, dil):
    Q = ATT_Q
    gw = HEADS_PER_GROUP * ATT_HEAD_DIM
    nblk = qkv_view.shape[1] // (dil * gw)
    sd = S // dil
    qb = min(ATT_QB, sd)
    assert win // dil == Q and S % dil == 0 and sd % qb == 0 and qb % Q == 0
    n_sub = qb // Q
    view = qkv_view.reshape(B, sd, qkv_view.shape[1])
    cur = lambda comp: pl.BlockSpec((None, qb, gw), lambda b, r, qi: (b, qi, r * nblk + comp))
    prev = lambda comp: pl.BlockSpec(
        (None, Q, gw), lambda b, r, qi: (b, jnp.maximum(qi * n_sub - 1, 0), r * nblk + comp))
    out = pl.pallas_call(
        _attn_kernel,
        grid=(B, dil, sd // qb),
        in_specs=[cur(0), prev(1), cur(1), prev(2), cur(2)],
        out_specs=pl.BlockSpec((None, qb, ATT_OUT_W), lambda b, r, qi: (b, qi, r)),
        out_shape=jax.ShapeDtypeStruct((B, sd, dil * ATT_OUT_W), F32),
        compiler_params=_params("parallel", "parallel", "arbitrary"),
        name=f"dilated_attn_g{gi}",
    )(view, view, view, view, view)
    return out.reshape(B * sd, dil * ATT_OUT_W)


def _attn_merge_kernel(o0_ref, o1_ref, o2_ref, w_ref, x_ref, g_ref, b_ref, o_ref, ot_ref, *scratch, alpha):
    dh = ATT_HEAD_DIM
    n_slab = ATT_OUT_W // LANES
    tm = x_ref.shape[0]
    groups = []
    scratch = list(scratch)
    for ref, (_, dil) in zip((o0_ref, o1_ref, o2_ref), DIL_PATTERN):
        if dil == 1:
            groups.append(lambda c, ref=ref: ref[:, c * LANES:(c + 1) * LANES])
            continue
        scr = scratch.pop(0)
        n = tm // dil
        for r in range(dil):
            for c in range(n_slab):
                col = r * ATT_OUT_W + c * LANES
                scr[c, pl.ds(r, n, stride=dil), :] = ref[:, col:col + LANES]
        groups.append(lambda c, scr=scr: scr[c])
    lses = [grp(n_slab - 1) for grp in groups]
    mx = jnp.maximum(jnp.maximum(lses[0], lses[1]), lses[2])
    ws = [jnp.exp(l - mx) for l in lses]
    den = ws[0] + ws[1] + ws[2]
    ws = [w / den for w in ws]
    heads = []
    for h in range(HEADS_PER_GROUP):
        acc = ws[0][:, h:h + 1] * groups[0](h)
        acc += ws[1][:, h:h + 1] * groups[1](h)
        acc += ws[2][:, h:h + 1] * groups[2](h)
        heads.append(acc)
    merged = jnp.concatenate(heads, axis=-1).astype(BF16)
    y = jnp.dot(merged, w_ref[...], preferred_element_type=F32)
    out = _layer_norm(alpha * x_ref[...] + y, g_ref[...], b_ref[...])
    o_ref[...] = out
    _store_row_tiles(ot_ref, out)


def _attn_merge_res_ln(outs, w, x, g, b, alpha):
    M, D = x.shape
    K = w.shape[0]
    tm = min(LN_TM, M)
    dils = [dil for _, dil in DIL_PATTERN]
    assert M % tm == 0 and all(tm % (8 * dil) == 0 for dil in dils) and D == SUBLANES * LANES
    ospecs = [pl.BlockSpec((tm // dil, dil * ATT_OUT_W), lambda i: (i, 0)) for dil in dils]
    return pl.pallas_call(
        functools.partial(_attn_merge_kernel, alpha=alpha),
        grid=(M // tm,),
        in_specs=ospecs + [
                  pl.BlockSpec((K, D), lambda i: (0, 0)),
                  pl.BlockSpec((tm, D), lambda i: (i, 0)),
                  pl.BlockSpec((1, D), lambda i: (0, 0)),
                  pl.BlockSpec((1, D), lambda i: (0, 0))],
        out_specs=[pl.BlockSpec((tm, D), lambda i: (i, 0)),
                   pl.BlockSpec((tm * SUBLANES, LANES), lambda i: (i, 0))],
        out_shape=[jax.ShapeDtypeStruct((M, D), F32),
                   jax.ShapeDtypeStruct((M * SUBLANES, LANES), F32)],
        scratch_shapes=[pltpu.VMEM((ATT_OUT_W // LANES, tm, LANES), F32) for dil in dils if dil > 1],
        compiler_params=_params("parallel"),
        name="attn_merge_res_ln",
    )(*outs, w, x, g.reshape(1, D), b.reshape(1, D))


def _router_kernel(x_ref, wh_ref, wl_ref, slab_ref, cnt_ref, run_ref):
    tb = x_ref.shape[0]

    @pl.when(pl.program_id(0) == 0)
    def _():
        run_ref[...] = jnp.zeros_like(run_ref)

    x = x_ref[...]
    xh = x.astype(BF16)
    xl = (x - xh.astype(F32)).astype(BF16)
    logits = (jnp.dot(xh, wh_ref[...], preferred_element_type=F32)
              + jnp.dot(xh, wl_ref[...], preferred_element_type=F32)
              + jnp.dot(xl, wh_ref[...], preferred_element_type=F32))
    lane = lax.broadcasted_iota(jnp.int32, (tb, LANES), 1)
    logits = jnp.where(lane < N_EXPERTS, logits, NEG_BIG)
    m1 = jnp.max(logits, axis=-1, keepdims=True)
    i1 = jnp.min(jnp.where(logits == m1, lane, LANES), axis=-1, keepdims=True)
    rest = jnp.where(lane == i1, NEG_BIG, logits)
    m2 = jnp.max(rest, axis=-1, keepdims=True)
    i2 = jnp.min(jnp.where(rest == m2, lane, LANES), axis=-1, keepdims=True)
    e = jnp.exp(m2 - m1)
    g1 = 1.0 / (1.0 + e)
    g2 = e / (1.0 + e)
    onehot = jnp.where((lane == i1) | (lane == i2), 1.0, 0.0)
    r = lax.broadcasted_iota(jnp.int32, (tb, tb), 0)
    c = lax.broadcasted_iota(jnp.int32, (tb, tb), 1)
    lower = jnp.where(c < r, 1.0, 0.0).astype(BF16)
    before = jnp.dot(lower, onehot.astype(BF16), preferred_element_type=F32) + run_ref[...]
    rank1 = jnp.sum(jnp.where(lane == i1, before, 0.0), axis=-1, keepdims=True)
    rank2 = jnp.sum(jnp.where(lane == i2, before, 0.0), axis=-1, keepdims=True)
    total = run_ref[...] + jnp.sum(onehot, axis=0, keepdims=True)
    run_ref[...] = total
    cnt_ref[...] = total
    slab = jnp.where(lane == 0, i1.astype(F32), 0.0)
    slab = jnp.where(lane == 1, i2.astype(F32), slab)
    slab = jnp.where(lane == 2, g1, slab)
    slab = jnp.where(lane == 3, g2, slab)
    slab = jnp.where(lane == 4, rank1, slab)
    slab = jnp.where(lane == 5, rank2, slab)
    slab_ref[...] = slab


def _router(x, w_router):
    T, D = x.shape
    tb = min(ROUTE_TB, T)
    assert T % tb == 0
    w = jnp.zeros((D, LANES), F32).at[:, :N_EXPERTS].set(w_router)
    wh = w.astype(BF16)
    wl = (w - wh.astype(F32)).astype(BF16)
    return pl.pallas_call(
        _router_kernel,
        grid=(T // tb,),
        in_specs=[pl.BlockSpec((tb, D), lambda i: (i, 0)),
                  pl.BlockSpec((D, LANES), lambda i: (0, 0)),
                  pl.BlockSpec((D, LANES), lambda i: (0, 0))],
        out_specs=[pl.BlockSpec((tb, LANES), lambda i: (i, 0)),
                   pl.BlockSpec((1, LANES), lambda i: (0, 0))],
        out_shape=[jax.ShapeDtypeStruct((T, LANES), F32),
                   jax.ShapeDtypeStruct((1, LANES), F32)],
        scratch_shapes=[pltpu.VMEM((1, LANES), F32)],
        compiler_params=_params("arbitrary"),
        name="moe_router",
    )(x, wh, wl)


def _expert_kernel(be_ref, na_ref, tok_ref, tok_nxt_ref, tok_nx2_ref, dst_prv_ref, dst_ref, x_hbm,
                   wg_ref, wu_ref, wd_ref, y_hbm, xbuf_ref, xb_ref, acc_ref, ybuf_ref,
                   hid_ref, sem_in, sem_out, *, n_f, subs):
    del be_ref
    i, f = pl.program_id(0), pl.program_id(1)
    n_active = na_ref[0]
    tm = xb_ref.shape[0]
    per = tm // (n_f * len(subs))
    slot = lax.rem(i, 2)
    xin, xin1, xin2 = lax.rem(i, 3), lax.rem(i + 1, 3), lax.rem(i + 2, 3)

    def tile(row):
        return pl.ds(pl.multiple_of(row * SUBLANES, SUBLANES), SUBLANES)

    def gather(tok, t, s):
        return pltpu.make_async_copy(x_hbm.at[tile(tok)], xbuf_ref.at[s, tile(t)], sem_in.at[s])

    def scatter(dst, t, s):
        return pltpu.make_async_copy(ybuf_ref.at[s, tile(t)], y_hbm.at[tile(dst)], sem_out.at[s])

    def gathered(s):
        return pltpu.make_async_copy(x_hbm.at[pl.ds(0, tm * SUBLANES)], xbuf_ref.at[s], sem_in.at[s])

    def scattered(s):
        return pltpu.make_async_copy(ybuf_ref.at[s], y_hbm.at[pl.ds(0, tm * SUBLANES)], sem_out.at[s])

    def for_rows(fn):
        def body(t, carry):
            fn(t)
            return carry
        lax.fori_loop(0, tm, body, 0, unroll=DMA_UNROLL)

    @pl.when(i < n_active)
    def _():
        @pl.when((i == 0) & (f == 0))
        def _():
            for_rows(lambda t: gather(tok_ref[0, 0, t], t, 0).start())
            for_rows(lambda t: gather(tok_nxt_ref[0, 0, t], t, 1).start())
            ybuf_ref[1] = jnp.zeros(ybuf_ref.shape[1:], F32)

        @pl.when(f == 0)
        def _():
            gathered(xin).wait()
            xb_ref[...] = _load_row_tiles(xbuf_ref.at[xin], tm).astype(BF16)
            acc_ref[...] = jnp.zeros_like(acc_ref)

        xb = xb_ref[...]
        off = 0
        for c, width in enumerate(subs):
            base = (f * len(subs) + c) * per
            for t in range(per):
                gather(tok_nx2_ref[0, 0, base + t], base + t, xin2).start()
                scatter(dst_prv_ref[0, 0, base + t], base + t, 1 - slot).start()
            cs = slice(off, off + width)
            gate = jnp.dot(xb, wg_ref[:, cs], preferred_element_type=F32)
            up = jnp.dot(xb, wu_ref[:, cs], preferred_element_type=F32)
            hid_ref[:, cs] = (gate * jax.nn.sigmoid(gate) * up).astype(BF16)
            off += width
        acc_ref[...] += jnp.dot(hid_ref[...], wd_ref[...], preferred_element_type=F32)

        @pl.when(f == n_f - 1)
        def _():
            @pl.when(i > 0)
            def _():
                scattered(slot).wait()
            _store_row_tiles(ybuf_ref.at[slot], acc_ref[...])

        @pl.when((i == n_active - 1) & (f == n_f - 1))
        def _():
            for_rows(lambda t: scatter(dst_ref[0, 0, t], t, slot).start())
            gathered(xin1).wait()
            gathered(xin2).wait()
            scattered(1 - slot).wait()
            scattered(slot).wait()

    @pl.when((i >= n_active) & (f == n_f - 1))
    def _():
        ybuf_ref[0] = jnp.zeros(ybuf_ref.shape[1:], F32)
        for_rows(lambda t: scatter(dst_ref[0, 0, t], t, 0).start())
        scattered(0).wait()


def _experts(x, slot_tok, slot_dst, n_rows_out, blk_exp, n_active, wg, wu, wd, layer):
    D = x.shape[1] * SUBLANES
    P = slot_tok.shape[0]
    F = wg.shape[3]
    tm, tf = MOE_BLOCK, EXP_TF
    subs = (EXP_SUB,) * (tf // EXP_SUB) + ((tf % EXP_SUB,) if tf % EXP_SUB else ())
    assert P % tm == 0 and F % tf == 0 and tm % ((F // tf) * len(subs)) == 0
    nb, nf = P // tm, F // tf
    toks = slot_tok.reshape(nb, 1, tm)
    spare = n_rows_out - tm + jnp.arange(tm, dtype=jnp.int32)
    dsts = jnp.concatenate([spare, slot_dst]).reshape(nb + 1, 1, tm)
    smem = lambda fn: pl.BlockSpec((1, 1, tm), fn, memory_space=pltpu.SMEM)
    grid_spec = pltpu.PrefetchScalarGridSpec(
        num_scalar_prefetch=2,
        grid=(nb, nf),
        in_specs=[smem(lambda i, f, be, na: (i, 0, 0)),
                  smem(lambda i, f, be, na: (jnp.minimum(i + 1, nb - 1), 0, 0)),
                  smem(lambda i, f, be, na: (jnp.minimum(i + 2, nb - 1), 0, 0)),
                  smem(lambda i, f, be, na: (i, 0, 0)),
                  smem(lambda i, f, be, na: (i + 1, 0, 0)),
                  pl.BlockSpec(memory_space=pl.ANY),
                  pl.BlockSpec((None, None, D, tf), lambda i, f, be, na: (layer, be[i], 0, f)),
                  pl.BlockSpec((None, None, D, tf), lambda i, f, be, na: (layer, be[i], 0, f)),
                  pl.BlockSpec((None, None, tf, D), lambda i, f, be, na: (layer, be[i], f, 0))],
        out_specs=pl.BlockSpec(memory_space=pl.ANY),
        scratch_shapes=[pltpu.VMEM((3, tm * SUBLANES, LANES), F32), pltpu.VMEM((tm, D), BF16),
                        pltpu.VMEM((tm, D), F32), pltpu.VMEM((2, tm * SUBLANES, LANES), F32),
                        pltpu.VMEM((tm, tf), BF16),
                        pltpu.SemaphoreType.DMA((3,)), pltpu.SemaphoreType.DMA((2,))],
    )
    return pl.pallas_call(
        functools.partial(_expert_kernel, n_f=nf, subs=subs),
        grid_spec=grid_spec,
        out_shape=jax.ShapeDtypeStruct((n_rows_out * SUBLANES, LANES), F32),
        compiler_params=_params("arbitrary", "arbitrary"),
        name="moe_experts",
    )(blk_exp, n_active, toks, toks, toks, dsts, dsts, x, wg, wu, wd)


def _combine_kernel(y1_ref, y2_ref, slab_ref, x_ref, g_ref, b_ref, o_ref, *, alpha):
    slab = slab_ref[...]
    tb = x_ref.shape[0]
    mix = slab[:, 2:3] * _load_row_tiles(y1_ref, tb) + slab[:, 3:4] * _load_row_tiles(y2_ref, tb)
    o_ref[...] = _layer_norm(alpha * x_ref[...] + mix, g_ref[...], b_ref[...])


def _combine_res_ln(y, slab, x, g, b, alpha):
    T, D = x.shape
    tb = min(COMB_TB, T)
    assert T % tb == 0
    nb = T // tb
    return pl.pallas_call(
        functools.partial(_combine_kernel, alpha=alpha),
        grid=(nb,),
        in_specs=[pl.BlockSpec((tb * SUBLANES, LANES), lambda i: (i, 0)),
                  pl.BlockSpec((tb * SUBLANES, LANES), lambda i: (i + nb, 0)),
                  pl.BlockSpec((tb, LANES), lambda i: (i, 0)),
                  pl.BlockSpec((tb, D), lambda i: (i, 0)),
                  pl.BlockSpec((1, D), lambda i: (0, 0)),
                  pl.BlockSpec((1, D), lambda i: (0, 0))],
        out_specs=pl.BlockSpec((tb, D), lambda i: (i, 0)),
        out_shape=jax.ShapeDtypeStruct((T, D), F32),
        compiler_params=_params("parallel"),
        name="moe_combine_res_ln",
    )(y, y, slab, x, g.reshape(1, D), b.reshape(1, D))


def _moe_res_ln(x, x_tiles, w_router, wg, wu, wd, layer, g, b, alpha):
    T, D = x.shape
    slab, counts = _router(x, w_router)
    counts = counts[0, :N_EXPERTS].astype(jnp.int32)
    padded = ((counts + MOE_BLOCK - 1) // MOE_BLOCK) * MOE_BLOCK
    pad_end = jnp.cumsum(padded)
    pad_start = pad_end - padded
    e1 = slab[:, 0].astype(jnp.int32)
    e2 = slab[:, 1].astype(jnp.int32)
    d1 = pad_start[e1] + slab[:, 4].astype(jnp.int32)
    d2 = pad_start[e2] + slab[:, 5].astype(jnp.int32)
    nblk = -(-(2 * T) // MOE_BLOCK) + N_EXPERTS
    n_slots = nblk * MOE_BLOCK
    blk_start = jnp.arange(nblk, dtype=jnp.int32) * MOE_BLOCK
    blk_exp = jnp.minimum(jnp.sum(pad_end[None, :] <= blk_start[:, None], axis=1), N_EXPERTS - 1)
    n_active = (pad_end[-1:] // MOE_BLOCK).astype(jnp.int32)
    slot = jnp.arange(n_slots, dtype=jnp.int32)
    slot_exp = jnp.sum(pad_end[None, :] <= slot[:, None], axis=1)
    first_tok = jnp.concatenate([jnp.cumsum(counts) - counts, jnp.array([2 * T], jnp.int32)])
    first_slot = jnp.concatenate([pad_start, pad_end[-1:]])
    n_tok = jnp.concatenate([counts, jnp.array([0], jnp.int32)])
    filled_before = first_tok[slot_exp] + jnp.minimum(slot - first_slot[slot_exp], n_tok[slot_exp])
    tok = jnp.arange(T, dtype=jnp.int32)
    slot_dst = (2 * T + slot - filled_before).at[jnp.concatenate([d1, d2])].set(
        jnp.concatenate([tok, T + tok]), unique_indices=True)
    slot_tok = jnp.where(slot_dst < 2 * T, slot_dst % T, 0)
    n_rows_out = n_slots + MOE_BLOCK
    y = _experts(x_tiles, slot_tok, slot_dst, n_rows_out, blk_exp.astype(jnp.int32), n_active,
                 wg, wu, wd, layer)
    return _combine_res_ln(y, slab, x, g, b, alpha)


def kernel(x, ln_gain, ln_bias, ret_w_in, ret_gn_gain, ret_w_out, att_w_qkv, att_w_out,
           ffn_w_gate, ffn_w_up, ffn_w_down, moe_w_router, moe_w_gate, moe_w_up, moe_w_down):
    B, S, D = x.shape
    depth = ln_gain.shape[0]
    alpha = (2 * depth) ** 0.25
    moe_wg, moe_wu, moe_wd = (w.astype(BF16) for w in (moe_w_gate, moe_w_up, moe_w_down))
    h = x.reshape(B * S, D)
    h_blocks = None
    for i in range(depth):
        j = i // 2
        if i % 2 == 0:
            proj = _matmul(h, ret_w_in[j].astype(BF16), "ret_in_proj")
            gated = _retention_core(proj, B, S, ret_gn_gain[j])
            h = _mm_res_ln(gated, ret_w_out[j].astype(BF16), h, ln_gain[i, 0], ln_bias[i, 0], alpha,
                           "ret_out_res_ln")
            h, h_blocks = _ffn_res_ln(h, ffn_w_gate[j].astype(BF16), ffn_w_up[j].astype(BF16),
                                      ffn_w_down[j].astype(BF16), ln_gain[i, 1], ln_bias[i, 1], alpha)
        else:
            gw = HEADS_PER_GROUP * ATT_HEAD_DIM
            outs = []
            for gi, (win, dil) in enumerate(DIL_PATTERN):
                w_g = jnp.concatenate(
                    [att_w_qkv[j][:, (c * N_GROUPS + gi) * gw:(c * N_GROUPS + gi + 1) * gw] for c in range(3)],
                    axis=1).astype(BF16)
                if dil == 1:
                    qkv = _matmul(h, w_g, f"att_qkv_proj_g{gi}")
                else:
                    if h_blocks is None:
                        h_blocks = jnp.transpose(h.reshape(B * S, D // LANES, LANES), (1, 0, 2))
                    qkv = _matmul_residue_view(h_blocks, w_g, dil, f"att_qkv_proj_g{gi}")
                outs.append(_dilated_group(qkv, B, S, gi, win, dil))
            h_blocks = None
            h, h_tiles = _attn_merge_res_ln(outs, att_w_out[j].astype(BF16), h, ln_gain[i, 0], ln_bias[i, 0],
                                            alpha)
            h = _moe_res_ln(h, h_tiles, moe_w_router[j], moe_wg, moe_wu, moe_wd, j,
                            ln_gain[i, 1], ln_bias[i, 1], alpha)
    return h.reshape(B, S, D)
```

```python
import functools

import jax
import jax.numpy as jnp
from jax import lax
from jax.experimental import pallas as pl
from jax.experimental.pallas import tpu as pltpu

F32 = jnp.float32
BF16 = jnp.bfloat16

RET_HEADS = 4
RET_CHUNK = 128
ROPE_BASE = 10000.0
DIL_PATTERN = ((128, 1), (512, 4), (2048, 16))
N_GROUPS = len(DIL_PATTERN)
HEADS_PER_GROUP = 4
ATT_HEAD_DIM = 128
N_EXPERTS = 8
LN_EPS = 1e-5

LANES = 128
SUBLANES = 8
SMEM_1D_TILE = 1024
VMEM_LIMIT = 56 * 1024 * 1024

MM_TM = 1024
MM_TN = 2048
LN_TM = 512
FFN_TM = 1024
FFN_TF = 512
EXP_TF = 512
EXP_SUB = 512
RET_ROWS = 512
ATT_Q = 128
ATT_QB = 512
ATT_OUT_W = HEADS_PER_GROUP * ATT_HEAD_DIM + LANES
MOE_BLOCK = 896
ROUTE_TB = 512
COMB_TB = 512
DMA_UNROLL = 8
NEG_BIG = -1e30


def _params(*sem):
    return pltpu.CompilerParams(dimension_semantics=sem, vmem_limit_bytes=VMEM_LIMIT)


def _layer_norm(y, g, b):
    mu = jnp.mean(y, axis=-1, keepdims=True)
    d = y - mu
    var = jnp.mean(d * d, axis=-1, keepdims=True)
    return d * lax.rsqrt(var + LN_EPS) * g + b


def _store_row_tiles(ref, y):
    m = y.shape[0]
    for c in range(SUBLANES):
        ref[pl.ds(c, m, stride=SUBLANES), :] = y[:, c * LANES:(c + 1) * LANES]


def _load_row_tiles(ref, m):
    return jnp.concatenate([ref[pl.ds(c, m, stride=SUBLANES), :] for c in range(SUBLANES)], axis=-1)


def _mm_kernel(x_ref, w_ref, o_ref, xb_ref):
    @pl.when(pl.program_id(1) == 0)
    def _():
        xb_ref[...] = x_ref[...].astype(BF16)

    o_ref[...] = jnp.dot(xb_ref[...], w_ref[...], preferred_element_type=F32).astype(o_ref.dtype)


def _matmul(x, w, name):
    M, K = x.shape
    N = w.shape[1]
    tm = min(MM_TM, M)
    tn = max(t for t in range(LANES, MM_TN + 1, LANES) if N % t == 0)
    assert M % tm == 0
    return pl.pallas_call(
        _mm_kernel,
        grid=(M // tm, N // tn),
        in_specs=[pl.BlockSpec((tm, K), lambda i, j: (i, 0)),
                  pl.BlockSpec((K, tn), lambda i, j: (0, j))],
        out_specs=pl.BlockSpec((tm, tn), lambda i, j: (i, j)),
        out_shape=jax.ShapeDtypeStruct((M, N), BF16),
        scratch_shapes=[pltpu.VMEM((tm, K), BF16)],
        compiler_params=_params("parallel", "arbitrary"),
        name=name,
    )(x, w)


def _ret_kernel(q_ref, k_ref, v_ref, g_ref, cos_ref, sin_ref, dec_ref, qdec_ref, kdec_ref,
                cdec_ref, gain_ref, o_ref, state_ref, *, n_chunks, dk):
    C, H = RET_CHUNK, RET_HEADS
    half = dk // 2
    dv = 2 * dk

    @pl.when(pl.program_id(1) == 0)
    def _():
        state_ref[...] = jnp.zeros_like(state_ref)

    def rot(t, cos, sin):
        t1, t2 = t[:, :half], t[:, half:]
        return jnp.concatenate([t1 * cos - t2 * sin, t2 * cos + t1 * sin], axis=-1)

    def chunk(c, carry):
        rows = pl.ds(pl.multiple_of(c * C, C), C)
        cos, sin = cos_ref[rows, :], sin_ref[rows, :]
        for h in range(H):
            kc = slice(h * dk, (h + 1) * dk)
            vc = slice(h * dv, (h + 1) * dv)
            q = rot(q_ref[rows, kc].astype(F32), cos, sin)
            k = rot(k_ref[rows, kc].astype(F32), cos, sin) * (dk ** -0.5)
            v = v_ref[rows, vc]
            scores = lax.dot_general(q.astype(BF16), k.astype(BF16), (((1,), (1,)), ((), ())),
                                     preferred_element_type=F32) * dec_ref[h]
            intra = jnp.dot(scores.astype(BF16), v, preferred_element_type=F32)
            state = state_ref[h]
            cross = jnp.dot((q * qdec_ref[h]).astype(BF16), state.astype(BF16),
                            preferred_element_type=F32)
            kn_t = (k * kdec_ref[h]).T.astype(BF16)
            state_ref[h] = cdec_ref[h] * state + jnp.dot(kn_t, v, preferred_element_type=F32)
            r = intra + cross
            mu = jnp.mean(r, axis=-1, keepdims=True)
            d = r - mu
            var = jnp.mean(d * d, axis=-1, keepdims=True)
            normed = d * lax.rsqrt(var + LN_EPS) * gain_ref[:, vc]
            gate = g_ref[rows, vc].astype(F32)
            o_ref[rows, vc] = (gate * jax.nn.sigmoid(gate) * normed).astype(o_ref.dtype)
        return carry

    lax.fori_loop(0, n_chunks, chunk, 0)


def _retention_core(proj, B, S, gn_gain):
    H, C = RET_HEADS, RET_CHUNK
    cols = proj.shape[1]
    dk = cols // (6 * H)
    dv = 2 * dk
    rb = min(RET_ROWS, S)
    assert S % rb == 0 and rb % C == 0
    nr = S // rb

    half = dk // 2
    inv = ROPE_BASE ** (-jnp.arange(half, dtype=F32) / half)
    ang = jnp.arange(S).astype(F32)[:, None] * inv[None, :]
    cos, sin = jnp.cos(ang), jnp.sin(ang)
    log_gamma = jnp.log(1.0 - 2.0 ** (-5.0 - jnp.arange(H, dtype=F32)))
    idx = jnp.arange(C, dtype=F32)
    rel = idx[:, None] - idx[None, :]
    decay_intra = jnp.where(rel >= 0, jnp.exp(log_gamma[:, None, None] * jnp.maximum(rel, 0.0)), 0.0)
    q_dec = jnp.exp(log_gamma[None, :] * (idx[:, None] + 1.0))
    k_dec = jnp.exp(log_gamma[None, :] * (C - 1.0 - idx[:, None]))
    chunk_dec = jnp.exp(log_gamma * C)
    qdec_b = jnp.broadcast_to(q_dec.T[:, :, None], (H, C, dk))
    kdec_b = jnp.broadcast_to(k_dec.T[:, :, None], (H, C, dk))
    cdec_b = jnp.broadcast_to(chunk_dec[:, None, None], (H, 1, dv))
    gain = gn_gain.reshape(1, H * dv)

    full = lambda shape: pl.BlockSpec(shape, lambda b, i: (0,) * len(shape))
    kern = functools.partial(_ret_kernel, n_chunks=rb // C, dk=dk)
    return pl.pallas_call(
        kern,
        grid=(B, nr),
        in_specs=[
            pl.BlockSpec((rb, H * dk), lambda b, i: (b * nr + i, 0)),
            pl.BlockSpec((rb, H * dk), lambda b, i: (b * nr + i, 1)),
            pl.BlockSpec((rb, H * dv), lambda b, i: (b * nr + i, 1)),
            pl.BlockSpec((rb, H * dv), lambda b, i: (b * nr + i, 2)),
            pl.BlockSpec((rb, half), lambda b, i: (i, 0)),
            pl.BlockSpec((rb, half), lambda b, i: (i, 0)),
            full((H, C, C)),
            full((H, C, dk)),
            full((H, C, dk)),
            full((H, 1, dv)),
            full((1, H * dv)),
        ],
        out_specs=pl.BlockSpec((rb, H * dv), lambda b, i: (b * nr + i, 0)),
        out_shape=jax.ShapeDtypeStruct((B * S, H * dv), BF16),
        scratch_shapes=[pltpu.VMEM((H, dk, dv), F32)],
        compiler_params=_params("parallel", "arbitrary"),
        name="retention_core",
    )(proj, proj, proj, proj, cos, sin, decay_intra, qdec_b, kdec_b, cdec_b, gain)


def _mm_res_ln_kernel(a_ref, w_ref, x_ref, g_ref, b_ref, o_ref, *, alpha):
    y = jnp.dot(a_ref[...], w_ref[...], preferred_element_type=F32)
    o_ref[...] = _layer_norm(alpha * x_ref[...] + y, g_ref[...], b_ref[...])


def _mm_res_ln(a, w, x, g, b, alpha, name):
    M, K = a.shape
    D = w.shape[1]
    tm = min(LN_TM, M)
    assert M % tm == 0
    return pl.pallas_call(
        functools.partial(_mm_res_ln_kernel, alpha=alpha),
        grid=(M // tm,),
        in_specs=[pl.BlockSpec((tm, K), lambda i: (i, 0)),
                  pl.BlockSpec((K, D), lambda i: (0, 0)),
                  pl.BlockSpec((tm, D), lambda i: (i, 0)),
                  pl.BlockSpec((1, D), lambda i: (0, 0)),
                  pl.BlockSpec((1, D), lambda i: (0, 0))],
        out_specs=pl.BlockSpec((tm, D), lambda i: (i, 0)),
        out_shape=jax.ShapeDtypeStruct((M, D), F32),
        compiler_params=_params("parallel"),
        name=name,
    )(a, w, x, g.reshape(1, D), b.reshape(1, D))


def _ffn_kernel(x_ref, wg_ref, wu_ref, wd_ref, g_ref, b_ref, o_ref, oc_ref, xb_ref, acc_ref, *, alpha):
    f = pl.program_id(1)

    @pl.when(f == 0)
    def _():
        xb_ref[...] = x_ref[...].astype(BF16)
        acc_ref[...] = jnp.zeros_like(acc_ref)

    xb = xb_ref[...]
    gate = jnp.dot(xb, wg_ref[...], preferred_element_type=F32)
    up = jnp.dot(xb, wu_ref[...], preferred_element_type=F32)
    hid = (gate * jax.nn.sigmoid(gate) * up).astype(BF16)
    acc_ref[...] += jnp.dot(hid, wd_ref[...], preferred_element_type=F32)

    @pl.when(f == pl.num_programs(1) - 1)
    def _():
        y = _layer_norm(alpha * x_ref[...] + acc_ref[...], g_ref[...], b_ref[...])
        o_ref[...] = y
        for c in range(oc_ref.shape[0]):
            oc_ref[c] = y[:, c * LANES:(c + 1) * LANES]


def _ffn_res_ln(x, wg, wu, wd, g, b, alpha):
    M, D = x.shape
    F = wg.shape[1]
    tm, tf = min(FFN_TM, M), FFN_TF
    assert M % tm == 0 and F % tf == 0 and D % LANES == 0
    nc = D // LANES
    return pl.pallas_call(
        functools.partial(_ffn_kernel, alpha=alpha),
        grid=(M // tm, F // tf),
        in_specs=[pl.BlockSpec((tm, D), lambda i, f: (i, 0)),
                  pl.BlockSpec((D, tf), lambda i, f: (0, f)),
                  pl.BlockSpec((D, tf), lambda i, f: (0, f)),
                  pl.BlockSpec((tf, D), lambda i, f: (f, 0)),
                  pl.BlockSpec((1, D), lambda i, f: (0, 0)),
                  pl.BlockSpec((1, D), lambda i, f: (0, 0))],
        out_specs=[pl.BlockSpec((tm, D), lambda i, f: (i, 0)),
                   pl.BlockSpec((nc, tm, LANES), lambda i, f: (0, i, 0))],
        out_shape=[jax.ShapeDtypeStruct((M, D), F32),
                   jax.ShapeDtypeStruct((nc, M, LANES), F32)],
        scratch_shapes=[pltpu.VMEM((tm, D), BF16), pltpu.VMEM((tm, D), F32)],
        compiler_params=_params("parallel", "arbitrary"),
        name="ffn_res_ln",
    )(x, wg, wu, wd, g.reshape(1, D), b.reshape(1, D))


def _mm_residue_kernel(xc_ref, w_ref, o_ref, lhs_ref, *, dil):
    nc, tm, _ = xc_ref.shape
    n = tm // dil
    N = w_ref.shape[1]
    for r in range(dil):
        for c in range(nc):
            lhs_ref[r * n:(r + 1) * n, c * LANES:(c + 1) * LANES] = (
                xc_ref[c, pl.ds(r, n, stride=dil), :].astype(BF16))
    res = jnp.dot(lhs_ref[...], w_ref[...], preferred_element_type=F32).astype(o_ref.dtype)
    for r in range(dil):
        o_ref[:, r * N:(r + 1) * N] = res[r * n:(r + 1) * n, :]


def _matmul_residue_view(xc, w, dil, name):
    nc, M, _ = xc.shape
    K, N = w.shape
    tm = min(MM_TM, M)
    assert M % tm == 0 and tm % (16 * dil) == 0 and nc * LANES == K
    return pl.pallas_call(
        functools.partial(_mm_residue_kernel, dil=dil),
        grid=(M // tm,),
        in_specs=[pl.BlockSpec((nc, tm, LANES), lambda i: (0, i, 0)),
                  pl.BlockSpec((K, N), lambda i: (0, 0))],
        out_specs=pl.BlockSpec((tm // dil, dil * N), lambda i: (i, 0)),
        out_shape=jax.ShapeDtypeStruct((M // dil, dil * N), BF16),
        scratch_shapes=[pltpu.VMEM((tm, K), BF16)],
        compiler_params=_params("parallel"),
        name=name,
    )(xc, w)


def _attn_kernel(q_ref, kp_ref, kc_ref, vp_ref, vc_ref, o_ref):
    Q, dh = ATT_Q, ATT_HEAD_DIM
    n_sub = q_ref.shape[0] // Q
    qi = pl.program_id(2)
    row = lax.broadcasted_iota(jnp.int32, (Q, 2 * Q), 0)
    col = lax.broadcasted_iota(jnp.int32, (Q, 2 * Q), 1)
    band = (col >= row) & (col <= row + Q)
    first = jnp.where(qi > 0, 0, Q)
    lane = lax.broadcasted_iota(jnp.int32, (Q, LANES), 1)
    for j in range(n_sub):
        rs = slice(j * Q, (j + 1) * Q)
        ks = slice((j - 1) * Q, (j + 1) * Q)
        valid = (band & (col >= first)) if j == 0 else band
        lse_slab = jnp.zeros((Q, LANES), F32)
        for h in range(HEADS_PER_GROUP):
            cs = slice(h * dh, (h + 1) * dh)
            if j == 0:
                k = jnp.concatenate([kp_ref[:, cs], kc_ref[:Q, cs]], axis=0)
                v = jnp.concatenate([vp_ref[:, cs], vc_ref[:Q, cs]], axis=0)
            else:
                k, v = kc_ref[ks, cs], vc_ref[ks, cs]
            s = lax.dot_general(q_ref[rs, cs], k, (((1,), (1,)), ((), ())),
                                preferred_element_type=F32) * (dh ** -0.5)
            s = jnp.where(valid, s, NEG_BIG)
            m = jnp.max(s, axis=-1, keepdims=True)
            p = jnp.exp(s - m)
            l = jnp.sum(p, axis=-1, keepdims=True)
            o = jnp.dot(p.astype(BF16), v, preferred_element_type=F32)
            o_ref[rs, cs] = o / l
            lse_slab = jnp.where(lane == h, m + jnp.log(l), lse_slab)
        o_ref[rs, HEADS_PER_GROUP * dh:] = lse_slab


def _dilated_group(qkv_view, B, S, gi, win, dil):
    Q = ATT_Q
    gw = HEADS_PER_GROUP * ATT_HEAD_DIM
    nblk = qkv_view.shape[1] // (dil * gw)
    sd = S // dil
    qb = min(ATT_QB, sd)
    assert win // dil == Q and S % dil == 0 and sd % qb == 0 and qb % Q == 0
    n_sub = qb // Q
    view = qkv_view.reshape(B, sd, qkv_view.shape[1])
    cur = lambda comp: pl.BlockSpec((None, qb, gw), lambda b, r, qi: (b, qi, r * nblk + comp))
    prev = lambda comp: pl.BlockSpec(
        (None, Q, gw), lambda b, r, qi: (b, jnp.maximum(qi * n_sub - 1, 0), r * nblk + comp))
    out = pl.pallas_call(
        _attn_kernel,
        grid=(B, dil, sd // qb),
        in_specs=[cur(0), prev(1), cur(1), prev(2), cur(2)],
        out_specs=pl.BlockSpec((None, qb, ATT_OUT_W), lambda b, r, qi: (b, qi, r)),
        out_shape=jax.ShapeDtypeStruct((B, sd, dil * ATT_OUT_W), F32),
        compiler_params=_params("parallel", "parallel", "arbitrary"),
        name=f"dilated_attn_g{gi}",
    )(view, view, view, view, view)
    return out.reshape(B * sd, dil * ATT_OUT_W)


def _attn_merge_kernel(o0_ref, o1_ref, o2_ref, w_ref, x_ref, g_ref, b_ref, o_ref, ot_ref, *scratch, alpha):
    dh = ATT_HEAD_DIM
    n_slab = ATT_OUT_W // LANES
    tm = x_ref.shape[0]
    groups = []
    scratch = list(scratch)
    for ref, (_, dil) in zip((o0_ref, o1_ref, o2_ref), DIL_PATTERN):
        if dil == 1:
            groups.append(lambda c, ref=ref: ref[:, c * LANES:(c + 1) * LANES])
            continue
        scr = scratch.pop(0)
        n = tm // dil
        for r in range(dil):
            for c in range(n_slab):
                col = r * ATT_OUT_W + c * LANES
                scr[c, pl.ds(r, n, stride=dil), :] = ref[:, col:col + LANES]
        groups.append(lambda c, scr=scr: scr[c])
    lses = [grp(n_slab - 1) for grp in groups]
    mx = jnp.maximum(jnp.maximum(lses[0], lses[1]), lses[2])
    ws = [jnp.exp(l - mx) for l in lses]
    den = ws[0] + ws[1] + ws[2]
    ws = [w / den for w in ws]
    heads = []
    for h in range(HEADS_PER_GROUP):
        acc = ws[0][:, h:h + 1] * groups[0](h)
        acc += ws[1][:, h:h + 1] * groups[1](h)
        acc += ws[2][:, h:h + 1] * groups[2](h)
        heads.append(acc)
    merged = jnp.concatenate(heads, axis=-1).astype(BF16)
    y = jnp.dot(merged, w_ref[...], preferred_element_type=F32)
    out = _layer_norm(alpha * x_ref[...] + y, g_ref[...], b_ref[...])
    o_ref[...] = out
    _store_row_tiles(ot_ref, out)


def _attn_merge_res_ln(outs, w, x, g, b, alpha):
    M, D = x.shape
    K = w.shape[0]
    tm = min(LN_TM, M)
    dils = [dil for _, dil in DIL_PATTERN]
    assert M % tm == 0 and all(tm % (8 * dil) == 0 for dil in dils) and D == SUBLANES * LANES
    ospecs = [pl.BlockSpec((tm // dil, dil * ATT_OUT_W), lambda i: (i, 0)) for dil in dils]
    return pl.pallas_call(
        functools.partial(_attn_merge_kernel, alpha=alpha),
        grid=(M // tm,),
        in_specs=ospecs + [
                  pl.BlockSpec((K, D), lambda i: (0, 0)),
                  pl.BlockSpec((tm, D), lambda i: (i, 0)),
                  pl.BlockSpec((1, D), lambda i: (0, 0)),
                  pl.BlockSpec((1, D), lambda i: (0, 0))],
        out_specs=[pl.BlockSpec((tm, D), lambda i: (i, 0)),
                   pl.BlockSpec((tm * SUBLANES, LANES), lambda i: (i, 0))],
        out_shape=[jax.ShapeDtypeStruct((M, D), F32),
                   jax.ShapeDtypeStruct((M * SUBLANES, LANES), F32)],
        scratch_shapes=[pltpu.VMEM((ATT_OUT_W // LANES, tm, LANES), F32) for dil in dils if dil > 1],
        compiler_params=_params("parallel"),
        name="attn_merge_res_ln",
    )(*outs, w, x, g.reshape(1, D), b.reshape(1, D))


def _router_kernel(x_ref, wh_ref, wl_ref, slab_ref, cnt_ref, run_ref):
    tb = x_ref.shape[0]

    @pl.when(pl.program_id(0) == 0)
    def _():
        run_ref[...] = jnp.zeros_like(run_ref)

    x = x_ref[...]
    xh = x.astype(BF16)
    xl = (x - xh.astype(F32)).astype(BF16)
    logits = (jnp.dot(xh, wh_ref[...], preferred_element_type=F32)
              + jnp.dot(xh, wl_ref[...], preferred_element_type=F32)
              + jnp.dot(xl, wh_ref[...], preferred_element_type=F32))
    lane = lax.broadcasted_iota(jnp.int32, (tb, LANES), 1)
    logits = jnp.where(lane < N_EXPERTS, logits, NEG_BIG)
    m1 = jnp.max(logits, axis=-1, keepdims=True)
    i1 = jnp.min(jnp.where(logits == m1, lane, LANES), axis=-1, keepdims=True)
    rest = jnp.where(lane == i1, NEG_BIG, logits)
    m2 = jnp.max(rest, axis=-1, keepdims=True)
    i2 = jnp.min(jnp.where(rest == m2, lane, LANES), axis=-1, keepdims=True)
    e = jnp.exp(m2 - m1)
    g1 = 1.0 / (1.0 + e)
    g2 = e / (1.0 + e)
    onehot = jnp.where((lane == i1) | (lane == i2), 1.0, 0.0)
    r = lax.broadcasted_iota(jnp.int32, (tb, tb), 0)
    c = lax.broadcasted_iota(jnp.int32, (tb, tb), 1)
    lower = jnp.where(c < r, 1.0, 0.0).astype(BF16)
    before = jnp.dot(lower, onehot.astype(BF16), preferred_element_type=F32) + run_ref[...]
    rank1 = jnp.sum(jnp.where(lane == i1, before, 0.0), axis=-1, keepdims=True)
    rank2 = jnp.sum(jnp.where(lane == i2, before, 0.0), axis=-1, keepdims=True)
    total = run_ref[...] + jnp.sum(onehot, axis=0, keepdims=True)
    run_ref[...] = total
    cnt_ref[...] = total
    slab = jnp.where(lane == 0, i1.astype(F32), 0.0)
    slab = jnp.where(lane == 1, i2.astype(F32), slab)
    slab = jnp.where(lane == 2, g1, slab)
    slab = jnp.where(lane == 3, g2, slab)
    slab = jnp.where(lane == 4, rank1, slab)
    slab = jnp.where(lane == 5, rank2, slab)
    slab_ref[...] = slab


def _router(x, w_router):
    T, D = x.shape
    tb = min(ROUTE_TB, T)
    assert T % tb == 0
    w = jnp.zeros((D, LANES), F32).at[:, :N_EXPERTS].set(w_router)
    wh = w.astype(BF16)
    wl = (w - wh.astype(F32)).astype(BF16)
    return pl.pallas_call(
        _router_kernel,
        grid=(T // tb,),
        in_specs=[pl.BlockSpec((tb, D), lambda i: (i, 0)),
                  pl.BlockSpec((D, LANES), lambda i: (0, 0)),
                  pl.BlockSpec((D, LANES), lambda i: (0, 0))],
        out_specs=[pl.BlockSpec((tb, LANES), lambda i: (i, 0)),
                   pl.BlockSpec((1, LANES), lambda i: (0, 0))],
        out_shape=[jax.ShapeDtypeStruct((T, LANES), F32),
                   jax.ShapeDtypeStruct((1, LANES), F32)],
        scratch_shapes=[pltpu.VMEM((1, LANES), F32)],
        compiler_params=_params("arbitrary"),
        name="moe_router",
    )(x, wh, wl)


def _expert_kernel(be_ref, na_ref, tok_ref, tok_nxt_ref, dst_prv_ref, dst_ref, x_hbm,
                   wg_ref, wu_ref, wd_ref, y_hbm, xbuf_ref, xb_ref, acc_ref, ybuf_ref,
                   hid_ref, sem_in, sem_out, *, n_f, subs):
    del be_ref
    i, f = pl.program_id(0), pl.program_id(1)
    n_active = na_ref[0]
    tm = xb_ref.shape[0]
    per = tm // (n_f * len(subs))
    slot = lax.rem(i, 2)

    def tile(row):
        return pl.ds(pl.multiple_of(row * SUBLANES, SUBLANES), SUBLANES)

    def gather(tok, t, s):
        return pltpu.make_async_copy(x_hbm.at[tile(tok)], xbuf_ref.at[s, tile(t)], sem_in.at[s])

    def scatter(dst, t, s):
        return pltpu.make_async_copy(ybuf_ref.at[s, tile(t)], y_hbm.at[tile(dst)], sem_out.at[s])

    def gathered(s):
        return pltpu.make_async_copy(x_hbm.at[pl.ds(0, tm * SUBLANES)], xbuf_ref.at[s], sem_in.at[s])

    def scattered(s):
        return pltpu.make_async_copy(ybuf_ref.at[s], y_hbm.at[pl.ds(0, tm * SUBLANES)], sem_out.at[s])

    def for_rows(fn):
        def body(t, carry):
            fn(t)
            return carry
        lax.fori_loop(0, tm, body, 0, unroll=DMA_UNROLL)

    @pl.when(i < n_active)
    def _():
        @pl.when((i == 0) & (f == 0))
        def _():
            for_rows(lambda t: gather(tok_ref[0, 0, t], t, 0).start())
            ybuf_ref[1] = jnp.zeros(ybuf_ref.shape[1:], F32)

        @pl.when(f == 0)
        def _():
            gathered(slot).wait()
            xb_ref[...] = _load_row_tiles(xbuf_ref.at[slot], tm).astype(BF16)
            acc_ref[...] = jnp.zeros_like(acc_ref)

        xb = xb_ref[...]
        off = 0
        for c, width in enumerate(subs):
            base = (f * len(subs) + c) * per
            for t in range(per):
                gather(tok_nxt_ref[0, 0, base + t], base + t, 1 - slot).start()
                scatter(dst_prv_ref[0, 0, base + t], base + t, 1 - slot).start()
            cs = slice(off, off + width)
            gate = jnp.dot(xb, wg_ref[:, cs], preferred_element_type=F32)
            up = jnp.dot(xb, wu_ref[:, cs], preferred_element_type=F32)
            hid_ref[:, cs] = (gate * jax.nn.sigmoid(gate) * up).astype(BF16)
            off += width
        acc_ref[...] += jnp.dot(hid_ref[...], wd_ref[...], preferred_element_type=F32)

        @pl.when(f == n_f - 1)
        def _():
            @pl.when(i > 0)
            def _():
                scattered(slot).wait()
            _store_row_tiles(ybuf_ref.at[slot], acc_ref[...])

        @pl.when((i == n_active - 1) & (f == n_f - 1))
        def _():
            for_rows(lambda t: scatter(dst_ref[0, 0, t], t, slot).start())
            gathered(1 - slot).wait()
            scattered(1 - slot).wait()
            scattered(slot).wait()

    @pl.when((i >= n_active) & (f == n_f - 1))
    def _():
        ybuf_ref[0] = jnp.zeros(ybuf_ref.shape[1:], F32)
        for_rows(lambda t: scatter(dst_ref[0, 0, t], t, 0).start())
        scattered(0).wait()


def _experts(x, slot_tok, slot_dst, n_rows_out, blk_exp, n_active, wg, wu, wd, layer):
    D = x.shape[1] * SUBLANES
    P = slot_tok.shape[0]
    F = wg.shape[3]
    tm, tf = MOE_BLOCK, EXP_TF
    subs = (EXP_SUB,) * (tf // EXP_SUB) + ((tf % EXP_SUB,) if tf % EXP_SUB else ())
    assert P % tm == 0 and F % tf == 0 and tm % ((F // tf) * len(subs)) == 0
    nb, nf = P // tm, F // tf
    toks = slot_tok.reshape(nb, 1, tm)
    spare = n_rows_out - tm + jnp.arange(tm, dtype=jnp.int32)
    dsts = jnp.concatenate([spare, slot_dst]).reshape(nb + 1, 1, tm)
    smem = lambda fn: pl.BlockSpec((1, 1, tm), fn, memory_space=pltpu.SMEM)
    grid_spec = pltpu.PrefetchScalarGridSpec(
        num_scalar_prefetch=2,
        grid=(nb, nf),
        in_specs=[smem(lambda i, f, be, na: (i, 0, 0)),
                  smem(lambda i, f, be, na: (jnp.minimum(i + 1, nb - 1), 0, 0)),
                  smem(lambda i, f, be, na: (i, 0, 0)),
                  smem(lambda i, f, be, na: (i + 1, 0, 0)),
                  pl.BlockSpec(memory_space=pl.ANY),
                  pl.BlockSpec((None, None, D, tf), lambda i, f, be, na: (layer, be[i], 0, f)),
                  pl.BlockSpec((None, None, D, tf), lambda i, f, be, na: (layer, be[i], 0, f)),
                  pl.BlockSpec((None, None, tf, D), lambda i, f, be, na: (layer, be[i], f, 0))],
        out_specs=pl.BlockSpec(memory_space=pl.ANY),
        scratch_shapes=[pltpu.VMEM((2, tm * SUBLANES, LANES), F32), pltpu.VMEM((tm, D), BF16),
                        pltpu.VMEM((tm, D), F32), pltpu.VMEM((2, tm * SUBLANES, LANES), F32),
                        pltpu.VMEM((tm, tf), BF16),
                        pltpu.SemaphoreType.DMA((2,)), pltpu.SemaphoreType.DMA((2,))],
    )
    return pl.pallas_call(
        functools.partial(_expert_kernel, n_f=nf, subs=subs),
        grid_spec=grid_spec,
        out_shape=jax.ShapeDtypeStruct((n_rows_out * SUBLANES, LANES), F32),
        compiler_params=_params("arbitrary", "arbitrary"),
        name="moe_experts",
    )(blk_exp, n_active, toks, toks, dsts, dsts, x, wg, wu, wd)


def _combine_kernel(y1_ref, y2_ref, slab_ref, x_ref, g_ref, b_ref, o_ref, *, alpha):
    slab = slab_ref[...]
    tb = x_ref.shape[0]
    mix = slab[:, 2:3] * _load_row_tiles(y1_ref, tb) + slab[:, 3:4] * _load_row_tiles(y2_ref, tb)
    o_ref[...] = _layer_norm(alpha * x_ref[...] + mix, g_ref[...], b_ref[...])


def _combine_res_ln(y, slab, x, g, b, alpha):
    T, D = x.shape
    tb = min(COMB_TB, T)
    assert T % tb == 0
    nb = T // tb
    return pl.pallas_call(
        functools.partial(_combine_kernel, alpha=alpha),
        grid=(nb,),
        in_specs=[pl.BlockSpec((tb * SUBLANES, LANES), lambda i: (i, 0)),
                  pl.BlockSpec((tb * SUBLANES, LANES), lambda i: (i + nb, 0)),
                  pl.BlockSpec((tb, LANES), lambda i: (i, 0)),
                  pl.BlockSpec((tb, D), lambda i: (i, 0)),
                  pl.BlockSpec((1, D), lambda i: (0, 0)),
                  pl.BlockSpec((1, D), lambda i: (0, 0))],
        out_specs=pl.BlockSpec((tb, D), lambda i: (i, 0)),
        out_shape=jax.ShapeDtypeStruct((T, D), F32),
        compiler_params=_params("parallel"),
        name="moe_combine_res_ln",
    )(y, y, slab, x, g.reshape(1, D), b.reshape(1, D))


def _slot_table_kernel(d1_ref, d2_ref, fill_end_ref, pad_end_ref, o_ref):
    T = d1_ref.shape[0]

    def place(t, carry):
        o_ref[d1_ref[t]] = t
        o_ref[d2_ref[t]] = T + t
        return carry

    lax.fori_loop(0, T, place, 0, unroll=DMA_UNROLL)

    spare = jnp.int32(2 * T)
    bounds = [(fill_end_ref[e], pad_end_ref[e]) for e in range(N_EXPERTS)]
    bounds.append((pad_end_ref[N_EXPERTS - 1], jnp.int32(o_ref.shape[0])))
    for lo, hi in bounds:
        def fill(s, carry, lo=lo, first=spare):
            o_ref[s] = first + (s - lo)
            return carry
        lax.fori_loop(lo, hi, fill, 0)
        spare = spare + (hi - lo)


def _slot_table(d1, d2, fill_end, pad_end, n_slots):
    smem = pl.BlockSpec(memory_space=pltpu.SMEM)
    n_pad = -(-n_slots // SMEM_1D_TILE) * SMEM_1D_TILE
    assert d1.shape[0] % SMEM_1D_TILE == 0
    return pl.pallas_call(
        _slot_table_kernel,
        in_specs=[smem, smem, smem, smem],
        out_specs=smem,
        out_shape=jax.ShapeDtypeStruct((n_pad,), jnp.int32),
        name="moe_slot_table",
    )(d1, d2, fill_end.astype(jnp.int32), pad_end.astype(jnp.int32))[:n_slots]


def _moe_res_ln(x, x_tiles, w_router, wg, wu, wd, layer, g, b, alpha):
    T, D = x.shape
    slab, counts = _router(x, w_router)
    counts = counts[0, :N_EXPERTS].astype(jnp.int32)
    padded = ((counts + MOE_BLOCK - 1) // MOE_BLOCK) * MOE_BLOCK
    pad_end = jnp.cumsum(padded)
    pad_start = pad_end - padded
    e1 = slab[:, 0].astype(jnp.int32)
    e2 = slab[:, 1].astype(jnp.int32)
    d1 = pad_start[e1] + slab[:, 4].astype(jnp.int32)
    d2 = pad_start[e2] + slab[:, 5].astype(jnp.int32)
    nblk = -(-(2 * T) // MOE_BLOCK) + N_EXPERTS
    n_slots = nblk * MOE_BLOCK
    blk_start = jnp.arange(nblk, dtype=jnp.int32) * MOE_BLOCK
    blk_exp = jnp.minimum(jnp.sum(pad_end[None, :] <= blk_start[:, None], axis=1), N_EXPERTS - 1)
    n_active = (pad_end[-1:] // MOE_BLOCK).astype(jnp.int32)
    slot_dst = _slot_table(d1, d2, pad_start + counts, pad_end, n_slots)
    slot_tok = jnp.where(slot_dst < 2 * T, slot_dst % T, 0)
    n_rows_out = n_slots + MOE_BLOCK
    y = _experts(x_tiles, slot_tok, slot_dst, n_rows_out, blk_exp.astype(jnp.int32), n_active,
                 wg, wu, wd, layer)
    return _combine_res_ln(y, slab, x, g, b, alpha)


def kernel(x, ln_gain, ln_bias, ret_w_in, ret_gn_gain, ret_w_out, att_w_qkv, att_w_out,
           ffn_w_gate, ffn_w_up, ffn_w_down, moe_w_router, moe_w_gate, moe_w_up, moe_w_down):
    B, S, D = x.shape
    depth = ln_gain.shape[0]
    alpha = (2 * depth) ** 0.25
    moe_wg, moe_wu, moe_wd = (w.astype(BF16) for w in (moe_w_gate, moe_w_up, moe_w_down))
    h = x.reshape(B * S, D)
    h_blocks = None
    for i in range(depth):
        j = i // 2
        if i % 2 == 0:
            proj = _matmul(h, ret_w_in[j].astype(BF16), "ret_in_proj")
            gated = _retention_core(proj, B, S, ret_gn_gain[j])
            h = _mm_res_ln(gated, ret_w_out[j].astype(BF16), h, ln_gain[i, 0], ln_bias[i, 0], alpha,
                           "ret_out_res_ln")
            h, h_blocks = _ffn_res_ln(h, ffn_w_gate[j].astype(BF16), ffn_w_up[j].astype(BF16),
                                      ffn_w_down[j].astype(BF16), ln_gain[i, 1], ln_bias[i, 1], alpha)
        else:
            gw = HEADS_PER_GROUP * ATT_HEAD_DIM
            outs = []
            for gi, (win, dil) in enumerate(DIL_PATTERN):
                w_g = jnp.concatenate(
                    [att_w_qkv[j][:, (c * N_GROUPS + gi) * gw:(c * N_GROUPS + gi + 1) * gw] for c in range(3)],
                    axis=1).astype(BF16)
                if dil == 1:
                    qkv = _matmul(h, w_g, f"att_qkv_proj_g{gi}")
                else:
                    if h_blocks is None:
                        h_blocks = jnp.transpose(h.reshape(B * S, D // LANES, LANES), (1, 0, 2))
                    qkv = _matmul_residue_view(h_blocks, w_g, dil, f"att_qkv_proj_g{gi}")
                outs.append(_dilated_group(qkv, B, S, gi, win, dil))
            h_blocks = None
            h, h_tiles = _attn_merge_res_ln(outs, att_w_out[j].astype(BF16), h, ln_gain[i, 0], ln_bias[i, 0],
                                            alpha)
            h = _moe_res_ln(h, h_tiles, moe_w_router[j], moe_wg, moe_wu, moe_wd, j,
                            ln_gain[i, 1], ln_bias[i, 1], alpha)
    return h.reshape(B, S, D)
```

```python
import functools

import jax
import jax.numpy as jnp
from jax import lax
from jax.experimental import pallas as pl
from jax.experimental.pallas import tpu as pltpu

F32 = jnp.float32
BF16 = jnp.bfloat16

RET_HEADS = 4
RET_CHUNK = 128
ROPE_BASE = 10000.0
DIL_PATTERN = ((128, 1), (512, 4), (2048, 16))
N_GROUPS = len(DIL_PATTERN)
HEADS_PER_GROUP = 4
ATT_HEAD_DIM = 128
N_EXPERTS = 8
LN_EPS = 1e-5

LANES = 128
SUBLANES = 8
BF16_ROWS = 2 * SUBLANES
SMEM_1D_TILE = 1024
VMEM_LIMIT = 56 * 1024 * 1024

MM_TM = 1024
MM_TN = 2048
LN_TM = 512
FFN_TM = 1024
FFN_TF = 512
EXP_TF = 512
RET_ROWS = 512
ATT_Q = 128
ATT_QB = 512
ATT_OUT_W = HEADS_PER_GROUP * ATT_HEAD_DIM + LANES
MOE_BLOCK = 896
ROUTE_TB = 512
COMB_TB = 512
DMA_UNROLL = 8
NEG_BIG = -1e30


def _params(*sem):
    return pltpu.CompilerParams(dimension_semantics=sem, vmem_limit_bytes=VMEM_LIMIT)


def _layer_norm(y, g, b):
    mu = jnp.mean(y, axis=-1, keepdims=True)
    d = y - mu
    var = jnp.mean(d * d, axis=-1, keepdims=True)
    return d * lax.rsqrt(var + LN_EPS) * g + b


def _store_row_tiles(ref, y):
    m = y.shape[0]
    for c in range(SUBLANES):
        ref[pl.ds(c, m, stride=SUBLANES), :] = y[:, c * LANES:(c + 1) * LANES]


def _load_row_tiles(ref, m):
    return jnp.concatenate([ref[pl.ds(c, m, stride=SUBLANES), :] for c in range(SUBLANES)], axis=-1)


def _mm_kernel(x_ref, w_ref, o_ref, xb_ref):
    @pl.when(pl.program_id(1) == 0)
    def _():
        xb_ref[...] = x_ref[...].astype(BF16)

    o_ref[...] = jnp.dot(xb_ref[...], w_ref[...], preferred_element_type=F32).astype(o_ref.dtype)


def _matmul(x, w, name):
    M, K = x.shape
    N = w.shape[1]
    tm = min(MM_TM, M)
    tn = max(t for t in range(LANES, MM_TN + 1, LANES) if N % t == 0)
    assert M % tm == 0
    return pl.pallas_call(
        _mm_kernel,
        grid=(M // tm, N // tn),
        in_specs=[pl.BlockSpec((tm, K), lambda i, j: (i, 0)),
                  pl.BlockSpec((K, tn), lambda i, j: (0, j))],
        out_specs=pl.BlockSpec((tm, tn), lambda i, j: (i, j)),
        out_shape=jax.ShapeDtypeStruct((M, N), BF16),
        scratch_shapes=[pltpu.VMEM((tm, K), BF16)],
        compiler_params=_params("parallel", "arbitrary"),
        name=name,
    )(x, w)


def _ret_kernel(q_ref, k_ref, v_ref, g_ref, cos_ref, sin_ref, dec_ref, qdec_ref, kdec_ref,
                cdec_ref, gain_ref, o_ref, state_ref, *, n_chunks, dk):
    C, H = RET_CHUNK, RET_HEADS
    half = dk // 2
    dv = 2 * dk

    @pl.when(pl.program_id(1) == 0)
    def _():
        state_ref[...] = jnp.zeros_like(state_ref)

    def rot(t, cos, sin):
        t1, t2 = t[:, :half], t[:, half:]
        return jnp.concatenate([t1 * cos - t2 * sin, t2 * cos + t1 * sin], axis=-1)

    def chunk(c, carry):
        rows = pl.ds(pl.multiple_of(c * C, C), C)
        cos, sin = cos_ref[rows, :], sin_ref[rows, :]
        for h in range(H):
            kc = slice(h * dk, (h + 1) * dk)
            vc = slice(h * dv, (h + 1) * dv)
            q = rot(q_ref[rows, kc].astype(F32), cos, sin)
            k = rot(k_ref[rows, kc].astype(F32), cos, sin) * (dk ** -0.5)
            v = v_ref[rows, vc]
            scores = lax.dot_general(q.astype(BF16), k.astype(BF16), (((1,), (1,)), ((), ())),
                                     preferred_element_type=F32) * dec_ref[h]
            intra = jnp.dot(scores.astype(BF16), v, preferred_element_type=F32)
            state = state_ref[h]
            cross = jnp.dot((q * qdec_ref[h]).astype(BF16), state.astype(BF16),
                            preferred_element_type=F32)
            kn_t = (k * kdec_ref[h]).T.astype(BF16)
            state_ref[h] = cdec_ref[h] * state + jnp.dot(kn_t, v, preferred_element_type=F32)
            r = intra + cross
            mu = jnp.mean(r, axis=-1, keepdims=True)
            d = r - mu
            var = jnp.mean(d * d, axis=-1, keepdims=True)
            normed = d * lax.rsqrt(var + LN_EPS) * gain_ref[:, vc]
            gate = g_ref[rows, vc].astype(F32)
            o_ref[rows, vc] = (gate * jax.nn.sigmoid(gate) * normed).astype(o_ref.dtype)
        return carry

    lax.fori_loop(0, n_chunks, chunk, 0)


def _retention_core(proj, B, S, gn_gain):
    H, C = RET_HEADS, RET_CHUNK
    cols = proj.shape[1]
    dk = cols // (6 * H)
    dv = 2 * dk
    rb = min(RET_ROWS, S)
    assert S % rb == 0 and rb % C == 0
    nr = S // rb

    half = dk // 2
    inv = ROPE_BASE ** (-jnp.arange(half, dtype=F32) / half)
    ang = jnp.arange(S).astype(F32)[:, None] * inv[None, :]
    cos, sin = jnp.cos(ang), jnp.sin(ang)
    log_gamma = jnp.log(1.0 - 2.0 ** (-5.0 - jnp.arange(H, dtype=F32)))
    idx = jnp.arange(C, dtype=F32)
    rel = idx[:, None] - idx[None, :]
    decay_intra = jnp.where(rel >= 0, jnp.exp(log_gamma[:, None, None] * jnp.maximum(rel, 0.0)), 0.0)
    q_dec = jnp.exp(log_gamma[None, :] * (idx[:, None] + 1.0))
    k_dec = jnp.exp(log_gamma[None, :] * (C - 1.0 - idx[:, None]))
    chunk_dec = jnp.exp(log_gamma * C)
    qdec_b = jnp.broadcast_to(q_dec.T[:, :, None], (H, C, dk))
    kdec_b = jnp.broadcast_to(k_dec.T[:, :, None], (H, C, dk))
    cdec_b = jnp.broadcast_to(chunk_dec[:, None, None], (H, 1, dv))
    gain = gn_gain.reshape(1, H * dv)

    full = lambda shape: pl.BlockSpec(shape, lambda b, i: (0,) * len(shape))
    kern = functools.partial(_ret_kernel, n_chunks=rb // C, dk=dk)
    return pl.pallas_call(
        kern,
        grid=(B, nr),
        in_specs=[
            pl.BlockSpec((rb, H * dk), lambda b, i: (b * nr + i, 0)),
            pl.BlockSpec((rb, H * dk), lambda b, i: (b * nr + i, 1)),
            pl.BlockSpec((rb, H * dv), lambda b, i: (b * nr + i, 1)),
            pl.BlockSpec((rb, H * dv), lambda b, i: (b * nr + i, 2)),
            pl.BlockSpec((rb, half), lambda b, i: (i, 0)),
            pl.BlockSpec((rb, half), lambda b, i: (i, 0)),
            full((H, C, C)),
            full((H, C, dk)),
            full((H, C, dk)),
            full((H, 1, dv)),
            full((1, H * dv)),
        ],
        out_specs=pl.BlockSpec((rb, H * dv), lambda b, i: (b * nr + i, 0)),
        out_shape=jax.ShapeDtypeStruct((B * S, H * dv), BF16),
        scratch_shapes=[pltpu.VMEM((H, dk, dv), F32)],
        compiler_params=_params("parallel", "arbitrary"),
        name="retention_core",
    )(proj, proj, proj, proj, cos, sin, decay_intra, qdec_b, kdec_b, cdec_b, gain)


def _mm_res_ln_kernel(a_ref, w_ref, x_ref, g_ref, b_ref, o_ref, *, alpha):
    y = jnp.dot(a_ref[...], w_ref[...], preferred_element_type=F32)
    o_ref[...] = _layer_norm(alpha * x_ref[...] + y, g_ref[...], b_ref[...])


def _mm_res_ln(a, w, x, g, b, alpha, name):
    M, K = a.shape
    D = w.shape[1]
    tm = min(LN_TM, M)
    assert M % tm == 0
    return pl.pallas_call(
        functools.partial(_mm_res_ln_kernel, alpha=alpha),
        grid=(M // tm,),
        in_specs=[pl.BlockSpec((tm, K), lambda i: (i, 0)),
                  pl.BlockSpec((K, D), lambda i: (0, 0)),
                  pl.BlockSpec((tm, D), lambda i: (i, 0)),
                  pl.BlockSpec((1, D), lambda i: (0, 0)),
                  pl.BlockSpec((1, D), lambda i: (0, 0))],
        out_specs=pl.BlockSpec((tm, D), lambda i: (i, 0)),
        out_shape=jax.ShapeDtypeStruct((M, D), F32),
        compiler_params=_params("parallel"),
        name=name,
    )(a, w, x, g.reshape(1, D), b.reshape(1, D))


def _ffn_kernel(x_ref, wg_ref, wu_ref, wd_ref, g_ref, b_ref, o_ref, oc_ref, xb_ref, acc_ref, *, alpha):
    f = pl.program_id(1)

    @pl.when(f == 0)
    def _():
        xb_ref[...] = x_ref[...].astype(BF16)
        acc_ref[...] = jnp.zeros_like(acc_ref)

    xb = xb_ref[...]
    gate = jnp.dot(xb, wg_ref[...], preferred_element_type=F32)
    up = jnp.dot(xb, wu_ref[...], preferred_element_type=F32)
    hid = (gate * jax.nn.sigmoid(gate) * up).astype(BF16)
    acc_ref[...] += jnp.dot(hid, wd_ref[...], preferred_element_type=F32)

    @pl.when(f == pl.num_programs(1) - 1)
    def _():
        y = _layer_norm(alpha * x_ref[...] + acc_ref[...], g_ref[...], b_ref[...])
        o_ref[...] = y
        for c in range(oc_ref.shape[0]):
            oc_ref[c] = y[:, c * LANES:(c + 1) * LANES]


def _ffn_res_ln(x, wg, wu, wd, g, b, alpha):
    M, D = x.shape
    F = wg.shape[1]
    tm, tf = min(FFN_TM, M), FFN_TF
    assert M % tm == 0 and F % tf == 0 and D % LANES == 0
    nc = D // LANES
    return pl.pallas_call(
        functools.partial(_ffn_kernel, alpha=alpha),
        grid=(M // tm, F // tf),
        in_specs=[pl.BlockSpec((tm, D), lambda i, f: (i, 0)),
                  pl.BlockSpec((D, tf), lambda i, f: (0, f)),
                  pl.BlockSpec((D, tf), lambda i, f: (0, f)),
                  pl.BlockSpec((tf, D), lambda i, f: (f, 0)),
                  pl.BlockSpec((1, D), lambda i, f: (0, 0)),
                  pl.BlockSpec((1, D), lambda i, f: (0, 0))],
        out_specs=[pl.BlockSpec((tm, D), lambda i, f: (i, 0)),
                   pl.BlockSpec((nc, tm, LANES), lambda i, f: (0, i, 0))],
        out_shape=[jax.ShapeDtypeStruct((M, D), F32),
                   jax.ShapeDtypeStruct((nc, M, LANES), F32)],
        scratch_shapes=[pltpu.VMEM((tm, D), BF16), pltpu.VMEM((tm, D), F32)],
        compiler_params=_params("parallel", "arbitrary"),
        name="ffn_res_ln",
    )(x, wg, wu, wd, g.reshape(1, D), b.reshape(1, D))


def _mm_residue_kernel(xc_ref, w_ref, o_ref, lhs_ref, *, dil):
    nc, tm, _ = xc_ref.shape
    n = tm // dil
    N = w_ref.shape[1]
    for r in range(dil):
        for c in range(nc):
            lhs_ref[r * n:(r + 1) * n, c * LANES:(c + 1) * LANES] = (
                xc_ref[c, pl.ds(r, n, stride=dil), :].astype(BF16))
    res = jnp.dot(lhs_ref[...], w_ref[...], preferred_element_type=F32).astype(o_ref.dtype)
    for r in range(dil):
        o_ref[:, r * N:(r + 1) * N] = res[r * n:(r + 1) * n, :]


def _matmul_residue_view(xc, w, dil, name):
    nc, M, _ = xc.shape
    K, N = w.shape
    tm = min(MM_TM, M)
    assert M % tm == 0 and tm % (BF16_ROWS * dil) == 0 and nc * LANES == K
    return pl.pallas_call(
        functools.partial(_mm_residue_kernel, dil=dil),
        grid=(M // tm,),
        in_specs=[pl.BlockSpec((nc, tm, LANES), lambda i: (0, i, 0)),
                  pl.BlockSpec((K, N), lambda i: (0, 0))],
        out_specs=pl.BlockSpec((tm // dil, dil * N), lambda i: (i, 0)),
        out_shape=jax.ShapeDtypeStruct((M // dil, dil * N), BF16),
        scratch_shapes=[pltpu.VMEM((tm, K), BF16)],
        compiler_params=_params("parallel"),
        name=name,
    )(xc, w)


def _attn_kernel(q_ref, kp_ref, kc_ref, vp_ref, vc_ref, o_ref):
    Q, dh = ATT_Q, ATT_HEAD_DIM
    n_sub = q_ref.shape[0] // Q
    qi = pl.program_id(2)
    row = lax.broadcasted_iota(jnp.int32, (Q, 2 * Q), 0)
    col = lax.broadcasted_iota(jnp.int32, (Q, 2 * Q), 1)
    band = (col >= row) & (col <= row + Q)
    first = jnp.where(qi > 0, 0, Q)
    lane = lax.broadcasted_iota(jnp.int32, (Q, LANES), 1)
    for j in range(n_sub):
        rs = slice(j * Q, (j + 1) * Q)
        ks = slice((j - 1) * Q, (j + 1) * Q)
        valid = (band & (col >= first)) if j == 0 else band
        lse_slab = jnp.zeros((Q, LANES), F32)
        for h in range(HEADS_PER_GROUP):
            cs = slice(h * dh, (h + 1) * dh)
            if j == 0:
                k = jnp.concatenate([kp_ref[:, cs], kc_ref[:Q, cs]], axis=0)
                v = jnp.concatenate([vp_ref[:, cs], vc_ref[:Q, cs]], axis=0)
            else:
                k, v = kc_ref[ks, cs], vc_ref[ks, cs]
            s = lax.dot_general(q_ref[rs, cs], k, (((1,), (1,)), ((), ())),
                                preferred_element_type=F32) * (dh ** -0.5)
            s = jnp.where(valid, s, NEG_BIG)
            m = jnp.max(s, axis=-1, keepdims=True)
            p = jnp.exp(s - m)
            l = jnp.sum(p, axis=-1, keepdims=True)
            o = jnp.dot(p.astype(BF16), v, preferred_element_type=F32)
            o_ref[rs, cs] = o / l
            lse_slab = jnp.where(lane == h, m + jnp.log(l), lse_slab)
        o_ref[rs, HEADS_PER_GROUP * dh:] = lse_slab


def _dilated_group(qkv_view, B, S, gi, win, dil):
    Q = ATT_Q
    gw = HEADS_PER_GROUP * ATT_HEAD_DIM
    nblk = qkv_view.shape[1] // (dil * gw)
    sd = S // dil
    qb = min(ATT_QB, sd)
    assert win // dil == Q and S % dil == 0 and sd % qb == 0 and qb % Q == 0
    n_sub = qb // Q
    view = qkv_view.reshape(B, sd, qkv_view.shape[1])
    cur = lambda comp: pl.BlockSpec((None, qb, gw), lambda b, r, qi: (b, qi, r * nblk + comp))
    prev = lambda comp: pl.BlockSpec(
        (None, Q, gw), lambda b, r, qi: (b, jnp.maximum(qi * n_sub - 1, 0), r * nblk + comp))
    out = pl.pallas_call(
        _attn_kernel,
        grid=(B, dil, sd // qb),
        in_specs=[cur(0), prev(1), cur(1), prev(2), cur(2)],
        out_specs=pl.BlockSpec((None, qb, ATT_OUT_W), lambda b, r, qi: (b, qi, r)),
        out_shape=jax.ShapeDtypeStruct((B, sd, dil * ATT_OUT_W), F32),
        compiler_params=_params("parallel", "parallel", "arbitrary"),
        name=f"dilated_attn_g{gi}",
    )(view, view, view, view, view)
    return out.reshape(B * sd, dil * ATT_OUT_W)


def _attn_merge_kernel(o0_ref, o1_ref, o2_ref, w_ref, x_ref, g_ref, b_ref, o_ref, ot_ref, *scratch, alpha):
    dh = ATT_HEAD_DIM
    n_slab = ATT_OUT_W // LANES
    tm = x_ref.shape[0]
    groups = []
    scratch = list(scratch)
    for ref, (_, dil) in zip((o0_ref, o1_ref, o2_ref), DIL_PATTERN):
        if dil == 1:
            groups.append(lambda c, ref=ref: ref[:, c * LANES:(c + 1) * LANES])
            continue
        scr = scratch.pop(0)
        n = tm // dil
        for r in range(dil):
            for c in range(n_slab):
                col = r * ATT_OUT_W + c * LANES
                scr[c, pl.ds(r, n, stride=dil), :] = ref[:, col:col + LANES]
        groups.append(lambda c, scr=scr: scr[c])
    lses = [grp(n_slab - 1) for grp in groups]
    mx = jnp.maximum(jnp.maximum(lses[0], lses[1]), lses[2])
    ws = [jnp.exp(l - mx) for l in lses]
    den = ws[0] + ws[1] + ws[2]
    ws = [w / den for w in ws]
    heads = []
    for h in range(HEADS_PER_GROUP):
        acc = ws[0][:, h:h + 1] * groups[0](h)
        acc += ws[1][:, h:h + 1] * groups[1](h)
        acc += ws[2][:, h:h + 1] * groups[2](h)
        heads.append(acc)
    merged = jnp.concatenate(heads, axis=-1).astype(BF16)
    y = jnp.dot(merged, w_ref[...], preferred_element_type=F32)
    out = _layer_norm(alpha * x_ref[...] + y, g_ref[...], b_ref[...])
    o_ref[...] = out
    _store_row_tiles(ot_ref, out)


def _attn_merge_res_ln(outs, w, x, g, b, alpha):
    M, D = x.shape
    K = w.shape[0]
    tm = min(LN_TM, M)
    dils = [dil for _, dil in DIL_PATTERN]
    assert M % tm == 0 and all(tm % (8 * dil) == 0 for dil in dils) and D == SUBLANES * LANES
    ospecs = [pl.BlockSpec((tm // dil, dil * ATT_OUT_W), lambda i: (i, 0)) for dil in dils]
    return pl.pallas_call(
        functools.partial(_attn_merge_kernel, alpha=alpha),
        grid=(M // tm,),
        in_specs=ospecs + [
                  pl.BlockSpec((K, D), lambda i: (0, 0)),
                  pl.BlockSpec((tm, D), lambda i: (i, 0)),
                  pl.BlockSpec((1, D), lambda i: (0, 0)),
                  pl.BlockSpec((1, D), lambda i: (0, 0))],
        out_specs=[pl.BlockSpec((tm, D), lambda i: (i, 0)),
                   pl.BlockSpec((tm * SUBLANES, LANES), lambda i: (i, 0))],
        out_shape=[jax.ShapeDtypeStruct((M, D), F32),
                   jax.ShapeDtypeStruct((M * SUBLANES, LANES), F32)],
        scratch_shapes=[pltpu.VMEM((ATT_OUT_W // LANES, tm, LANES), F32) for dil in dils if dil > 1],
        compiler_params=_params("parallel"),
        name="attn_merge_res_ln",
    )(*outs, w, x, g.reshape(1, D), b.reshape(1, D))


def _router_kernel(x_ref, wh_ref, wl_ref, slab_ref, cnt_ref, run_ref):
    tb = x_ref.shape[0]

    @pl.when(pl.program_id(0) == 0)
    def _():
        run_ref[...] = jnp.zeros_like(run_ref)

    x = x_ref[...]
    xh = x.astype(BF16)
    xl = (x - xh.astype(F32)).astype(BF16)
    logits = (jnp.dot(xh, wh_ref[...], preferred_element_type=F32)
              + jnp.dot(xh, wl_ref[...], preferred_element_type=F32)
              + jnp.dot(xl, wh_ref[...], preferred_element_type=F32))
    lane = lax.broadcasted_iota(jnp.int32, (tb, LANES), 1)
    logits = jnp.where(lane < N_EXPERTS, logits, NEG_BIG)
    m1 = jnp.max(logits, axis=-1, keepdims=True)
    i1 = jnp.min(jnp.where(logits == m1, lane, LANES), axis=-1, keepdims=True)
    rest = jnp.where(lane == i1, NEG_BIG, logits)
    m2 = jnp.max(rest, axis=-1, keepdims=True)
    i2 = jnp.min(jnp.where(rest == m2, lane, LANES), axis=-1, keepdims=True)
    e = jnp.exp(m2 - m1)
    g1 = 1.0 / (1.0 + e)
    g2 = e / (1.0 + e)
    onehot = jnp.where((lane == i1) | (lane == i2), 1.0, 0.0)
    r = lax.broadcasted_iota(jnp.int32, (tb, tb), 0)
    c = lax.broadcasted_iota(jnp.int32, (tb, tb), 1)
    lower = jnp.where(c < r, 1.0, 0.0).astype(BF16)
    before = jnp.dot(lower, onehot.astype(BF16), preferred_element_type=F32) + run_ref[...]
    rank1 = jnp.sum(jnp.where(lane == i1, before, 0.0), axis=-1, keepdims=True)
    rank2 = jnp.sum(jnp.where(lane == i2, before, 0.0), axis=-1, keepdims=True)
    total = run_ref[...] + jnp.sum(onehot, axis=0, keepdims=True)
    run_ref[...] = total
    cnt_ref[...] = total
    slab = jnp.where(lane == 0, i1.astype(F32), 0.0)
    slab = jnp.where(lane == 1, i2.astype(F32), slab)
    slab = jnp.where(lane == 2, g1, slab)
    slab = jnp.where(lane == 3, g2, slab)
    slab = jnp.where(lane == 4, rank1, slab)
    slab = jnp.where(lane == 5, rank2, slab)
    slab_ref[...] = slab


def _router(x, w_router):
    T, D = x.shape
    tb = min(ROUTE_TB, T)
    assert T % tb == 0
    w = jnp.zeros((D, LANES), F32).at[:, :N_EXPERTS].set(w_router)
    wh = w.astype(BF16)
    wl = (w - wh.astype(F32)).astype(BF16)
    return pl.pallas_call(
        _router_kernel,
        grid=(T // tb,),
        in_specs=[pl.BlockSpec((tb, D), lambda i: (i, 0)),
                  pl.BlockSpec((D, LANES), lambda i: (0, 0)),
                  pl.BlockSpec((D, LANES), lambda i: (0, 0))],
        out_specs=[pl.BlockSpec((tb, LANES), lambda i: (i, 0)),
                   pl.BlockSpec((1, LANES), lambda i: (0, 0))],
        out_shape=[jax.ShapeDtypeStruct((T, LANES), F32),
                   jax.ShapeDtypeStruct((1, LANES), F32)],
        scratch_shapes=[pltpu.VMEM((1, LANES), F32)],
        compiler_params=_params("arbitrary"),
        name="moe_router",
    )(x, wh, wl)


def _expert_kernel(be_ref, na_ref, tok_ref, tok_nxt_ref, dst_prv_ref, dst_ref, x_hbm,
                   wg_ref, wu_ref, wd_ref, y_hbm, xbuf_ref, xb_ref, acc_ref, ybuf_ref,
                   sem_in, sem_out, *, n_f):
    del be_ref
    i, f = pl.program_id(0), pl.program_id(1)
    n_active = na_ref[0]
    tm = xb_ref.shape[0]
    per = tm // n_f
    slot = lax.rem(i, 2)

    def tile(row):
        return pl.ds(pl.multiple_of(row * SUBLANES, SUBLANES), SUBLANES)

    def gather(tok, t, s):
        return pltpu.make_async_copy(x_hbm.at[tile(tok)], xbuf_ref.at[s, tile(t)], sem_in.at[s])

    def scatter(dst, t, s):
        return pltpu.make_async_copy(ybuf_ref.at[s, tile(t)], y_hbm.at[tile(dst)], sem_out.at[s])

    def gathered(s):
        return pltpu.make_async_copy(x_hbm.at[pl.ds(0, tm * SUBLANES)], xbuf_ref.at[s], sem_in.at[s])

    def scattered(s):
        return pltpu.make_async_copy(ybuf_ref.at[s], y_hbm.at[pl.ds(0, tm * SUBLANES)], sem_out.at[s])

    def for_rows(fn):
        def body(t, carry):
            fn(t)
            return carry
        lax.fori_loop(0, tm, body, 0, unroll=DMA_UNROLL)

    @pl.when(i < n_active)
    def _():
        @pl.when((i == 0) & (f == 0))
        def _():
            for_rows(lambda t: gather(tok_ref[0, 0, t], t, 0).start())
            ybuf_ref[1] = jnp.zeros(ybuf_ref.shape[1:], F32)

        @pl.when(f == 0)
        def _():
            gathered(slot).wait()
            xb_ref[...] = _load_row_tiles(xbuf_ref.at[slot], tm).astype(BF16)
            acc_ref[...] = jnp.zeros_like(acc_ref)

        base = f * per
        for t in range(per):
            gather(tok_nxt_ref[0, 0, base + t], base + t, 1 - slot).start()
            scatter(dst_prv_ref[0, 0, base + t], base + t, 1 - slot).start()
        xb = xb_ref[...]
        gate = jnp.dot(xb, wg_ref[...], preferred_element_type=F32)
        up = jnp.dot(xb, wu_ref[...], preferred_element_type=F32)
        hid = (gate * jax.nn.sigmoid(gate) * up).astype(BF16)
        acc_ref[...] += jnp.dot(hid, wd_ref[...], preferred_element_type=F32)

        @pl.when(f == n_f - 1)
        def _():
            @pl.when(i > 0)
            def _():
                scattered(slot).wait()
            _store_row_tiles(ybuf_ref.at[slot], acc_ref[...])

        @pl.when((i == n_active - 1) & (f == n_f - 1))
        def _():
            for_rows(lambda t: scatter(dst_ref[0, 0, t], t, slot).start())
            gathered(1 - slot).wait()
            scattered(1 - slot).wait()
            scattered(slot).wait()

    @pl.when((i >= n_active) & (f == n_f - 1))
    def _():
        ybuf_ref[0] = jnp.zeros(ybuf_ref.shape[1:], F32)
        for_rows(lambda t: scatter(dst_ref[0, 0, t], t, 0).start())
        scattered(0).wait()


def _experts(x, slot_tok, slot_dst, n_rows_out, blk_exp, n_active, wg, wu, wd, layer):
    D = x.shape[1] * SUBLANES
    P = slot_tok.shape[0]
    F = wg.shape[3]
    tm, tf = MOE_BLOCK, EXP_TF
    assert P % tm == 0 and F % tf == 0 and tm % (F // tf) == 0
    nb, nf = P // tm, F // tf
    toks = slot_tok.reshape(nb, 1, tm)
    spare = n_rows_out - tm + jnp.arange(tm, dtype=jnp.int32)
    dsts = jnp.concatenate([spare, slot_dst]).reshape(nb + 1, 1, tm)
    smem = lambda fn: pl.BlockSpec((1, 1, tm), fn, memory_space=pltpu.SMEM)
    grid_spec = pltpu.PrefetchScalarGridSpec(
        num_scalar_prefetch=2,
        grid=(nb, nf),
        in_specs=[smem(lambda i, f, be, na: (i, 0, 0)),
                  smem(lambda i, f, be, na: (jnp.minimum(i + 1, nb - 1), 0, 0)),
                  smem(lambda i, f, be, na: (i, 0, 0)),
                  smem(lambda i, f, be, na: (i + 1, 0, 0)),
                  pl.BlockSpec(memory_space=pl.ANY),
                  pl.BlockSpec((None, None, D, tf), lambda i, f, be, na: (layer, be[i], 0, f)),
                  pl.BlockSpec((None, None, D, tf), lambda i, f, be, na: (layer, be[i], 0, f)),
                  pl.BlockSpec((None, None, tf, D), lambda i, f, be, na: (layer, be[i], f, 0))],
        out_specs=pl.BlockSpec(memory_space=pl.ANY),
        scratch_shapes=[pltpu.VMEM((2, tm * SUBLANES, LANES), F32), pltpu.VMEM((tm, D), BF16),
                        pltpu.VMEM((tm, D), F32), pltpu.VMEM((2, tm * SUBLANES, LANES), F32),
                        pltpu.SemaphoreType.DMA((2,)), pltpu.SemaphoreType.DMA((2,))],
    )
    return pl.pallas_call(
        functools.partial(_expert_kernel, n_f=nf),
        grid_spec=grid_spec,
        out_shape=jax.ShapeDtypeStruct((n_rows_out * SUBLANES, LANES), F32),
        compiler_params=_params("arbitrary", "arbitrary"),
        name="moe_experts",
    )(blk_exp, n_active, toks, toks, dsts, dsts, x, wg, wu, wd)


def _combine_kernel(y1_ref, y2_ref, slab_ref, x_ref, g_ref, b_ref, o_ref, *, alpha):
    slab = slab_ref[...]
    tb = x_ref.shape[0]
    mix = slab[:, 2:3] * _load_row_tiles(y1_ref, tb) + slab[:, 3:4] * _load_row_tiles(y2_ref, tb)
    o_ref[...] = _layer_norm(alpha * x_ref[...] + mix, g_ref[...], b_ref[...])


def _combine_res_ln(y, slab, x, g, b, alpha):
    T, D = x.shape
    tb = min(COMB_TB, T)
    assert T % tb == 0
    nb = T // tb
    return pl.pallas_call(
        functools.partial(_combine_kernel, alpha=alpha),
        grid=(nb,),
        in_specs=[pl.BlockSpec((tb * SUBLANES, LANES), lambda i: (i, 0)),
                  pl.BlockSpec((tb * SUBLANES, LANES), lambda i: (i + nb, 0)),
                  pl.BlockSpec((tb, LANES), lambda i: (i, 0)),
                  pl.BlockSpec((tb, D), lambda i: (i, 0)),
                  pl.BlockSpec((1, D), lambda i: (0, 0)),
                  pl.BlockSpec((1, D), lambda i: (0, 0))],
        out_specs=pl.BlockSpec((tb, D), lambda i: (i, 0)),
        out_shape=jax.ShapeDtypeStruct((T, D), F32),
        compiler_params=_params("parallel"),
        name="moe_combine_res_ln",
    )(y, y, slab, x, g.reshape(1, D), b.reshape(1, D))


def _slot_table_kernel(d1_ref, d2_ref, fill_end_ref, pad_end_ref, o_ref):
    T = d1_ref.shape[0]

    def place(t, carry):
        o_ref[d1_ref[t]] = t
        o_ref[d2_ref[t]] = T + t
        return carry

    lax.fori_loop(0, T, place, 0, unroll=DMA_UNROLL)

    spare = jnp.int32(2 * T)
    bounds = [(fill_end_ref[e], pad_end_ref[e]) for e in range(N_EXPERTS)]
    bounds.append((pad_end_ref[N_EXPERTS - 1], jnp.int32(o_ref.shape[0])))
    for lo, hi in bounds:
        def fill(s, carry, lo=lo, first=spare):
            o_ref[s] = first + (s - lo)
            return carry
        lax.fori_loop(lo, hi, fill, 0)
        spare = spare + (hi - lo)


def _slot_table(d1, d2, fill_end, pad_end, n_slots):
    smem = pl.BlockSpec(memory_space=pltpu.SMEM)
    n_pad = -(-n_slots // SMEM_1D_TILE) * SMEM_1D_TILE
    assert d1.shape[0] % SMEM_1D_TILE == 0
    return pl.pallas_call(
        _slot_table_kernel,
        in_specs=[smem, smem, smem, smem],
        out_specs=smem,
        out_shape=jax.ShapeDtypeStruct((n_pad,), jnp.int32),
        name="moe_slot_table",
    )(d1, d2, fill_end.astype(jnp.int32), pad_end.astype(jnp.int32))[:n_slots]


def _moe_res_ln(x, x_tiles, w_router, wg, wu, wd, layer, g, b, alpha):
    T, D = x.shape
    slab, counts = _router(x, w_router)
    counts = counts[0, :N_EXPERTS].astype(jnp.int32)
    padded = ((counts + MOE_BLOCK - 1) // MOE_BLOCK) * MOE_BLOCK
    pad_end = jnp.cumsum(padded)
    pad_start = pad_end - padded
    e1 = slab[:, 0].astype(jnp.int32)
    e2 = slab[:, 1].astype(jnp.int32)
    d1 = pad_start[e1] + slab[:, 4].astype(jnp.int32)
    d2 = pad_start[e2] + slab[:, 5].astype(jnp.int32)
    nblk = -(-(2 * T) // MOE_BLOCK) + N_EXPERTS
    n_slots = nblk * MOE_BLOCK
    blk_start = jnp.arange(nblk, dtype=jnp.int32) * MOE_BLOCK
    blk_exp = jnp.minimum(jnp.sum(pad_end[None, :] <= blk_start[:, None], axis=1), N_EXPERTS - 1)
    n_active = (pad_end[-1:] // MOE_BLOCK).astype(jnp.int32)
    slot_dst = _slot_table(d1, d2, pad_start + counts, pad_end, n_slots)
    slot_tok = jnp.where(slot_dst < 2 * T, slot_dst % T, 0)
    n_rows_out = n_slots + MOE_BLOCK
    y = _experts(x_tiles, slot_tok, slot_dst, n_rows_out, blk_exp.astype(jnp.int32), n_active,
                 wg, wu, wd, layer)
    return _combine_res_ln(y, slab, x, g, b, alpha)


def kernel(x, ln_gain, ln_bias, ret_w_in, ret_gn_gain, ret_w_out, att_w_qkv, att_w_out,
           ffn_w_gate, ffn_w_up, ffn_w_down, moe_w_router, moe_w_gate, moe_w_up, moe_w_down):
    B, S, D = x.shape
    depth = ln_gain.shape[0]
    alpha = (2 * depth) ** 0.25
    moe_wg, moe_wu, moe_wd = (w.astype(BF16) for w in (moe_w_gate, moe_w_up, moe_w_down))
    h = x.reshape(B * S, D)
    h_blocks = None
    for i in range(depth):
        j = i // 2
        if i % 2 == 0:
            proj = _matmul(h, ret_w_in[j].astype(BF16), "ret_in_proj")
            gated = _retention_core(proj, B, S, ret_gn_gain[j])
            h = _mm_res_ln(gated, ret_w_out[j].astype(BF16), h, ln_gain[i, 0], ln_bias[i, 0], alpha,
                           "ret_out_res_ln")
            h, h_blocks = _ffn_res_ln(h, ffn_w_gate[j].astype(BF16), ffn_w_up[j].astype(BF16),
                                      ffn_w_down[j].astype(BF16), ln_gain[i, 1], ln_bias[i, 1], alpha)
        else:
            gw = HEADS_PER_GROUP * ATT_HEAD_DIM
            outs = []
            for gi, (win, dil) in enumerate(DIL_PATTERN):
                w_g = jnp.concatenate(
                    [att_w_qkv[j][:, (c * N_GROUPS + gi) * gw:(c * N_GROUPS + gi + 1) * gw] for c in range(3)],
                    axis=1).astype(BF16)
                if dil == 1:
                    qkv = _matmul(h, w_g, f"att_qkv_proj_g{gi}")
                else:
                    if h_blocks is None:
                        h_blocks = jnp.transpose(h.reshape(B * S, D // LANES, LANES), (1, 0, 2))
                    qkv = _matmul_residue_view(h_blocks, w_g, dil, f"att_qkv_proj_g{gi}")
                outs.append(_dilated_group(qkv, B, S, gi, win, dil))
            h_blocks = None
            h, h_tiles = _attn_merge_res_ln(outs, att_w_out[j].astype(BF16), h, ln_gain[i, 0], ln_bias[i, 0],
                                            alpha)
            h = _moe_res_ln(h, h_tiles, moe_w_router[j], moe_wg, moe_wu, moe_wd, j,
                            ln_gain[i, 1], ln_bias[i, 1], alpha)
    return h.reshape(B, S, D)
```

```python
import functools

import jax
import jax.numpy as jnp
from jax import lax
from jax.experimental import pallas as pl
from jax.experimental.pallas import tpu as pltpu

F32 = jnp.float32
BF16 = jnp.bfloat16

RET_HEADS = 4
RET_CHUNK = 128
ROPE_BASE = 10000.0
DIL_PATTERN = ((128, 1), (512, 4), (2048, 16))
N_GROUPS = len(DIL_PATTERN)
HEADS_PER_GROUP = 4
ATT_HEAD_DIM = 128
N_EXPERTS = 8
LN_EPS = 1e-5

LANES = 128
SUBLANES = 8
BF16_ROWS = 2 * SUBLANES
SMEM_1D_TILE = 1024
VMEM_LIMIT = 56 * 1024 * 1024

MM_TM = 1024
MM_TN = 2048
LN_TM = 512
FFN_TM = 1024
FFN_TF = 512
EXP_TF = 512
RET_ROWS = 1024
ATT_Q = 128
ATT_QB = 1024
ATT_OUT_W = HEADS_PER_GROUP * ATT_HEAD_DIM + LANES
MOE_BLOCK = 896
ROUTE_TB = 512
COMB_TB = 1024
DMA_UNROLL = 8
NEG_BIG = -1e30


def _params(*sem):
    return pltpu.CompilerParams(dimension_semantics=sem, vmem_limit_bytes=VMEM_LIMIT)


def _layer_norm(y, g, b):
    mu = jnp.mean(y, axis=-1, keepdims=True)
    d = y - mu
    var = jnp.mean(d * d, axis=-1, keepdims=True)
    return d * lax.rsqrt(var + LN_EPS) * g + b


def _store_row_tiles(ref, y):
    m = y.shape[0]
    for c in range(SUBLANES):
        ref[pl.ds(c, m, stride=SUBLANES), :] = y[:, c * LANES:(c + 1) * LANES]


def _load_row_tiles(ref, m):
    return jnp.concatenate([ref[pl.ds(c, m, stride=SUBLANES), :] for c in range(SUBLANES)], axis=-1)


def _mm_kernel(x_ref, w_ref, o_ref, xb_ref):
    @pl.when(pl.program_id(1) == 0)
    def _():
        xb_ref[...] = x_ref[...].astype(BF16)

    o_ref[...] = jnp.dot(xb_ref[...], w_ref[...], preferred_element_type=F32).astype(o_ref.dtype)


def _matmul(x, w, name):
    M, K = x.shape
    N = w.shape[1]
    tm = min(MM_TM, M)
    tn = max(t for t in range(LANES, MM_TN + 1, LANES) if N % t == 0)
    assert M % tm == 0
    return pl.pallas_call(
        _mm_kernel,
        grid=(M // tm, N // tn),
        in_specs=[pl.BlockSpec((tm, K), lambda i, j: (i, 0)),
                  pl.BlockSpec((K, tn), lambda i, j: (0, j))],
        out_specs=pl.BlockSpec((tm, tn), lambda i, j: (i, j)),
        out_shape=jax.ShapeDtypeStruct((M, N), BF16),
        scratch_shapes=[pltpu.VMEM((tm, K), BF16)],
        compiler_params=_params("parallel", "arbitrary"),
        name=name,
    )(x, w)


def _ret_kernel(q_ref, k_ref, v_ref, g_ref, cos_ref, sin_ref, dec_ref, qdec_ref, kdec_ref,
                cdec_ref, gain_ref, o_ref, state_ref, *, n_chunks, dk):
    C, H = RET_CHUNK, RET_HEADS
    half = dk // 2
    dv = 2 * dk

    @pl.when(pl.program_id(1) == 0)
    def _():
        state_ref[...] = jnp.zeros_like(state_ref)

    def rot(t, cos, sin):
        t1, t2 = t[:, :half], t[:, half:]
        return jnp.concatenate([t1 * cos - t2 * sin, t2 * cos + t1 * sin], axis=-1)

    def chunk(c, carry):
        rows = pl.ds(pl.multiple_of(c * C, C), C)
        cos, sin = cos_ref[rows, :], sin_ref[rows, :]
        for h in range(H):
            kc = slice(h * dk, (h + 1) * dk)
            vc = slice(h * dv, (h + 1) * dv)
            q = rot(q_ref[rows, kc].astype(F32), cos, sin)
            k = rot(k_ref[rows, kc].astype(F32), cos, sin) * (dk ** -0.5)
            v = v_ref[rows, vc]
            scores = lax.dot_general(q.astype(BF16), k.astype(BF16), (((1,), (1,)), ((), ())),
                                     preferred_element_type=F32) * dec_ref[h]
            intra = jnp.dot(scores.astype(BF16), v, preferred_element_type=F32)
            state = state_ref[h]
            cross = jnp.dot((q * qdec_ref[h]).astype(BF16), state.astype(BF16),
                            preferred_element_type=F32)
            kn_t = (k * kdec_ref[h]).T.astype(BF16)
            state_ref[h] = cdec_ref[h] * state + jnp.dot(kn_t, v, preferred_element_type=F32)
            r = intra + cross
            mu = jnp.mean(r, axis=-1, keepdims=True)
            d = r - mu
            var = jnp.mean(d * d, axis=-1, keepdims=True)
            normed = d * lax.rsqrt(var + LN_EPS) * gain_ref[:, vc]
            gate = g_ref[rows, vc].astype(F32)
            o_ref[rows, vc] = (gate * jax.nn.sigmoid(gate) * normed).astype(o_ref.dtype)
        return carry

    lax.fori_loop(0, n_chunks, chunk, 0)


def _retention_core(proj, B, S, gn_gain):
    H, C = RET_HEADS, RET_CHUNK
    cols = proj.shape[1]
    dk = cols // (6 * H)
    dv = 2 * dk
    rb = min(RET_ROWS, S)
    assert S % rb == 0 and rb % C == 0
    nr = S // rb

    half = dk // 2
    inv = ROPE_BASE ** (-jnp.arange(half, dtype=F32) / half)
    ang = jnp.arange(S).astype(F32)[:, None] * inv[None, :]
    cos, sin = jnp.cos(ang), jnp.sin(ang)
    log_gamma = jnp.log(1.0 - 2.0 ** (-5.0 - jnp.arange(H, dtype=F32)))
    idx = jnp.arange(C, dtype=F32)
    rel = idx[:, None] - idx[None, :]
    decay_intra = jnp.where(rel >= 0, jnp.exp(log_gamma[:, None, None] * jnp.maximum(rel, 0.0)), 0.0)
    q_dec = jnp.exp(log_gamma[None, :] * (idx[:, None] + 1.0))
    k_dec = jnp.exp(log_gamma[None, :] * (C - 1.0 - idx[:, None]))
    chunk_dec = jnp.exp(log_gamma * C)
    qdec_b = jnp.broadcast_to(q_dec.T[:, :, None], (H, C, dk))
    kdec_b = jnp.broadcast_to(k_dec.T[:, :, None], (H, C, dk))
    cdec_b = jnp.broadcast_to(chunk_dec[:, None, None], (H, 1, dv))
    gain = gn_gain.reshape(1, H * dv)

    full = lambda shape: pl.BlockSpec(shape, lambda b, i: (0,) * len(shape))
    kern = functools.partial(_ret_kernel, n_chunks=rb // C, dk=dk)
    return pl.pallas_call(
        kern,
        grid=(B, nr),
        in_specs=[
            pl.BlockSpec((rb, H * dk), lambda b, i: (b * nr + i, 0)),
            pl.BlockSpec((rb, H * dk), lambda b, i: (b * nr + i, 1)),
            pl.BlockSpec((rb, H * dv), lambda b, i: (b * nr + i, 1)),
            pl.BlockSpec((rb, H * dv), lambda b, i: (b * nr + i, 2)),
            pl.BlockSpec((rb, half), lambda b, i: (i, 0)),
            pl.BlockSpec((rb, half), lambda b, i: (i, 0)),
            full((H, C, C)),
            full((H, C, dk)),
            full((H, C, dk)),
            full((H, 1, dv)),
            full((1, H * dv)),
        ],
        out_specs=pl.BlockSpec((rb, H * dv), lambda b, i: (b * nr + i, 0)),
        out_shape=jax.ShapeDtypeStruct((B * S, H * dv), BF16),
        scratch_shapes=[pltpu.VMEM((H, dk, dv), F32)],
        compiler_params=_params("parallel", "arbitrary"),
        name="retention_core",
    )(proj, proj, proj, proj, cos, sin, decay_intra, qdec_b, kdec_b, cdec_b, gain)


def _mm_res_ln_kernel(a_ref, w_ref, x_ref, g_ref, b_ref, o_ref, *, alpha):
    y = jnp.dot(a_ref[...], w_ref[...], preferred_element_type=F32)
    o_ref[...] = _layer_norm(alpha * x_ref[...] + y, g_ref[...], b_ref[...])


def _mm_res_ln(a, w, x, g, b, alpha, name):
    M, K = a.shape
    D = w.shape[1]
    tm = min(LN_TM, M)
    assert M % tm == 0
    return pl.pallas_call(
        functools.partial(_mm_res_ln_kernel, alpha=alpha),
        grid=(M // tm,),
        in_specs=[pl.BlockSpec((tm, K), lambda i: (i, 0)),
                  pl.BlockSpec((K, D), lambda i: (0, 0)),
                  pl.BlockSpec((tm, D), lambda i: (i, 0)),
                  pl.BlockSpec((1, D), lambda i: (0, 0)),
                  pl.BlockSpec((1, D), lambda i: (0, 0))],
        out_specs=pl.BlockSpec((tm, D), lambda i: (i, 0)),
        out_shape=jax.ShapeDtypeStruct((M, D), F32),
        compiler_params=_params("parallel"),
        name=name,
    )(a, w, x, g.reshape(1, D), b.reshape(1, D))


def _ffn_kernel(x_ref, wg_ref, wu_ref, wd_ref, g_ref, b_ref, o_ref, oc_ref, xb_ref, acc_ref, *, alpha):
    f = pl.program_id(1)

    @pl.when(f == 0)
    def _():
        xb_ref[...] = x_ref[...].astype(BF16)
        acc_ref[...] = jnp.zeros_like(acc_ref)

    xb = xb_ref[...]
    gate = jnp.dot(xb, wg_ref[...], preferred_element_type=F32)
    up = jnp.dot(xb, wu_ref[...], preferred_element_type=F32)
    hid = (gate * jax.nn.sigmoid(gate) * up).astype(BF16)
    acc_ref[...] += jnp.dot(hid, wd_ref[...], preferred_element_type=F32)

    @pl.when(f == pl.num_programs(1) - 1)
    def _():
        y = _layer_norm(alpha * x_ref[...] + acc_ref[...], g_ref[...], b_ref[...])
        o_ref[...] = y
        for c in range(oc_ref.shape[0]):
            oc_ref[c] = y[:, c * LANES:(c + 1) * LANES]


def _ffn_res_ln(x, wg, wu, wd, g, b, alpha):
    M, D = x.shape
    F = wg.shape[1]
    tm, tf = min(FFN_TM, M), FFN_TF
    assert M % tm == 0 and F % tf == 0 and D % LANES == 0
    nc = D // LANES
    return pl.pallas_call(
        functools.partial(_ffn_kernel, alpha=alpha),
        grid=(M // tm, F // tf),
        in_specs=[pl.BlockSpec((tm, D), lambda i, f: (i, 0)),
                  pl.BlockSpec((D, tf), lambda i, f: (0, f)),
                  pl.BlockSpec((D, tf), lambda i, f: (0, f)),
                  pl.BlockSpec((tf, D), lambda i, f: (f, 0)),
                  pl.BlockSpec((1, D), lambda i, f: (0, 0)),
                  pl.BlockSpec((1, D), lambda i, f: (0, 0))],
        out_specs=[pl.BlockSpec((tm, D), lambda i, f: (i, 0)),
                   pl.BlockSpec((nc, tm, LANES), lambda i, f: (0, i, 0))],
        out_shape=[jax.ShapeDtypeStruct((M, D), F32),
                   jax.ShapeDtypeStruct((nc, M, LANES), F32)],
        scratch_shapes=[pltpu.VMEM((tm, D), BF16), pltpu.VMEM((tm, D), F32)],
        compiler_params=_params("parallel", "arbitrary"),
        name="ffn_res_ln",
    )(x, wg, wu, wd, g.reshape(1, D), b.reshape(1, D))


def _mm_residue_kernel(xc_ref, w_ref, o_ref, lhs_ref, *, dil):
    nc, tm, _ = xc_ref.shape
    n = tm // dil
    N = w_ref.shape[1]
    for r in range(dil):
        for c in range(nc):
            lhs_ref[r * n:(r + 1) * n, c * LANES:(c + 1) * LANES] = (
                xc_ref[c, pl.ds(r, n, stride=dil), :].astype(BF16))
    res = jnp.dot(lhs_ref[...], w_ref[...], preferred_element_type=F32).astype(o_ref.dtype)
    for r in range(dil):
        o_ref[:, r * N:(r + 1) * N] = res[r * n:(r + 1) * n, :]


def _matmul_residue_view(xc, w, dil, name):
    nc, M, _ = xc.shape
    K, N = w.shape
    tm = min(MM_TM, M)
    assert M % tm == 0 and tm % (BF16_ROWS * dil) == 0 and nc * LANES == K
    return pl.pallas_call(
        functools.partial(_mm_residue_kernel, dil=dil),
        grid=(M // tm,),
        in_specs=[pl.BlockSpec((nc, tm, LANES), lambda i: (0, i, 0)),
                  pl.BlockSpec((K, N), lambda i: (0, 0))],
        out_specs=pl.BlockSpec((tm // dil, dil * N), lambda i: (i, 0)),
        out_shape=jax.ShapeDtypeStruct((M // dil, dil * N), BF16),
        scratch_shapes=[pltpu.VMEM((tm, K), BF16)],
        compiler_params=_params("parallel"),
        name=name,
    )(xc, w)


def _attn_kernel(q_ref, kp_ref, kc_ref, vp_ref, vc_ref, o_ref):
    Q, dh = ATT_Q, ATT_HEAD_DIM
    n_sub = q_ref.shape[0] // Q
    qi = pl.program_id(2)
    row = lax.broadcasted_iota(jnp.int32, (Q, 2 * Q), 0)
    col = lax.broadcasted_iota(jnp.int32, (Q, 2 * Q), 1)
    band = (col >= row) & (col <= row + Q)
    first = jnp.where(qi > 0, 0, Q)
    lane = lax.broadcasted_iota(jnp.int32, (Q, LANES), 1)
    for j in range(n_sub):
        rs = slice(j * Q, (j + 1) * Q)
        ks = slice((j - 1) * Q, (j + 1) * Q)
        valid = (band & (col >= first)) if j == 0 else band
        lse_slab = jnp.zeros((Q, LANES), F32)
        for h in range(HEADS_PER_GROUP):
            cs = slice(h * dh, (h + 1) * dh)
            if j == 0:
                k = jnp.concatenate([kp_ref[:, cs], kc_ref[:Q, cs]], axis=0)
                v = jnp.concatenate([vp_ref[:, cs], vc_ref[:Q, cs]], axis=0)
            else:
                k, v = kc_ref[ks, cs], vc_ref[ks, cs]
            s = lax.dot_general(q_ref[rs, cs], k, (((1,), (1,)), ((), ())),
                                preferred_element_type=F32) * (dh ** -0.5)
            s = jnp.where(valid, s, NEG_BIG)
            m = jnp.max(s, axis=-1, keepdims=True)
            p = jnp.exp(s - m)
            l = jnp.sum(p, axis=-1, keepdims=True)
            o = jnp.dot(p.astype(BF16), v, preferred_element_type=F32)
            o_ref[rs, cs] = o / l
            lse_slab = jnp.where(lane == h, m + jnp.log(l), lse_slab)
        o_ref[rs, HEADS_PER_GROUP * dh:] = lse_slab


def _dilated_group(qkv_view, B, S, gi, win, dil):
    Q = ATT_Q
    gw = HEADS_PER_GROUP * ATT_HEAD_DIM
    nblk = qkv_view.shape[1] // (dil * gw)
    sd = S // dil
    qb = min(ATT_QB, sd)
    assert win // dil == Q and S % dil == 0 and sd % qb == 0 and qb % Q == 0
    n_sub = qb // Q
    view = qkv_view.reshape(B, sd, qkv_view.shape[1])
    cur = lambda comp: pl.BlockSpec((None, qb, gw), lambda b, r, qi: (b, qi, r * nblk + comp))
    prev = lambda comp: pl.BlockSpec(
        (None, Q, gw), lambda b, r, qi: (b, jnp.maximum(qi * n_sub - 1, 0), r * nblk + comp))
    out = pl.pallas_call(
        _attn_kernel,
        grid=(B, dil, sd // qb),
        in_specs=[cur(0), prev(1), cur(1), prev(2), cur(2)],
        out_specs=pl.BlockSpec((None, qb, ATT_OUT_W), lambda b, r, qi: (b, qi, r)),
        out_shape=jax.ShapeDtypeStruct((B, sd, dil * ATT_OUT_W), F32),
        compiler_params=_params("parallel", "parallel", "arbitrary"),
        name=f"dilated_attn_g{gi}",
    )(view, view, view, view, view)
    return out.reshape(B * sd, dil * ATT_OUT_W)


def _attn_merge_kernel(o0_ref, o1_ref, o2_ref, w_ref, x_ref, g_ref, b_ref, o_ref, ot_ref, *scratch, alpha):
    dh = ATT_HEAD_DIM
    n_slab = ATT_OUT_W // LANES
    tm = x_ref.shape[0]
    groups = []
    scratch = list(scratch)
    for ref, (_, dil) in zip((o0_ref, o1_ref, o2_ref), DIL_PATTERN):
        if dil == 1:
            groups.append(lambda c, ref=ref: ref[:, c * LANES:(c + 1) * LANES])
            continue
        scr = scratch.pop(0)
        n = tm // dil
        for r in range(dil):
            for c in range(n_slab):
                col = r * ATT_OUT_W + c * LANES
                scr[c, pl.ds(r, n, stride=dil), :] = ref[:, col:col + LANES]
        groups.append(lambda c, scr=scr: scr[c])
    lses = [grp(n_slab - 1) for grp in groups]
    mx = jnp.maximum(jnp.maximum(lses[0], lses[1]), lses[2])
    ws = [jnp.exp(l - mx) for l in lses]
    den = ws[0] + ws[1] + ws[2]
    ws = [w / den for w in ws]
    heads = []
    for h in range(HEADS_PER_GROUP):
        acc = ws[0][:, h:h + 1] * groups[0](h)
        acc += ws[1][:, h:h + 1] * groups[1](h)
        acc += ws[2][:, h:h + 1] * groups[2](h)
        heads.append(acc)
    merged = jnp.concatenate(heads, axis=-1).astype(BF16)
    y = jnp.dot(merged, w_ref[...], preferred_element_type=F32)
    out = _layer_norm(alpha * x_ref[...] + y, g_ref[...], b_ref[...])
    o_ref[...] = out
    _store_row_tiles(ot_ref, out)


def _attn_merge_res_ln(outs, w, x, g, b, alpha):
    M, D = x.shape
    K = w.shape[0]
    tm = min(LN_TM, M)
    dils = [dil for _, dil in DIL_PATTERN]
    assert M % tm == 0 and all(tm % (8 * dil) == 0 for dil in dils) and D == SUBLANES * LANES
    ospecs = [pl.BlockSpec((tm // dil, dil * ATT_OUT_W), lambda i: (i, 0)) for dil in dils]
    return pl.pallas_call(
        functools.partial(_attn_merge_kernel, alpha=alpha),
        grid=(M // tm,),
        in_specs=ospecs + [
                  pl.BlockSpec((K, D), lambda i: (0, 0)),
                  pl.BlockSpec((tm, D), lambda i: (i, 0)),
                  pl.BlockSpec((1, D), lambda i: (0, 0)),
                  pl.BlockSpec((1, D), lambda i: (0, 0))],
        out_specs=[pl.BlockSpec((tm, D), lambda i: (i, 0)),
                   pl.BlockSpec((tm * SUBLANES, LANES), lambda i: (i, 0))],
        out_shape=[jax.ShapeDtypeStruct((M, D), F32),
                   jax.ShapeDtypeStruct((M * SUBLANES, LANES), F32)],
        scratch_shapes=[pltpu.VMEM((ATT_OUT_W // LANES, tm, LANES), F32) for dil in dils if dil > 1],
        compiler_params=_params("parallel"),
        name="attn_merge_res_ln",
    )(*outs, w, x, g.reshape(1, D), b.reshape(1, D))


def _router_kernel(x_ref, wh_ref, wl_ref, slab_ref, cnt_ref, run_ref):
    tb = x_ref.shape[0]

    @pl.when(pl.program_id(0) == 0)
    def _():
        run_ref[...] = jnp.zeros_like(run_ref)

    x = x_ref[...]
    xh = x.astype(BF16)
    xl = (x - xh.astype(F32)).astype(BF16)
    logits = (jnp.dot(xh, wh_ref[...], preferred_element_type=F32)
              + jnp.dot(xh, wl_ref[...], preferred_element_type=F32)
              + jnp.dot(xl, wh_ref[...], preferred_element_type=F32))
    lane = lax.broadcasted_iota(jnp.int32, (tb, LANES), 1)
    logits = jnp.where(lane < N_EXPERTS, logits, NEG_BIG)
    m1 = jnp.max(logits, axis=-1, keepdims=True)
    i1 = jnp.min(jnp.where(logits == m1, lane, LANES), axis=-1, keepdims=True)
    rest = jnp.where(lane == i1, NEG_BIG, logits)
    m2 = jnp.max(rest, axis=-1, keepdims=True)
    i2 = jnp.min(jnp.where(rest == m2, lane, LANES), axis=-1, keepdims=True)
    e = jnp.exp(m2 - m1)
    g1 = 1.0 / (1.0 + e)
    g2 = e / (1.0 + e)
    onehot = jnp.where((lane == i1) | (lane == i2), 1.0, 0.0)
    r = lax.broadcasted_iota(jnp.int32, (tb, tb), 0)
    c = lax.broadcasted_iota(jnp.int32, (tb, tb), 1)
    lower = jnp.where(c < r, 1.0, 0.0).astype(BF16)
    before = jnp.dot(lower, onehot.astype(BF16), preferred_element_type=F32) + run_ref[...]
    rank1 = jnp.sum(jnp.where(lane == i1, before, 0.0), axis=-1, keepdims=True)
    rank2 = jnp.sum(jnp.where(lane == i2, before, 0.0), axis=-1, keepdims=True)
    total = run_ref[...] + jnp.sum(onehot, axis=0, keepdims=True)
    run_ref[...] = total
    cnt_ref[...] = total
    slab = jnp.where(lane == 0, i1.astype(F32), 0.0)
    slab = jnp.where(lane == 1, i2.astype(F32), slab)
    slab = jnp.where(lane == 2, g1, slab)
    slab = jnp.where(lane == 3, g2, slab)
    slab = jnp.where(lane == 4, rank1, slab)
    slab = jnp.where(lane == 5, rank2, slab)
    slab_ref[...] = slab


def _router(x, w_router):
    T, D = x.shape
    tb = min(ROUTE_TB, T)
    assert T % tb == 0
    w = jnp.zeros((D, LANES), F32).at[:, :N_EXPERTS].set(w_router)
    wh = w.astype(BF16)
    wl = (w - wh.astype(F32)).astype(BF16)
    return pl.pallas_call(
        _router_kernel,
        grid=(T // tb,),
        in_specs=[pl.BlockSpec((tb, D), lambda i: (i, 0)),
                  pl.BlockSpec((D, LANES), lambda i: (0, 0)),
                  pl.BlockSpec((D, LANES), lambda i: (0, 0))],
        out_specs=[pl.BlockSpec((tb, LANES), lambda i: (i, 0)),
                   pl.BlockSpec((1, LANES), lambda i: (0, 0))],
        out_shape=[jax.ShapeDtypeStruct((T, LANES), F32),
                   jax.ShapeDtypeStruct((1, LANES), F32)],
        scratch_shapes=[pltpu.VMEM((1, LANES), F32)],
        compiler_params=_params("arbitrary"),
        name="moe_router",
    )(x, wh, wl)


def _expert_kernel(be_ref, na_ref, tok_ref, tok_nxt_ref, dst_prv_ref, dst_ref, x_hbm,
                   wg_ref, wu_ref, wd_ref, y_hbm, xbuf_ref, xb_ref, acc_ref, ybuf_ref,
                   sem_in, sem_out, *, n_f):
    del be_ref
    i, f = pl.program_id(0), pl.program_id(1)
    n_active = na_ref[0]
    tm = xb_ref.shape[0]
    per = tm // n_f
    slot = lax.rem(i, 2)

    def tile(row):
        return pl.ds(pl.multiple_of(row * SUBLANES, SUBLANES), SUBLANES)

    def gather(tok, t, s):
        return pltpu.make_async_copy(x_hbm.at[tile(tok)], xbuf_ref.at[s, tile(t)], sem_in.at[s])

    def scatter(dst, t, s):
        return pltpu.make_async_copy(ybuf_ref.at[s, tile(t)], y_hbm.at[tile(dst)], sem_out.at[s])

    def gathered(s):
        return pltpu.make_async_copy(x_hbm.at[pl.ds(0, tm * SUBLANES)], xbuf_ref.at[s], sem_in.at[s])

    def scattered(s):
        return pltpu.make_async_copy(ybuf_ref.at[s], y_hbm.at[pl.ds(0, tm * SUBLANES)], sem_out.at[s])

    def for_rows(fn):
        def body(t, carry):
            fn(t)
            return carry
        lax.fori_loop(0, tm, body, 0, unroll=DMA_UNROLL)

    @pl.when(i < n_active)
    def _():
        @pl.when((i == 0) & (f == 0))
        def _():
            for_rows(lambda t: gather(tok_ref[0, 0, t], t, 0).start())
            ybuf_ref[1] = jnp.zeros(ybuf_ref.shape[1:], F32)

        @pl.when(f == 0)
        def _():
            gathered(slot).wait()
            xb_ref[...] = _load_row_tiles(xbuf_ref.at[slot], tm).astype(BF16)
            acc_ref[...] = jnp.zeros_like(acc_ref)

        base = f * per
        for t in range(per):
            gather(tok_nxt_ref[0, 0, base + t], base + t, 1 - slot).start()
            scatter(dst_prv_ref[0, 0, base + t], base + t, 1 - slot).start()
        xb = xb_ref[...]
        gate = jnp.dot(xb, wg_ref[...], preferred_element_type=F32)
        up = jnp.dot(xb, wu_ref[...], preferred_element_type=F32)
        hid = (gate * jax.nn.sigmoid(gate) * up).astype(BF16)
        acc_ref[...] += jnp.dot(hid, wd_ref[...], preferred_element_type=F32)

        @pl.when(f == n_f - 1)
        def _():
            @pl.when(i > 0)
            def _():
                scattered(slot).wait()
            _store_row_tiles(ybuf_ref.at[slot], acc_ref[...])

        @pl.when((i == n_active - 1) & (f == n_f - 1))
        def _():
            for_rows(lambda t: scatter(dst_ref[0, 0, t], t, slot).start())
            gathered(1 - slot).wait()
            scattered(1 - slot).wait()
            scattered(slot).wait()

    @pl.when((i >= n_active) & (f == n_f - 1))
    def _():
        ybuf_ref[0] = jnp.zeros(ybuf_ref.shape[1:], F32)
        for_rows(lambda t: scatter(dst_ref[0, 0, t], t, 0).start())
        scattered(0).wait()


def _experts(x, slot_tok, slot_dst, n_rows_out, blk_exp, n_active, wg, wu, wd, layer):
    D = x.shape[1] * SUBLANES
    P = slot_tok.shape[0]
    F = wg.shape[3]
    tm, tf = MOE_BLOCK, EXP_TF
    assert P % tm == 0 and F % tf == 0 and tm % (F // tf) == 0
    nb, nf = P // tm, F // tf
    toks = slot_tok.reshape(nb, 1, tm)
    spare = n_rows_out - tm + jnp.arange(tm, dtype=jnp.int32)
    dsts = jnp.concatenate([spare, slot_dst]).reshape(nb + 1, 1, tm)
    smem = lambda fn: pl.BlockSpec((1, 1, tm), fn, memory_space=pltpu.SMEM)
    grid_spec = pltpu.PrefetchScalarGridSpec(
        num_scalar_prefetch=2,
        grid=(nb, nf),
        in_specs=[smem(lambda i, f, be, na: (i, 0, 0)),
                  smem(lambda i, f, be, na: (jnp.minimum(i + 1, nb - 1), 0, 0)),
                  smem(lambda i, f, be, na: (i, 0, 0)),
                  smem(lambda i, f, be, na: (i + 1, 0, 0)),
                  pl.BlockSpec(memory_space=pl.ANY),
                  pl.BlockSpec((None, None, D, tf), lambda i, f, be, na: (layer, be[i], 0, f)),
                  pl.BlockSpec((None, None, D, tf), lambda i, f, be, na: (layer, be[i], 0, f)),
                  pl.BlockSpec((None, None, tf, D), lambda i, f, be, na: (layer, be[i], f, 0))],
        out_specs=pl.BlockSpec(memory_space=pl.ANY),
        scratch_shapes=[pltpu.VMEM((2, tm * SUBLANES, LANES), F32), pltpu.VMEM((tm, D), BF16),
                        pltpu.VMEM((tm, D), F32), pltpu.VMEM((2, tm * SUBLANES, LANES), F32),
                        pltpu.SemaphoreType.DMA((2,)), pltpu.SemaphoreType.DMA((2,))],
    )
    return pl.pallas_call(
        functools.partial(_expert_kernel, n_f=nf),
        grid_spec=grid_spec,
        out_shape=jax.ShapeDtypeStruct((n_rows_out * SUBLANES, LANES), F32),
        compiler_params=_params("arbitrary", "arbitrary"),
        name="moe_experts",
    )(blk_exp, n_active, toks, toks, dsts, dsts, x, wg, wu, wd)


def _combine_kernel(y1_ref, y2_ref, slab_ref, x_ref, g_ref, b_ref, o_ref, *, alpha):
    slab = slab_ref[...]
    tb = x_ref.shape[0]
    mix = slab[:, 2:3] * _load_row_tiles(y1_ref, tb) + slab[:, 3:4] * _load_row_tiles(y2_ref, tb)
    o_ref[...] = _layer_norm(alpha * x_ref[...] + mix, g_ref[...], b_ref[...])


def _combine_res_ln(y, slab, x, g, b, alpha):
    T, D = x.shape
    tb = min(COMB_TB, T)
    assert T % tb == 0
    nb = T // tb
    return pl.pallas_call(
        functools.partial(_combine_kernel, alpha=alpha),
        grid=(nb,),
        in_specs=[pl.BlockSpec((tb * SUBLANES, LANES), lambda i: (i, 0)),
                  pl.BlockSpec((tb * SUBLANES, LANES), lambda i: (i + nb, 0)),
                  pl.BlockSpec((tb, LANES), lambda i: (i, 0)),
                  pl.BlockSpec((tb, D), lambda i: (i, 0)),
                  pl.BlockSpec((1, D), lambda i: (0, 0)),
                  pl.BlockSpec((1, D), lambda i: (0, 0))],
        out_specs=pl.BlockSpec((tb, D), lambda i: (i, 0)),
        out_shape=jax.ShapeDtypeStruct((T, D), F32),
        compiler_params=_params("parallel"),
        name="moe_combine_res_ln",
    )(y, y, slab, x, g.reshape(1, D), b.reshape(1, D))


def _slot_table_kernel(d1_ref, d2_ref, fill_end_ref, pad_end_ref, o_ref):
    T = d1_ref.shape[0]

    def place(t, carry):
        o_ref[d1_ref[t]] = t
        o_ref[d2_ref[t]] = T + t
        return carry

    lax.fori_loop(0, T, place, 0, unroll=DMA_UNROLL)

    spare = jnp.int32(2 * T)
    bounds = [(fill_end_ref[e], pad_end_ref[e]) for e in range(N_EXPERTS)]
    bounds.append((pad_end_ref[N_EXPERTS - 1], jnp.int32(o_ref.shape[0])))
    for lo, hi in bounds:
        def fill(s, carry, lo=lo, first=spare):
            o_ref[s] = first + (s - lo)
            return carry
        lax.fori_loop(lo, hi, fill, 0)
        spare = spare + (hi - lo)


def _slot_table(d1, d2, fill_end, pad_end, n_slots):
    smem = pl.BlockSpec(memory_space=pltpu.SMEM)
    n_pad = -(-n_slots // SMEM_1D_TILE) * SMEM_1D_TILE
    assert d1.shape[0] % SMEM_1D_TILE == 0
    return pl.pallas_call(
        _slot_table_kernel,
        in_specs=[smem, smem, smem, smem],
        out_specs=smem,
        out_shape=jax.ShapeDtypeStruct((n_pad,), jnp.int32),
        name="moe_slot_table",
    )(d1, d2, fill_end.astype(jnp.int32), pad_end.astype(jnp.int32))[:n_slots]


def _moe_res_ln(x, x_tiles, w_router, wg, wu, wd, layer, g, b, alpha):
    T, D = x.shape
    slab, counts = _router(x, w_router)
    counts = counts[0, :N_EXPERTS].astype(jnp.int32)
    padded = ((counts + MOE_BLOCK - 1) // MOE_BLOCK) * MOE_BLOCK
    pad_end = jnp.cumsum(padded)
    pad_start = pad_end - padded
    e1 = slab[:, 0].astype(jnp.int32)
    e2 = slab[:, 1].astype(jnp.int32)
    d1 = pad_start[e1] + slab[:, 4].astype(jnp.int32)
    d2 = pad_start[e2] + slab[:, 5].astype(jnp.int32)
    nblk = -(-(2 * T) // MOE_BLOCK) + N_EXPERTS
    n_slots = nblk * MOE_BLOCK
    blk_start = jnp.arange(nblk, dtype=jnp.int32) * MOE_BLOCK
    blk_exp = jnp.minimum(jnp.sum(pad_end[None, :] <= blk_start[:, None], axis=1), N_EXPERTS - 1)
    n_active = (pad_end[-1:] // MOE_BLOCK).astype(jnp.int32)
    slot_dst = _slot_table(d1, d2, pad_start + counts, pad_end, n_slots)
    slot_tok = jnp.where(slot_dst < 2 * T, slot_dst % T, 0)
    n_rows_out = n_slots + MOE_BLOCK
    y = _experts(x_tiles, slot_tok, slot_dst, n_rows_out, blk_exp.astype(jnp.int32), n_active,
                 wg, wu, wd, layer)
    return _combine_res_ln(y, slab, x, g, b, alpha)


def kernel(x, ln_gain, ln_bias, ret_w_in, ret_gn_gain, ret_w_out, att_w_qkv, att_w_out,
           ffn_w_gate, ffn_w_up, ffn_w_down, moe_w_router, moe_w_gate, moe_w_up, moe_w_down):
    B, S, D = x.shape
    depth = ln_gain.shape[0]
    alpha = (2 * depth) ** 0.25
    moe_wg, moe_wu, moe_wd = (w.astype(BF16) for w in (moe_w_gate, moe_w_up, moe_w_down))
    h = x.reshape(B * S, D)
    h_blocks = None
    for i in range(depth):
        j = i // 2
        if i % 2 == 0:
            proj = _matmul(h, ret_w_in[j].astype(BF16), "ret_in_proj")
            gated = _retention_core(proj, B, S, ret_gn_gain[j])
            h = _mm_res_ln(gated, ret_w_out[j].astype(BF16), h, ln_gain[i, 0], ln_bias[i, 0], alpha,
                           "ret_out_res_ln")
            h, h_blocks = _ffn_res_ln(h, ffn_w_gate[j].astype(BF16), ffn_w_up[j].astype(BF16),
                                      ffn_w_down[j].astype(BF16), ln_gain[i, 1], ln_bias[i, 1], alpha)
        else:
            gw = HEADS_PER_GROUP * ATT_HEAD_DIM
            outs = []
            for gi, (win, dil) in enumerate(DIL_PATTERN):
                w_g = jnp.concatenate(
                    [att_w_qkv[j][:, (c * N_GROUPS + gi) * gw:(c * N_GROUPS + gi + 1) * gw] for c in range(3)],
                    axis=1).astype(BF16)
                if dil == 1:
                    qkv = _matmul(h, w_g, f"att_qkv_proj_g{gi}")
                else:
                    if h_blocks is None:
                        h_blocks = jnp.transpose(h.reshape(B * S, D // LANES, LANES), (1, 0, 2))
                    qkv = _matmul_residue_view(h_blocks, w_g, dil, f"att_qkv_proj_g{gi}")
                outs.append(_dilated_group(qkv, B, S, gi, win, dil))
            h_blocks = None
            h, h_tiles = _attn_merge_res_ln(outs, att_w_out[j].astype(BF16), h, ln_gain[i, 0], ln_bias[i, 0],
                                            alpha)
            h = _moe_res_ln(h, h_tiles, moe_w_router[j], moe_wg, moe_wu, moe_wd, j,
                            ln_gain[i, 1], ln_bias[i, 1], alpha)
    return h.reshape(B, S, D)
```

```python
import functools

import jax
import jax.numpy as jnp
from jax import lax
from jax.experimental import pallas as pl
from jax.experimental.pallas import tpu as pltpu

F32 = jnp.float32
BF16 = jnp.bfloat16

RET_HEADS = 4
RET_CHUNK = 128
ROPE_BASE = 10000.0
DIL_PATTERN = ((128, 1), (512, 4), (2048, 16))
N_GROUPS = len(DIL_PATTERN)
HEADS_PER_GROUP = 4
ATT_HEAD_DIM = 128
N_EXPERTS = 8
LN_EPS = 1e-5

LANES = 128
SUBLANES = 8
BF16_ROWS = 2 * SUBLANES
SMEM_1D_TILE = 1024
VMEM_LIMIT = 56 * 1024 * 1024

MM_TM = 1024
MM_TN = 2048
LN_TM = 512
FFN_TM = 1024
FFN_TF = 512
EXP_TF = 512
RET_ROWS = 1024
ATT_Q = 128
ATT_QB = 2048
ATT_OUT_W = HEADS_PER_GROUP * ATT_HEAD_DIM + LANES
MOE_BLOCK = 896
ROUTE_TB = 512
COMB_TB = 1024
DMA_UNROLL = 8
NEG_BIG = -1e30


def _params(*sem):
    return pltpu.CompilerParams(dimension_semantics=sem, vmem_limit_bytes=VMEM_LIMIT)


def _layer_norm(y, g, b):
    mu = jnp.mean(y, axis=-1, keepdims=True)
    d = y - mu
    var = jnp.mean(d * d, axis=-1, keepdims=True)
    return d * lax.rsqrt(var + LN_EPS) * g + b


def _store_row_tiles(ref, y):
    m = y.shape[0]
    for c in range(SUBLANES):
        ref[pl.ds(c, m, stride=SUBLANES), :] = y[:, c * LANES:(c + 1) * LANES]


def _load_row_tiles(ref, m):
    return jnp.concatenate([ref[pl.ds(c, m, stride=SUBLANES), :] for c in range(SUBLANES)], axis=-1)


def _mm_kernel(x_ref, w_ref, o_ref, xb_ref):
    @pl.when(pl.program_id(1) == 0)
    def _():
        xb_ref[...] = x_ref[...].astype(BF16)

    o_ref[...] = jnp.dot(xb_ref[...], w_ref[...], preferred_element_type=F32).astype(o_ref.dtype)


def _matmul(x, w, name):
    M, K = x.shape
    N = w.shape[1]
    tm = min(MM_TM, M)
    tn = max(t for t in range(LANES, MM_TN + 1, LANES) if N % t == 0)
    assert M % tm == 0
    return pl.pallas_call(
        _mm_kernel,
        grid=(M // tm, N // tn),
        in_specs=[pl.BlockSpec((tm, K), lambda i, j: (i, 0)),
                  pl.BlockSpec((K, tn), lambda i, j: (0, j))],
        out_specs=pl.BlockSpec((tm, tn), lambda i, j: (i, j)),
        out_shape=jax.ShapeDtypeStruct((M, N), BF16),
        scratch_shapes=[pltpu.VMEM((tm, K), BF16)],
        compiler_params=_params("parallel", "arbitrary"),
        name=name,
    )(x, w)


def _ret_kernel(q_ref, k_ref, v_ref, g_ref, cos_ref, sin_ref, dec_ref, qdec_ref, kdec_ref,
                cdec_ref, gain_ref, o_ref, state_ref, *, n_chunks, dk):
    C, H = RET_CHUNK, RET_HEADS
    half = dk // 2
    dv = 2 * dk

    @pl.when(pl.program_id(1) == 0)
    def _():
        state_ref[...] = jnp.zeros_like(state_ref)

    def rot(t, cos, sin):
        t1, t2 = t[:, :half], t[:, half:]
        return jnp.concatenate([t1 * cos - t2 * sin, t2 * cos + t1 * sin], axis=-1)

    def chunk(c, carry):
        rows = pl.ds(pl.multiple_of(c * C, C), C)
        cos, sin = cos_ref[rows, :], sin_ref[rows, :]
        for h in range(H):
            kc = slice(h * dk, (h + 1) * dk)
            vc = slice(h * dv, (h + 1) * dv)
            q = rot(q_ref[rows, kc].astype(F32), cos, sin)
            k = rot(k_ref[rows, kc].astype(F32), cos, sin) * (dk ** -0.5)
            v = v_ref[rows, vc]
            scores = lax.dot_general(q.astype(BF16), k.astype(BF16), (((1,), (1,)), ((), ())),
                                     preferred_element_type=F32) * dec_ref[h]
            intra = jnp.dot(scores.astype(BF16), v, preferred_element_type=F32)
            state = state_ref[h]
            cross = jnp.dot((q * qdec_ref[h]).astype(BF16), state.astype(BF16),
                            preferred_element_type=F32)
            kn_t = (k * kdec_ref[h]).T.astype(BF16)
            state_ref[h] = cdec_ref[h] * state + jnp.dot(kn_t, v, preferred_element_type=F32)
            r = intra + cross
            mu = jnp.mean(r, axis=-1, keepdims=True)
            d = r - mu
            var = jnp.mean(d * d, axis=-1, keepdims=True)
            normed = d * lax.rsqrt(var + LN_EPS) * gain_ref[:, vc]
            gate = g_ref[rows, vc].astype(F32)
            o_ref[rows, vc] = (gate * jax.nn.sigmoid(gate) * normed).astype(o_ref.dtype)
        return carry

    lax.fori_loop(0, n_chunks, chunk, 0)


def _retention_core(proj, B, S, gn_gain):
    H, C = RET_HEADS, RET_CHUNK
    cols = proj.shape[1]
    dk = cols // (6 * H)
    dv = 2 * dk
    rb = min(RET_ROWS, S)
    assert S % rb == 0 and rb % C == 0
    nr = S // rb

    half = dk // 2
    inv = ROPE_BASE ** (-jnp.arange(half, dtype=F32) / half)
    ang = jnp.arange(S).astype(F32)[:, None] * inv[None, :]
    cos, sin = jnp.cos(ang), jnp.sin(ang)
    log_gamma = jnp.log(1.0 - 2.0 ** (-5.0 - jnp.arange(H, dtype=F32)))
    idx = jnp.arange(C, dtype=F32)
    rel = idx[:, None] - idx[None, :]
    decay_intra = jnp.where(rel >= 0, jnp.exp(log_gamma[:, None, None] * jnp.maximum(rel, 0.0)), 0.0)
    q_dec = jnp.exp(log_gamma[None, :] * (idx[:, None] + 1.0))
    k_dec = jnp.exp(log_gamma[None, :] * (C - 1.0 - idx[:, None]))
    chunk_dec = jnp.exp(log_gamma * C)
    qdec_b = jnp.broadcast_to(q_dec.T[:, :, None], (H, C, dk))
    kdec_b = jnp.broadcast_to(k_dec.T[:, :, None], (H, C, dk))
    cdec_b = jnp.broadcast_to(chunk_dec[:, None, None], (H, 1, dv))
    gain = gn_gain.reshape(1, H * dv)

    full = lambda shape: pl.BlockSpec(shape, lambda b, i: (0,) * len(shape))
    kern = functools.partial(_ret_kernel, n_chunks=rb // C, dk=dk)
    return pl.pallas_call(
        kern,
        grid=(B, nr),
        in_specs=[
            pl.BlockSpec((rb, H * dk), lambda b, i: (b * nr + i, 0)),
            pl.BlockSpec((rb, H * dk), lambda b, i: (b * nr + i, 1)),
            pl.BlockSpec((rb, H * dv), lambda b, i: (b * nr + i, 1)),
            pl.BlockSpec((rb, H * dv), lambda b, i: (b * nr + i, 2)),
            pl.BlockSpec((rb, half), lambda b, i: (i, 0)),
            pl.BlockSpec((rb, half), lambda b, i: (i, 0)),
            full((H, C, C)),
            full((H, C, dk)),
            full((H, C, dk)),
            full((H, 1, dv)),
            full((1, H * dv)),
        ],
        out_specs=pl.BlockSpec((rb, H * dv), lambda b, i: (b * nr + i, 0)),
        out_shape=jax.ShapeDtypeStruct((B * S, H * dv), BF16),
        scratch_shapes=[pltpu.VMEM((H, dk, dv), F32)],
        compiler_params=_params("parallel", "arbitrary"),
        name="retention_core",
    )(proj, proj, proj, proj, cos, sin, decay_intra, qdec_b, kdec_b, cdec_b, gain)


def _mm_res_ln_kernel(a_ref, w_ref, x_ref, g_ref, b_ref, o_ref, *, alpha):
    y = jnp.dot(a_ref[...], w_ref[...], preferred_element_type=F32)
    o_ref[...] = _layer_norm(alpha * x_ref[...] + y, g_ref[...], b_ref[...])


def _mm_res_ln(a, w, x, g, b, alpha, name):
    M, K = a.shape
    D = w.shape[1]
    tm = min(LN_TM, M)
    assert M % tm == 0
    return pl.pallas_call(
        functools.partial(_mm_res_ln_kernel, alpha=alpha),
        grid=(M // tm,),
        in_specs=[pl.BlockSpec((tm, K), lambda i: (i, 0)),
                  pl.BlockSpec((K, D), lambda i: (0, 0)),
                  pl.BlockSpec((tm, D), lambda i: (i, 0)),
                  pl.BlockSpec((1, D), lambda i: (0, 0)),
                  pl.BlockSpec((1, D), lambda i: (0, 0))],
        out_specs=pl.BlockSpec((tm, D), lambda i: (i, 0)),
        out_shape=jax.ShapeDtypeStruct((M, D), F32),
        compiler_params=_params("parallel"),
        name=name,
    )(a, w, x, g.reshape(1, D), b.reshape(1, D))


def _ffn_kernel(x_ref, wg_ref, wu_ref, wd_ref, g_ref, b_ref, o_ref, oc_ref, xb_ref, acc_ref, *, alpha):
    f = pl.program_id(1)

    @pl.when(f == 0)
    def _():
        xb_ref[...] = x_ref[...].astype(BF16)
        acc_ref[...] = jnp.zeros_like(acc_ref)

    xb = xb_ref[...]
    gate = jnp.dot(xb, wg_ref[...], preferred_element_type=F32)
    up = jnp.dot(xb, wu_ref[...], preferred_element_type=F32)
    hid = (gate * jax.nn.sigmoid(gate) * up).astype(BF16)
    acc_ref[...] += jnp.dot(hid, wd_ref[...], preferred_element_type=F32)

    @pl.when(f == pl.num_programs(1) - 1)
    def _():
        y = _layer_norm(alpha * x_ref[...] + acc_ref[...], g_ref[...], b_ref[...])
        o_ref[...] = y
        for c in range(oc_ref.shape[0]):
            oc_ref[c] = y[:, c * LANES:(c + 1) * LANES]


def _ffn_res_ln(x, wg, wu, wd, g, b, alpha):
    M, D = x.shape
    F = wg.shape[1]
    tm, tf = min(FFN_TM, M), FFN_TF
    assert M % tm == 0 and F % tf == 0 and D % LANES == 0
    nc = D // LANES
    return pl.pallas_call(
        functools.partial(_ffn_kernel, alpha=alpha),
        grid=(M // tm, F // tf),
        in_specs=[pl.BlockSpec((tm, D), lambda i, f: (i, 0)),
                  pl.BlockSpec((D, tf), lambda i, f: (0, f)),
                  pl.BlockSpec((D, tf), lambda i, f: (0, f)),
                  pl.BlockSpec((tf, D), lambda i, f: (f, 0)),
                  pl.BlockSpec((1, D), lambda i, f: (0, 0)),
                  pl.BlockSpec((1, D), lambda i, f: (0, 0))],
        out_specs=[pl.BlockSpec((tm, D), lambda i, f: (i, 0)),
                   pl.BlockSpec((nc, tm, LANES), lambda i, f: (0, i, 0))],
        out_shape=[jax.ShapeDtypeStruct((M, D), F32),
                   jax.ShapeDtypeStruct((nc, M, LANES), F32)],
        scratch_shapes=[pltpu.VMEM((tm, D), BF16), pltpu.VMEM((tm, D), F32)],
        compiler_params=_params("parallel", "arbitrary"),
        name="ffn_res_ln",
    )(x, wg, wu, wd, g.reshape(1, D), b.reshape(1, D))


def _mm_residue_kernel(xc_ref, w_ref, o_ref, lhs_ref, *, dil):
    nc, tm, _ = xc_ref.shape
    n = tm // dil
    N = w_ref.shape[1]
    for r in range(dil):
        for c in range(nc):
            lhs_ref[r * n:(r + 1) * n, c * LANES:(c + 1) * LANES] = (
                xc_ref[c, pl.ds(r, n, stride=dil), :].astype(BF16))
    res = jnp.dot(lhs_ref[...], w_ref[...], preferred_element_type=F32).astype(o_ref.dtype)
    for r in range(dil):
        o_ref[:, r * N:(r + 1) * N] = res[r * n:(r + 1) * n, :]


def _matmul_residue_view(xc, w, dil, name):
    nc, M, _ = xc.shape
    K, N = w.shape
    tm = min(MM_TM, M)
    assert M % tm == 0 and tm % (BF16_ROWS * dil) == 0 and nc * LANES == K
    return pl.pallas_call(
        functools.partial(_mm_residue_kernel, dil=dil),
        grid=(M // tm,),
        in_specs=[pl.BlockSpec((nc, tm, LANES), lambda i: (0, i, 0)),
                  pl.BlockSpec((K, N), lambda i: (0, 0))],
        out_specs=pl.BlockSpec((tm // dil, dil * N), lambda i: (i, 0)),
        out_shape=jax.ShapeDtypeStruct((M // dil, dil * N), BF16),
        scratch_shapes=[pltpu.VMEM((tm, K), BF16)],
        compiler_params=_params("parallel"),
        name=name,
    )(xc, w)


def _attn_kernel(q_ref, kp_ref, kc_ref, vp_ref, vc_ref, o_ref):
    Q, dh = ATT_Q, ATT_HEAD_DIM
    n_sub = q_ref.shape[0] // Q
    qi = pl.program_id(2)
    row = lax.broadcasted_iota(jnp.int32, (Q, 2 * Q), 0)
    col = lax.broadcasted_iota(jnp.int32, (Q, 2 * Q), 1)
    band = (col >= row) & (col <= row + Q)
    first = jnp.where(qi > 0, 0, Q)
    lane = lax.broadcasted_iota(jnp.int32, (Q, LANES), 1)
    for j in range(n_sub):
        rs = slice(j * Q, (j + 1) * Q)
        ks = slice((j - 1) * Q, (j + 1) * Q)
        valid = (band & (col >= first)) if j == 0 else band
        lse_slab = jnp.zeros((Q, LANES), F32)
        for h in range(HEADS_PER_GROUP):
            cs = slice(h * dh, (h + 1) * dh)
            if j == 0:
                k = jnp.concatenate([kp_ref[:, cs], kc_ref[:Q, cs]], axis=0)
                v = jnp.concatenate([vp_ref[:, cs], vc_ref[:Q, cs]], axis=0)
            else:
                k, v = kc_ref[ks, cs], vc_ref[ks, cs]
            s = lax.dot_general(q_ref[rs, cs], k, (((1,), (1,)), ((), ())),
                                preferred_element_type=F32) * (dh ** -0.5)
            s = jnp.where(valid, s, NEG_BIG)
            m = jnp.max(s, axis=-1, keepdims=True)
            p = jnp.exp(s - m)
            l = jnp.sum(p, axis=-1, keepdims=True)
            o = jnp.dot(p.astype(BF16), v, preferred_element_type=F32)
            o_ref[rs, cs] = o / l
            lse_slab = jnp.where(lane == h, m + jnp.log(l), lse_slab)
        o_ref[rs, HEADS_PER_GROUP * dh:] = lse_slab


def _dilated_group(qkv_view, B, S, gi, win, dil):
    Q = ATT_Q
    gw = HEADS_PER_GROUP * ATT_HEAD_DIM
    nblk = qkv_view.shape[1] // (dil * gw)
    sd = S // dil
    qb = min(ATT_QB, sd)
    assert win // dil == Q and S % dil == 0 and sd % qb == 0 and qb % Q == 0
    n_sub = qb // Q
    view = qkv_view.reshape(B, sd, qkv_view.shape[1])
    cur = lambda comp: pl.BlockSpec((None, qb, gw), lambda b, r, qi: (b, qi, r * nblk + comp))
    prev = lambda comp: pl.BlockSpec(
        (None, Q, gw), lambda b, r, qi: (b, jnp.maximum(qi * n_sub - 1, 0), r * nblk + comp))
    out = pl.pallas_call(
        _attn_kernel,
        grid=(B, dil, sd // qb),
        in_specs=[cur(0), prev(1), cur(1), prev(2), cur(2)],
        out_specs=pl.BlockSpec((None, qb, ATT_OUT_W), lambda b, r, qi: (b, qi, r)),
        out_shape=jax.ShapeDtypeStruct((B, sd, dil * ATT_OUT_W), F32),
        compiler_params=_params("parallel", "parallel", "arbitrary"),
        name=f"dilated_attn_g{gi}",
    )(view, view, view, view, view)
    return out.reshape(B * sd, dil * ATT_OUT_W)


def _attn_merge_kernel(o0_ref, o1_ref, o2_ref, w_ref, x_ref, g_ref, b_ref, o_ref, ot_ref, *scratch, alpha):
    dh = ATT_HEAD_DIM
    n_slab = ATT_OUT_W // LANES
    tm = x_ref.shape[0]
    groups = []
    scratch = list(scratch)
    for ref, (_, dil) in zip((o0_ref, o1_ref, o2_ref), DIL_PATTERN):
        if dil == 1:
            groups.append(lambda c, ref=ref: ref[:, c * LANES:(c + 1) * LANES])
            continue
        scr = scratch.pop(0)
        n = tm // dil
        for r in range(dil):
            for c in range(n_slab):
                col = r * ATT_OUT_W + c * LANES
                scr[c, pl.ds(r, n, stride=dil), :] = ref[:, col:col + LANES]
        groups.append(lambda c, scr=scr: scr[c])
    lses = [grp(n_slab - 1) for grp in groups]
    mx = jnp.maximum(jnp.maximum(lses[0], lses[1]), lses[2])
    ws = [jnp.exp(l - mx) for l in lses]
    den = ws[0] + ws[1] + ws[2]
    ws = [w / den for w in ws]
    heads = []
    for h in range(HEADS_PER_GROUP):
        acc = ws[0][:, h:h + 1] * groups[0](h)
        acc += ws[1][:, h:h + 1] * groups[1](h)
        acc += ws[2][:, h:h + 1] * groups[2](h)
        heads.append(acc)
    merged = jnp.concatenate(heads, axis=-1).astype(BF16)
    y = jnp.dot(merged, w_ref[...], preferred_element_type=F32)
    out = _layer_norm(alpha * x_ref[...] + y, g_ref[...], b_ref[...])
    o_ref[...] = out
    _store_row_tiles(ot_ref, out)


def _attn_merge_res_ln(outs, w, x, g, b, alpha):
    M, D = x.shape
    K = w.shape[0]
    tm = min(LN_TM, M)
    dils = [dil for _, dil in DIL_PATTERN]
    assert M % tm == 0 and all(tm % (8 * dil) == 0 for dil in dils) and D == SUBLANES * LANES
    ospecs = [pl.BlockSpec((tm // dil, dil * ATT_OUT_W), lambda i: (i, 0)) for dil in dils]
    return pl.pallas_call(
        functools.partial(_attn_merge_kernel, alpha=alpha),
        grid=(M // tm,),
        in_specs=ospecs + [
                  pl.BlockSpec((K, D), lambda i: (0, 0)),
                  pl.BlockSpec((tm, D), lambda i: (i, 0)),
                  pl.BlockSpec((1, D), lambda i: (0, 0)),
                  pl.BlockSpec((1, D), lambda i: (0, 0))],
        out_specs=[pl.BlockSpec((tm, D), lambda i: (i, 0)),
                   pl.BlockSpec((tm * SUBLANES, LANES), lambda i: (i, 0))],
        out_shape=[jax.ShapeDtypeStruct((M, D), F32),
                   jax.ShapeDtypeStruct((M * SUBLANES, LANES), F32)],
        scratch_shapes=[pltpu.VMEM((ATT_OUT_W // LANES, tm, LANES), F32) for dil in dils if dil > 1],
        compiler_params=_params("parallel"),
        name="attn_merge_res_ln",
    )(*outs, w, x, g.reshape(1, D), b.reshape(1, D))


def _router_kernel(x_ref, wh_ref, wl_ref, slab_ref, cnt_ref, run_ref):
    tb = x_ref.shape[0]

    @pl.when(pl.program_id(0) == 0)
    def _():
        run_ref[...] = jnp.zeros_like(run_ref)

    x = x_ref[...]
    xh = x.astype(BF16)
    xl = (x - xh.astype(F32)).astype(BF16)
    logits = (jnp.dot(xh, wh_ref[...], preferred_element_type=F32)
              + jnp.dot(xh, wl_ref[...], preferred_element_type=F32)
              + jnp.dot(xl, wh_ref[...], preferred_element_type=F32))
    lane = lax.broadcasted_iota(jnp.int32, (tb, LANES), 1)
    logits = jnp.where(lane < N_EXPERTS, logits, NEG_BIG)
    m1 = jnp.max(logits, axis=-1, keepdims=True)
    i1 = jnp.min(jnp.where(logits == m1, lane, LANES), axis=-1, keepdims=True)
    rest = jnp.where(lane == i1, NEG_BIG, logits)
    m2 = jnp.max(rest, axis=-1, keepdims=True)
    i2 = jnp.min(jnp.where(rest == m2, lane, LANES), axis=-1, keepdims=True)
    e = jnp.exp(m2 - m1)
    g1 = 1.0 / (1.0 + e)
    g2 = e / (1.0 + e)
    onehot = jnp.where((lane == i1) | (lane == i2), 1.0, 0.0)
    r = lax.broadcasted_iota(jnp.int32, (tb, tb), 0)
    c = lax.broadcasted_iota(jnp.int32, (tb, tb), 1)
    lower = jnp.where(c < r, 1.0, 0.0).astype(BF16)
    before = jnp.dot(lower, onehot.astype(BF16), preferred_element_type=F32) + run_ref[...]
    rank1 = jnp.sum(jnp.where(lane == i1, before, 0.0), axis=-1, keepdims=True)
    rank2 = jnp.sum(jnp.where(lane == i2, before, 0.0), axis=-1, keepdims=True)
    total = run_ref[...] + jnp.sum(onehot, axis=0, keepdims=True)
    run_ref[...] = total
    cnt_ref[...] = total
    slab = jnp.where(lane == 0, i1.astype(F32), 0.0)
    slab = jnp.where(lane == 1, i2.astype(F32), slab)
    slab = jnp.where(lane == 2, g1, slab)
    slab = jnp.where(lane == 3, g2, slab)
    slab = jnp.where(lane == 4, rank1, slab)
    slab = jnp.where(lane == 5, rank2, slab)
    slab_ref[...] = slab


def _router(x, w_router):
    T, D = x.shape
    tb = min(ROUTE_TB, T)
    assert T % tb == 0
    w = jnp.zeros((D, LANES), F32).at[:, :N_EXPERTS].set(w_router)
    wh = w.astype(BF16)
    wl = (w - wh.astype(F32)).astype(BF16)
    return pl.pallas_call(
        _router_kernel,
        grid=(T // tb,),
        in_specs=[pl.BlockSpec((tb, D), lambda i: (i, 0)),
                  pl.BlockSpec((D, LANES), lambda i: (0, 0)),
                  pl.BlockSpec((D, LANES), lambda i: (0, 0))],
        out_specs=[pl.BlockSpec((tb, LANES), lambda i: (i, 0)),
                   pl.BlockSpec((1, LANES), lambda i: (0, 0))],
        out_shape=[jax.ShapeDtypeStruct((T, LANES), F32),
                   jax.ShapeDtypeStruct((1, LANES), F32)],
        scratch_shapes=[pltpu.VMEM((1, LANES), F32)],
        compiler_params=_params("arbitrary"),
        name="moe_router",
    )(x, wh, wl)


def _expert_kernel(be_ref, na_ref, tok_ref, tok_nxt_ref, dst_prv_ref, dst_ref, x_hbm,
                   wg_ref, wu_ref, wd_ref, y_hbm, xbuf_ref, xb_ref, acc_ref, ybuf_ref,
                   sem_in, sem_out, *, n_f):
    del be_ref
    i, f = pl.program_id(0), pl.program_id(1)
    n_active = na_ref[0]
    tm = xb_ref.shape[0]
    per = tm // n_f
    slot = lax.rem(i, 2)

    def tile(row):
        return pl.ds(pl.multiple_of(row * SUBLANES, SUBLANES), SUBLANES)

    def gather(tok, t, s):
        return pltpu.make_async_copy(x_hbm.at[tile(tok)], xbuf_ref.at[s, tile(t)], sem_in.at[s])

    def scatter(dst, t, s):
        return pltpu.make_async_copy(ybuf_ref.at[s, tile(t)], y_hbm.at[tile(dst)], sem_out.at[s])

    def gathered(s):
        return pltpu.make_async_copy(x_hbm.at[pl.ds(0, tm * SUBLANES)], xbuf_ref.at[s], sem_in.at[s])

    def scattered(s):
        return pltpu.make_async_copy(ybuf_ref.at[s], y_hbm.at[pl.ds(0, tm * SUBLANES)], sem_out.at[s])

    def for_rows(fn):
        def body(t, carry):
            fn(t)
            return carry
        lax.fori_loop(0, tm, body, 0, unroll=DMA_UNROLL)

    @pl.when(i < n_active)
    def _():
        @pl.when((i == 0) & (f == 0))
        def _():
            for_rows(lambda t: gather(tok_ref[0, 0, t], t, 0).start())
            ybuf_ref[1] = jnp.zeros(ybuf_ref.shape[1:], F32)

        @pl.when(f == 0)
        def _():
            gathered(slot).wait()
            xb_ref[...] = _load_row_tiles(xbuf_ref.at[slot], tm).astype(BF16)
            acc_ref[...] = jnp.zeros_like(acc_ref)

        base = f * per
        for t in range(per):
            gather(tok_nxt_ref[0, 0, base + t], base + t, 1 - slot).start()
            scatter(dst_prv_ref[0, 0, base + t], base + t, 1 - slot).start()
        xb = xb_ref[...]
        gate = jnp.dot(xb, wg_ref[...], preferred_element_type=F32)
        up = jnp.dot(xb, wu_ref[...], preferred_element_type=F32)
        hid = (gate * jax.nn.sigmoid(gate) * up).astype(BF16)
        acc_ref[...] += jnp.dot(hid, wd_ref[...], preferred_element_type=F32)

        @pl.when(f == n_f - 1)
        def _():
            @pl.when(i > 0)
            def _():
                scattered(slot).wait()
            _store_row_tiles(ybuf_ref.at[slot], acc_ref[...])

        @pl.when((i == n_active - 1) & (f == n_f - 1))
        def _():
            for_rows(lambda t: scatter(dst_ref[0, 0, t], t, slot).start())
            gathered(1 - slot).wait()
            scattered(1 - slot).wait()
            scattered(slot).wait()

    @pl.when((i >= n_active) & (f == n_f - 1))
    def _():
        ybuf_ref[0] = jnp.zeros(ybuf_ref.shape[1:], F32)
        for_rows(lambda t: scatter(dst_ref[0, 0, t], t, 0).start())
        scattered(0).wait()


def _experts(x, slot_tok, slot_dst, n_rows_out, blk_exp, n_active, wg, wu, wd, layer):
    D = x.shape[1] * SUBLANES
    P = slot_tok.shape[0]
    F = wg.shape[3]
    tm, tf = MOE_BLOCK, EXP_TF
    assert P % tm == 0 and F % tf == 0 and tm % (F // tf) == 0
    nb, nf = P // tm, F // tf
    toks = slot_tok.reshape(nb, 1, tm)
    spare = n_rows_out - tm + jnp.arange(tm, dtype=jnp.int32)
    dsts = jnp.concatenate([spare, slot_dst]).reshape(nb + 1, 1, tm)
    smem = lambda fn: pl.BlockSpec((1, 1, tm), fn, memory_space=pltpu.SMEM)
    grid_spec = pltpu.PrefetchScalarGridSpec(
        num_scalar_prefetch=2,
        grid=(nb, nf),
        in_specs=[smem(lambda i, f, be, na: (i, 0, 0)),
                  smem(lambda i, f, be, na: (jnp.minimum(i + 1, nb - 1), 0, 0)),
                  smem(lambda i, f, be, na: (i, 0, 0)),
                  smem(lambda i, f, be, na: (i + 1, 0, 0)),
                  pl.BlockSpec(memory_space=pl.ANY),
                  pl.BlockSpec((None, None, D, tf), lambda i, f, be, na: (layer, be[i], 0, f)),
                  pl.BlockSpec((None, None, D, tf), lambda i, f, be, na: (layer, be[i], 0, f)),
                  pl.BlockSpec((None, None, tf, D), lambda i, f, be, na: (layer, be[i], f, 0))],
        out_specs=pl.BlockSpec(memory_space=pl.ANY),
        scratch_shapes=[pltpu.VMEM((2, tm * SUBLANES, LANES), F32), pltpu.VMEM((tm, D), BF16),
                        pltpu.VMEM((tm, D), F32), pltpu.VMEM((2, tm * SUBLANES, LANES), F32),
                        pltpu.SemaphoreType.DMA((2,)), pltpu.SemaphoreType.DMA((2,))],
    )
    return pl.pallas_call(
        functools.partial(_expert_kernel, n_f=nf),
        grid_spec=grid_spec,
        out_shape=jax.ShapeDtypeStruct((n_rows_out * SUBLANES, LANES), F32),
        compiler_params=_params("arbitrary", "arbitrary"),
        name="moe_experts",
    )(blk_exp, n_active, toks, toks, dsts, dsts, x, wg, wu, wd)


def _combine_kernel(y1_ref, y2_ref, slab_ref, x_ref, g_ref, b_ref, o_ref, *, alpha):
    slab = slab_ref[...]
    tb = x_ref.shape[0]
    mix = slab[:, 2:3] * _load_row_tiles(y1_ref, tb) + slab[:, 3:4] * _load_row_tiles(y2_ref, tb)
    o_ref[...] = _layer_norm(alpha * x_ref[...] + mix, g_ref[...], b_ref[...])


def _combine_res_ln(y, slab, x, g, b, alpha):
    T, D = x.shape
    tb = min(COMB_TB, T)
    assert T % tb == 0
    nb = T // tb
    return pl.pallas_call(
        functools.partial(_combine_kernel, alpha=alpha),
        grid=(nb,),
        in_specs=[pl.BlockSpec((tb * SUBLANES, LANES), lambda i: (i, 0)),
                  pl.BlockSpec((tb * SUBLANES, LANES), lambda i: (i + nb, 0)),
                  pl.BlockSpec((tb, LANES), lambda i: (i, 0)),
                  pl.BlockSpec((tb, D), lambda i: (i, 0)),
                  pl.BlockSpec((1, D), lambda i: (0, 0)),
                  pl.BlockSpec((1, D), lambda i: (0, 0))],
        out_specs=pl.BlockSpec((tb, D), lambda i: (i, 0)),
        out_shape=jax.ShapeDtypeStruct((T, D), F32),
        compiler_params=_params("parallel"),
        name="moe_combine_res_ln",
    )(y, y, slab, x, g.reshape(1, D), b.reshape(1, D))


def _slot_table_kernel(d1_ref, d2_ref, fill_end_ref, pad_end_ref, o_ref):
    T = d1_ref.shape[0]

    def place(t, carry):
        o_ref[d1_ref[t]] = t
        o_ref[d2_ref[t]] = T + t
        return carry

    lax.fori_loop(0, T, place, 0, unroll=DMA_UNROLL)

    spare = jnp.int32(2 * T)
    bounds = [(fill_end_ref[e], pad_end_ref[e]) for e in range(N_EXPERTS)]
    bounds.append((pad_end_ref[N_EXPERTS - 1], jnp.int32(o_ref.shape[0])))
    for lo, hi in bounds:
        def fill(s, carry, lo=lo, first=spare):
            o_ref[s] = first + (s - lo)
            return carry
        lax.fori_loop(lo, hi, fill, 0)
        spare = spare + (hi - lo)


def _slot_table(d1, d2, fill_end, pad_end, n_slots):
    smem = pl.BlockSpec(memory_space=pltpu.SMEM)
    n_pad = -(-n_slots // SMEM_1D_TILE) * SMEM_1D_TILE
    assert d1.shape[0] % SMEM_1D_TILE == 0
    return pl.pallas_call(
        _slot_table_kernel,
        in_specs=[smem, smem, smem, smem],
        out_specs=smem,
        out_shape=jax.ShapeDtypeStruct((n_pad,), jnp.int32),
        name="moe_slot_table",
    )(d1, d2, fill_end.astype(jnp.int32), pad_end.astype(jnp.int32))[:n_slots]


def _moe_res_ln(x, x_tiles, w_router, wg, wu, wd, layer, g, b, alpha):
    T, D = x.shape
    slab, counts = _router(x, w_router)
    counts = counts[0, :N_EXPERTS].astype(jnp.int32)
    padded = ((counts + MOE_BLOCK - 1) // MOE_BLOCK) * MOE_BLOCK
    pad_end = jnp.cumsum(padded)
    pad_start = pad_end - padded
    e1 = slab[:, 0].astype(jnp.int32)
    e2 = slab[:, 1].astype(jnp.int32)
    d1 = pad_start[e1] + slab[:, 4].astype(jnp.int32)
    d2 = pad_start[e2] + slab[:, 5].astype(jnp.int32)
    nblk = -(-(2 * T) // MOE_BLOCK) + N_EXPERTS
    n_slots = nblk * MOE_BLOCK
    blk_start = jnp.arange(nblk, dtype=jnp.int32) * MOE_BLOCK
    blk_exp = jnp.minimum(jnp.sum(pad_end[None, :] <= blk_start[:, None], axis=1), N_EXPERTS - 1)
    n_active = (pad_end[-1:] // MOE_BLOCK).astype(jnp.int32)
    slot_dst = _slot_table(d1, d2, pad_start + counts, pad_end, n_slots)
    slot_tok = jnp.where(slot_dst < 2 * T, slot_dst % T, 0)
    n_rows_out = n_slots + MOE_BLOCK
    y = _experts(x_tiles, slot_tok, slot_dst, n_rows_out, blk_exp.astype(jnp.int32), n_active,
                 wg, wu, wd, layer)
    return _combine_res_ln(y, slab, x, g, b, alpha)


def kernel(x, ln_gain, ln_bias, ret_w_in, ret_gn_gain, ret_w_out, att_w_qkv, att_w_out,
           ffn_w_gate, ffn_w_up, ffn_w_down, moe_w_router, moe_w_gate, moe_w_up, moe_w_down):
    B, S, D = x.shape
    depth = ln_gain.shape[0]
    alpha = (2 * depth) ** 0.25
    moe_wg, moe_wu, moe_wd = (w.astype(BF16) for w in (moe_w_gate, moe_w_up, moe_w_down))
    h = x.reshape(B * S, D)
    h_blocks = None
    for i in range(depth):
        j = i // 2
        if i % 2 == 0:
            proj = _matmul(h, ret_w_in[j].astype(BF16), "ret_in_proj")
            gated = _retention_core(proj, B, S, ret_gn_gain[j])
            h = _mm_res_ln(gated, ret_w_out[j].astype(BF16), h, ln_gain[i, 0], ln_bias[i, 0], alpha,
                           "ret_out_res_ln")
            h, h_blocks = _ffn_res_ln(h, ffn_w_gate[j].astype(BF16), ffn_w_up[j].astype(BF16),
                                      ffn_w_down[j].astype(BF16), ln_gain[i, 1], ln_bias[i, 1], alpha)
        else:
            gw = HEADS_PER_GROUP * ATT_HEAD_DIM
            outs = []
            for gi, (win, dil) in enumerate(DIL_PATTERN):
                w_g = jnp.concatenate(
                    [att_w_qkv[j][:, (c * N_GROUPS + gi) * gw:(c * N_GROUPS + gi + 1) * gw] for c in range(3)],
                    axis=1).astype(BF16)
                if dil == 1:
                    qkv = _matmul(h, w_g, f"att_qkv_proj_g{gi}")
                else:
                    if h_blocks is None:
                        h_blocks = jnp.transpose(h.reshape(B * S, D // LANES, LANES), (1, 0, 2))
                    qkv = _matmul_residue_view(h_blocks, w_g, dil, f"att_qkv_proj_g{gi}")
                outs.append(_dilated_group(qkv, B, S, gi, win, dil))
            h_blocks = None
            h, h_tiles = _attn_merge_res_ln(outs, att_w_out[j].astype(BF16), h, ln_gain[i, 0], ln_bias[i, 0],
                                            alpha)
            h = _moe_res_ln(h, h_tiles, moe_w_router[j], moe_wg, moe_wu, moe_wd, j,
                            ln_gain[i, 1], ln_bias[i, 1], alpha)
    return h.reshape(B, S, D)
```
